```python
import math
import jax, jax.numpy as jnp
from jax import lax
import numpy as np

D_MODEL = 1024
BATCH = 8
SEQ = 2048
DEPTH = 2

CHUNK = 64

SSD_D_INNER = D_MODEL
SSD_HEADDIM = 64
SSD_HEADS = SSD_D_INNER // SSD_HEADDIM
SSD_GROUPS = 2
SSD_STATE = 128
SSD_CONV = 4
SSD_XBC = SSD_D_INNER + 2 * SSD_GROUPS * SSD_STATE

RWKV_DIM = D_MODEL
RWKV_HEAD = 64
RWKV_HEADS = RWKV_DIM // RWKV_HEAD
RWKV_W_LORA = 64
RWKV_A_LORA = 64
RWKV_G_LORA = 128
RWKV_COLS = 3 * RWKV_DIM + RWKV_W_LORA + RWKV_A_LORA + RWKV_G_LORA
RWKV_GN_EPS = 64e-5

HGRN_DIM = D_MODEL
HGRN_EXPAND = 128
HGRN_HEADS = HGRN_DIM // HGRN_EXPAND
HGRN_VDIM = HGRN_DIM // HGRN_HEADS

IN_SIZES = (SSD_D_INNER, SSD_XBC, SSD_HEADS, RWKV_COLS, 4 * HGRN_DIM, 3 * D_MODEL)
N_IN = sum(IN_SIZES)

N_EXPERTS = 16
N_GROUPS = 4
EXPERTS_PER_GROUP = N_EXPERTS // N_GROUPS
TOPK_GROUPS = 1
TOP_K = 2
D_EXPERT = 512

DN_ALPHA = (2 * DEPTH) ** 0.25
DN_BETA = (8 * DEPTH) ** -0.25
LN_EPS = 1e-5
RMS_EPS = 1e-6

kernel_name = "hybrid_ssd_rwkv7_hgrn2_moe_deepnorm"


def _split(x, sizes):
    offs, acc = [], 0
    for s in sizes[:-1]:
        acc += s
        offs.append(acc)
    return jnp.split(x, offs, axis=-1)


def _layernorm(x, g, b):
    xf = x.astype(jnp.float32)
    mu = jnp.mean(xf, -1, keepdims=True)
    var = jnp.mean(jnp.square(xf - mu), -1, keepdims=True)
    return ((xf - mu) * lax.rsqrt(var + LN_EPS) * g + b).astype(x.dtype)


def _rmsnorm(x, w):
    xf = x.astype(jnp.float32)
    return xf * lax.rsqrt(jnp.mean(xf * xf, -1, keepdims=True) + RMS_EPS) * w


def _causal_dwconv(x, w, b):
    k, c = w.shape
    y = lax.conv_general_dilated(x, w[:, None, :], window_strides=(1,), padding=[(k - 1, 0)],
                                 dimension_numbers=("NWC", "WIO", "NWC"), feature_group_count=c)
    return y + b


def _chunk_scan(states, decay):
    def step(s, inp):
        st, dec = inp
        return s * dec + st, s
    s0 = jnp.zeros_like(states[:, 0])
    _, s_in = lax.scan(step, s0, (jnp.moveaxis(states, 1, 0), jnp.moveaxis(decay, 1, 0)))
    return jnp.moveaxis(s_in, 0, 1)


def _ssd(z, xbc, dt_raw, conv_w, conv_b, dt_bias, a_log, d_skip, norm_w):
    f32 = jnp.float32
    b, L, _ = z.shape
    nc = L // CHUNK
    hg = SSD_HEADS // SSD_GROUPS
    xbc = jax.nn.silu(_causal_dwconv(xbc, conv_w, conv_b))
    xs, bm, cm = _split(xbc, (SSD_D_INNER, SSD_GROUPS * SSD_STATE, SSD_GROUPS * SSD_STATE))
    xs = xs.reshape(b, nc, CHUNK, SSD_GROUPS, hg, SSD_HEADDIM).astype(f32)
    bm = bm.reshape(b, nc, CHUNK, SSD_GROUPS, SSD_STATE).astype(f32)
    cm = cm.reshape(b, nc, CHUNK, SSD_GROUPS, SSD_STATE).astype(f32)
    dt = jax.nn.softplus(dt_raw.astype(f32) + dt_bias)
    a = -jnp.exp(a_log.astype(f32))
    da = (dt * a).reshape(b, nc, CHUNK, SSD_GROUPS, hg)
    dt = dt.reshape(b, nc, CHUNK, SSD_GROUPS, hg)
    acs = jnp.cumsum(da, axis=2)
    causal = jnp.tril(jnp.ones((CHUNK, CHUNK), bool))
    seg = acs[:, :, :, None] - acs[:, :, None, :]
    decay = jnp.exp(jnp.where(causal[:, :, None, None], seg, -jnp.inf))
    cb = jnp.einsum('bclgn,bcsgn->bclsg', cm, bm)
    wts = cb[..., None] * decay * dt[:, :, None]
    y_diag = jnp.einsum('bclsgh,bcsghp->bclghp', wts, xs)
    to_end = jnp.exp(acs[:, :, -1:] - acs) * dt
    states = jnp.einsum('bclgn,bclgh,bclghp->bcghpn', bm, to_end, xs)
    s_in = _chunk_scan(states, jnp.exp(acs[:, :, -1])[..., None, None])
    y_off = jnp.einsum('bclgn,bcghpn,bclgh->bclghp', cm, s_in, jnp.exp(acs))
    y = y_diag + y_off + xs * d_skip.astype(f32).reshape(SSD_GROUPS, hg)[:, :, None]
    y = y.reshape(b, L, SSD_D_INNER) * jax.nn.silu(z.astype(f32))
    y = _rmsnorm(y.reshape(b, L, SSD_GROUPS, -1), norm_w.reshape(SSD_GROUPS, -1))
    return y.reshape(b, L, SSD_D_INNER).astype(z.dtype)


def _rwkv7(feat, mu, w0, w2, a0, a2, g2, k_k, k_a, r_k, ln_w, ln_b):
    f32 = jnp.float32
    b, L, _ = feat.shape
    prev = jnp.pad(feat, ((0, 0), (1, 0), (0, 0)))[:, :-1]
    feat = feat + (prev - feat) * mu
    r, k, v, xw, xa, xg = _split(feat, (RWKV_DIM, RWKV_DIM, RWKV_DIM, RWKV_W_LORA, RWKV_A_LORA, RWKV_G_LORA))
    w = -jax.nn.softplus(-(w0 + jnp.tanh(xw) @ w2)) - 0.5
    decay = jnp.exp(-jnp.exp(w.astype(f32)))
    a = jax.nn.sigmoid((a0 + xa @ a2).astype(f32))
    g = jax.nn.sigmoid(xg) @ g2
    hs = lambda t: t.astype(f32).reshape(b, L, RWKV_HEADS, RWKV_HEAD)
    kk = hs(k * k_k)
    kk = kk / jnp.maximum(jnp.sqrt(jnp.sum(kk * kk, -1, keepdims=True)), 1e-12)
    k = k.astype(f32) * (1.0 + (a - 1.0) * k_a)
    r, k, v, decay, a = hs(r), hs(k), hs(v), hs(decay), hs(a)
    a_vec, b_vec = -kk, kk * a

    def step(s, inp):
        r_t, w_t, k_t, v_t, a_t, b_t = inp
        sa = jnp.einsum('bhvk,bhk->bhv', s, a_t)
        s = s * w_t[:, :, None, :] + sa[..., None] * b_t[:, :, None, :] + v_t[..., None] * k_t[:, :, None, :]
        return s, jnp.einsum('bhvk,bhk->bhv', s, r_t)

    tm = lambda t: jnp.moveaxis(t, 1, 0)
    s0 = jnp.zeros((b, RWKV_HEADS, RWKV_HEAD, RWKV_HEAD), f32)
    _, y = lax.scan(step, s0, (tm(r), tm(decay), tm(k), tm(v), tm(a_vec), tm(b_vec)))
    y = jnp.moveaxis(y, 0, 1)
    mean = jnp.mean(y, -1, keepdims=True)
    var = jnp.mean(jnp.square(y - mean), -1, keepdims=True)
    y = ((y - mean) * lax.rsqrt(var + RWKV_GN_EPS)).reshape(b, L, RWKV_DIM) * ln_w + ln_b
    bonus = jnp.sum(r * k * r_k.astype(f32), -1, keepdims=True) * v
    y = (y + bonus.reshape(b, L, RWKV_DIM)) * g
    return y.astype(feat.dtype)


def _hgrn2(feat, lb, norm_w):
    f32 = jnp.float32
    b, L, _ = feat.shape
    nc = L // CHUNK
    q, f, i, g = jnp.split(feat, 4, axis=-1)
    q = jax.nn.silu(q.astype(f32))
    forget = lb + (1.0 - lb) * jax.nn.sigmoid(f.astype(f32))
    k = 1.0 - forget
    rs = lambda t, d: t.reshape(b, nc, CHUNK, HGRN_HEADS, d)
    q, k = rs(q, HGRN_EXPAND), rs(k, HGRN_EXPAND)
    logf = rs(jnp.log(forget), HGRN_EXPAND)
    v = rs(i.astype(f32), HGRN_VDIM)
    bc = jnp.cumsum(logf, axis=2)
    ref = bc[:, :, CHUNK // 2:CHUNK // 2 + 1]
    att = jnp.einsum('bclhd,bcshd->bchls', q * jnp.exp(bc - ref), k * jnp.exp(ref - bc))
    causal = jnp.tril(jnp.ones((CHUNK, CHUNK), bool))
    att = jnp.where(causal, att, 0.0)
    o_intra = jnp.einsum('bchls,bcshe->bclhe', att, v)
    states = jnp.einsum('bclhd,bclhe->bchde', k * jnp.exp(bc[:, :, -1:] - bc), v)
    s_in = _chunk_scan(states, jnp.exp(bc[:, :, -1])[..., None])
    o_inter = jnp.einsum('bclhd,bchde->bclhe', q * jnp.exp(bc), s_in)
    o = _rmsnorm(o_intra + o_inter, norm_w).reshape(b, L, HGRN_DIM)
    o = o * jax.nn.sigmoid(g.astype(f32))
    return o.astype(feat.dtype)


def _moe(x, router_w, router_bias, w_gate, w_up, w_down):
    f32 = jnp.float32
    b, L, d = x.shape
    t = x.reshape(-1, d)
    probs = jax.nn.softmax((t @ router_w).astype(f32), axis=-1)
    sel = probs + router_bias.astype(f32)
    grp = sel.reshape(-1, N_GROUPS, EXPERTS_PER_GROUP)
    grp_score = jnp.sum(lax.top_k(grp, TOP_K)[0], -1)
    _, gidx = lax.top_k(grp_score, TOPK_GROUPS)
    gmask = jnp.sum(jax.nn.one_hot(gidx, N_GROUPS, dtype=f32), -2) > 0
    masked = jnp.where(jnp.repeat(gmask, EXPERTS_PER_GROUP, axis=-1), sel, -jnp.inf)
    _, eidx = lax.top_k(masked, TOP_K)
    w_sel = jnp.take_along_axis(probs, eidx, -1)
    w_sel = w_sel / jnp.sum(w_sel, -1, keepdims=True)
    gates = jnp.sum(jax.nn.one_hot(eidx, N_EXPERTS, dtype=f32) * w_sel[..., None], -2)
    out = jnp.zeros(t.shape, f32)
    for e in range(N_EXPERTS):
        h = jax.nn.silu(t @ w_gate[e]) * (t @ w_up[e])
        out = out + gates[:, e:e + 1] * (h @ w_down[e]).astype(f32)
    return out.reshape(b, L, d).astype(x.dtype)


def setup_inputs(seed: int = 0) -> dict:
    key = jax.random.key(seed)
    ks = iter(jax.random.split(key, 48))
    f32 = jnp.float32

    def nrm(shape, scale):
        return jax.random.normal(next(ks), shape, f32) * scale

    def unif(shape, lo, hi):
        return jax.random.uniform(next(ks), shape, f32, lo, hi)

    nl = DEPTH
    dt0 = jnp.exp(unif((nl, SSD_HEADS), math.log(1e-3), math.log(1e-1)))
    return {
        "x": nrm((BATCH, SEQ, D_MODEL), 1.0),
        "ln_in_g": 1.0 + nrm((D_MODEL,), 0.02),
        "ln_in_b": nrm((D_MODEL,), 0.02),
        "w_in": nrm((nl, D_MODEL, N_IN), D_MODEL ** -0.5),
        "ssd_conv_w": nrm((nl, SSD_CONV, SSD_XBC), SSD_CONV ** -0.5),
        "ssd_conv_b": nrm((nl, SSD_XBC), 0.02),
        "ssd_dt_bias": dt0 + jnp.log(-jnp.expm1(-dt0)),
        "ssd_a_log": jnp.log(unif((nl, SSD_HEADS), 1.0, 16.0)),
        "ssd_d": 1.0 + nrm((nl, SSD_HEADS), 0.1),
        "ssd_norm_w": 1.0 + nrm((nl, SSD_D_INNER), 0.02),
        "rwkv_mu": unif((nl, RWKV_COLS), 0.0, 1.0),
        "rwkv_w0": unif((nl, RWKV_DIM), -6.0, -1.0),
        "rwkv_w2": nrm((nl, RWKV_W_LORA, RWKV_DIM), 0.1 * RWKV_W_LORA ** -0.5),
        "rwkv_a0": nrm((nl, RWKV_DIM), 0.1),
        "rwkv_a2": nrm((nl, RWKV_A_LORA, RWKV_DIM), 0.1 * RWKV_A_LORA ** -0.5),
        "rwkv_g2": nrm((nl, RWKV_G_LORA, RWKV_DIM), RWKV_G_LORA ** -0.5),
        "rwkv_k_k": 0.85 + nrm((nl, RWKV_DIM), 0.02),
        "rwkv_k_a": 1.0 + nrm((nl, RWKV_DIM), 0.02),
        "rwkv_r_k": nrm((nl, RWKV_HEADS, RWKV_HEAD), 0.1),
        "rwkv_ln_w": 1.0 + nrm((nl, RWKV_DIM), 0.02),
        "rwkv_ln_b": nrm((nl, RWKV_DIM), 0.02),
        "hgrn_lb": nrm((nl, HGRN_DIM), 0.5),
        "hgrn_norm_w": 1.0 + nrm((nl, HGRN_VDIM), 0.02),
        "w_br_ssd": nrm((nl, SSD_D_INNER, D_MODEL), DN_BETA * SSD_D_INNER ** -0.5),
        "w_br_rwkv": nrm((nl, RWKV_DIM, D_MODEL), DN_BETA * RWKV_DIM ** -0.5),
        "w_br_hgrn": nrm((nl, HGRN_DIM, D_MODEL), DN_BETA * HGRN_DIM ** -0.5),
        "w_out": nrm((nl, D_MODEL, D_MODEL), DN_BETA * D_MODEL ** -0.5),
        "ln1_g": 1.0 + nrm((nl, D_MODEL), 0.02),
        "ln1_b": nrm((nl, D_MODEL), 0.02),
        "router_w": nrm((D_MODEL, N_EXPERTS), D_MODEL ** -0.5),
        "router_bias": nrm((N_EXPERTS,), 0.01),
        "exp_w_gate": nrm((nl, N_EXPERTS, D_MODEL, D_EXPERT), DN_BETA * D_MODEL ** -0.5),
        "exp_w_up": nrm((nl, N_EXPERTS, D_MODEL, D_EXPERT), DN_BETA * D_MODEL ** -0.5),
        "exp_w_down": nrm((nl, N_EXPERTS, D_EXPERT, D_MODEL), DN_BETA * D_EXPERT ** -0.5),
        "ln2_g": 1.0 + nrm((nl, D_MODEL), 0.02),
        "ln2_b": nrm((nl, D_MODEL), 0.02),
    }


def reference(x, ln_in_g, ln_in_b, w_in, ssd_conv_w, ssd_conv_b, ssd_dt_bias, ssd_a_log, ssd_d, ssd_norm_w,
              rwkv_mu, rwkv_w0, rwkv_w2, rwkv_a0, rwkv_a2, rwkv_g2, rwkv_k_k, rwkv_k_a, rwkv_r_k, rwkv_ln_w,
              rwkv_ln_b, hgrn_lb, hgrn_norm_w, w_br_ssd, w_br_rwkv, w_br_hgrn, w_out, ln1_g, ln1_b, router_w,
              router_bias, exp_w_gate, exp_w_up, exp_w_down, ln2_g, ln2_b):
    h = _layernorm(x, ln_in_g, ln_in_b)
    lsm = jax.nn.softmax(hgrn_lb.astype(jnp.float32), axis=0)
    lower_bounds = jnp.cumsum(lsm, axis=0) - lsm[0]
    for l in range(DEPTH):
        feats = h @ w_in[l]
        z, xbc, dt_raw, f_rwkv, f_hgrn, gates = _split(feats, IN_SIZES)
        y_a = _ssd(z, xbc, dt_raw, ssd_conv_w[l], ssd_conv_b[l], ssd_dt_bias[l], ssd_a_log[l], ssd_d[l],
                   ssd_norm_w[l])
        y_b = _rwkv7(f_rwkv, rwkv_mu[l], rwkv_w0[l], rwkv_w2[l], rwkv_a0[l], rwkv_a2[l], rwkv_g2[l],
                     rwkv_k_k[l], rwkv_k_a[l], rwkv_r_k[l], rwkv_ln_w[l], rwkv_ln_b[l])
        y_c = _hgrn2(f_hgrn, lower_bounds[l], hgrn_norm_w[l])
        g_a, g_b, g_c = jnp.split(gates, 3, axis=-1)
        merged = (jax.nn.sigmoid(g_a) * (y_a @ w_br_ssd[l])
                  + jax.nn.sigmoid(g_b) * (y_b @ w_br_rwkv[l])
                  + jax.nn.sigmoid(g_c) * (y_c @ w_br_hgrn[l]))
        h = _layernorm(DN_ALPHA * h + merged @ w_out[l], ln1_g[l], ln1_b[l])
        moe = _moe(h, router_w, router_bias, exp_w_gate[l], exp_w_up[l], exp_w_down[l])
        h = _layernorm(DN_ALPHA * h + moe, ln2_g[l], ln2_b[l])
    return h
```

```python
import functools
import math

import jax
import jax.numpy as jnp
from jax import lax
from jax.experimental import pallas as pl
from jax.experimental.pallas import tpu as pltpu

F32 = jnp.float32
BF16 = jnp.bfloat16

D_MODEL = 1024
DEPTH = 2
CHUNK = 64
LANES = 128

SSD_HEADS = 16
SSD_HEADDIM = 64
SSD_GROUPS = 2
SSD_STATE = 128
SSD_CONV = 4
SSD_XBC = D_MODEL + 2 * SSD_GROUPS * SSD_STATE

RWKV_HEADS = 16
RWKV_HEAD = 64
RWKV_COLS = 3 * D_MODEL + 64 + 64 + 128
RWKV_GN_EPS = 64e-5

HGRN_HEADS = 8
HGRN_EXPAND = 128

IN_SIZES = (D_MODEL, SSD_XBC, SSD_HEADS, RWKV_COLS, 4 * D_MODEL, 3 * D_MODEL)

N_EXPERTS = 16
N_GROUPS = 4
EXPERTS_PER_GROUP = 4
D_EXPERT = 512

DN_ALPHA = (2 * DEPTH) ** 0.25
LN_EPS = 1e-5
RMS_EPS = 1e-6

VMEM_LIMIT = 56 * 1024 * 1024

_NN = (((1,), (0,)), ((), ()))
_NT = (((1,), (1,)), ((), ()))
_TN = (((0,), (0,)), ((), ()))


def _dot(a, b, dims=_NN):
    return lax.dot_general(a, b, dims, preferred_element_type=F32)


def _bdot(a, b, dims=_NN):
    return _dot(a.astype(BF16), b.astype(BF16), dims)


def _split3(x):
    x1 = x.astype(BF16)
    r1 = x - x1.astype(F32)
    x2 = r1.astype(BF16)
    x3 = (r1 - x2.astype(F32)).astype(BF16)
    return x1, x2, x3


def _sel_l(sel, x):
    x1, x2, x3 = _split3(x)
    return _dot(sel, x1) + _dot(sel, x2) + _dot(sel, x3)


def _sel_r(x, sel):
    x1, x2, x3 = _split3(x)
    return _dot(x1, sel) + _dot(x2, sel) + _dot(x3, sel)


def _sigmoid(x):
    return 1.0 / (1.0 + jnp.exp(-x))


def _silu(x):
    return x * _sigmoid(x)


def _softplus(x):
    return jnp.maximum(x, 0.0) + jnp.log1p(jnp.exp(-jnp.abs(x)))


def _layernorm_rows(x, g, b):
    mu = jnp.mean(x, axis=-1, keepdims=True)
    xc = x - mu
    var = jnp.mean(xc * xc, axis=-1, keepdims=True)
    return xc * lax.rsqrt(var + LN_EPS) * g + b


def _tri(n, strict=False):
    r = lax.broadcasted_iota(jnp.int32, (n, n), 0)
    c = lax.broadcasted_iota(jnp.int32, (n, n), 1)
    return (c < r) if strict else (c <= r)


def _cparams(sem):
    return pltpu.CompilerParams(dimension_semantics=sem, vmem_limit_bytes=VMEM_LIMIT)


def _full(shape):
    return pl.BlockSpec(shape, lambda *_: (0,) * len(shape))


def _ln_kernel(x_ref, g_ref, b_ref, o_ref, ob_ref):
    y = _layernorm_rows(x_ref[...], g_ref[...], b_ref[...])
    o_ref[...] = y
    ob_ref[...] = y.astype(BF16)


def _layernorm(x, g, b, tm=512):
    t, d = x.shape
    row = pl.BlockSpec((tm, d), lambda i: (i, 0))
    return pl.pallas_call(
        _ln_kernel,
        grid=(t // tm,),
        in_specs=[row, _full((1, d)), _full((1, d))],
        out_specs=[row, row],
        out_shape=[jax.ShapeDtypeStruct((t, d), F32), jax.ShapeDtypeStruct((t, d), BF16)],
        compiler_params=_cparams(("parallel",)),
        name="layernorm_in",
    )(x, g.reshape(1, d), b.reshape(1, d))


def _proj_kernel(x_ref, w_ref, o_ref):
    o_ref[...] = _dot(x_ref[...], w_ref[...]).astype(o_ref.dtype)


def _project(x_bf, w_bf, name, tm=1024):
    t, k = x_bf.shape
    n = w_bf.shape[1]
    tm = min(tm, t)
    tn = n
    for cand in range(min(n, 2048), 0, -LANES):
        if n % cand == 0:
            tn = cand
            break
    return pl.pallas_call(
        _proj_kernel,
        grid=(n // tn, t // tm),
        in_specs=[pl.BlockSpec((tm, k), lambda j, i: (i, 0)), pl.BlockSpec((k, tn), lambda j, i: (0, j))],
        out_specs=pl.BlockSpec((tm, tn), lambda j, i: (i, j)),
        out_shape=jax.ShapeDtypeStruct((t, n), F32),
        compiler_params=_cparams(("parallel", "parallel")),
        name=name,
    )(x_bf, w_bf)


def _head_expand(n_heads, width):
    h = lax.broadcasted_iota(jnp.int32, (LANES, n_heads * width), 0)
    c = lax.broadcasted_iota(jnp.int32, (LANES, n_heads * width), 1)
    return (c // width == h).astype(BF16)


def _ssd_kernel(z_ref, xbc_ref, dt_ref, cw_ref, cb_ref, dtb_ref, a_ref, dsk_ref, nw_ref, e_ref,
                o_ref, pad_ref, st_ref):
    c = pl.program_id(1)
    n_pairs = SSD_HEADS // 2

    @pl.when(c == 0)
    def _():
        pad_ref[0:8, :] = jnp.zeros((8, SSD_XBC), F32)
        st_ref[...] = jnp.zeros(st_ref.shape, F32)

    pad_ref[8:8 + CHUNK, :] = xbc_ref[...]
    acc = jnp.broadcast_to(cb_ref[...], (CHUNK, SSD_XBC))
    for j in range(SSD_CONV):
        acc = acc + cw_ref[j:j + 1, :] * pad_ref[pl.ds(8 - (SSD_CONV - 1) + j, CHUNK), :]
    pad_ref[0:8, :] = pad_ref[CHUNK:CHUNK + 8, :]
    xbc = _silu(acc)
    xs = xbc[:, :D_MODEL]

    e_mat = e_ref[...]
    dt = _softplus(dt_ref[...] + dtb_ref[...])
    da = dt * a_ref[...]
    tri = _tri(CHUNK).astype(BF16)
    acs = _sel_l(tri, da)
    acs_e = _sel_r(acs, e_mat)
    dt_e = _sel_r(dt, e_mat)

    li = lax.broadcasted_iota(jnp.int32, (CHUNK, D_MODEL), 0)
    si = lax.broadcasted_iota(jnp.int32, (CHUNK, D_MODEL), 1) & (SSD_HEADDIM - 1)
    acs_row = jnp.sum(jnp.where(li == si, acs_e, 0.0), axis=0, keepdims=True)
    decay = jnp.exp(jnp.where(si <= li, acs_e - acs_row, -jnp.inf))
    xs_dt = xs * dt_e
    exp_acs = jnp.exp(acs_e)
    acs_last = acs_e[CHUNK - 1:CHUNK, :]
    to_end = jnp.exp(acs_last - acs_e)
    exp_last = jnp.exp(acs_last)

    lane = lax.broadcasted_iota(jnp.int32, (CHUNK, LANES), 1)
    m0 = lane < SSD_HEADDIM

    y_parts = []
    for j in range(n_pairs):
        g = j // (n_pairs // SSD_GROUPS)
        sl = slice(j * LANES, (j + 1) * LANES)
        bm = xbc[:, D_MODEL + g * SSD_STATE:D_MODEL + (g + 1) * SSD_STATE]
        cm = xbc[:, D_MODEL + (SSD_GROUPS + g) * SSD_STATE:D_MODEL + (SSD_GROUPS + g + 1) * SSD_STATE]
        cb2 = _bdot(cm, jnp.concatenate([bm, bm], axis=0), _NT)
        wts = cb2 * decay[:, sl]
        xp = xs_dt[:, sl]
        xbd = jnp.concatenate([jnp.where(m0, xp, 0.0), jnp.where(m0, 0.0, xp)], axis=0)
        st = st_ref[j]
        y = _bdot(wts, xbd) + _bdot(cm, st) * exp_acs[:, sl]
        st_ref[j] = st * exp_last[:, sl] + _bdot(bm, xp * to_end[:, sl], _TN)
        y_parts.append(y)
    y = jnp.concatenate(y_parts, axis=1) + xs * dsk_ref[...]
    y = y * _silu(z_ref[...])
    gw = D_MODEL // SSD_GROUPS
    for g in range(SSD_GROUPS):
        yg = y[:, g * gw:(g + 1) * gw]
        ms = jnp.mean(yg * yg, axis=-1, keepdims=True)
        o_ref[:, g * gw:(g + 1) * gw] = yg * lax.rsqrt(ms + RMS_EPS) * nw_ref[:, g * gw:(g + 1) * gw]


def _ssd(z, xbc, dt_raw, conv_w, conv_b, dt_bias, a_log, d_skip, norm_w, batch):
    t = z.shape[0]
    nc = t // batch // CHUNK
    pad16 = lambda v: jnp.pad(v.astype(F32), (0, LANES - SSD_HEADS)).reshape(1, LANES)
    row = lambda w: pl.BlockSpec((CHUNK, w), lambda b, c: (b * nc + c, 0))
    return pl.pallas_call(
        _ssd_kernel,
        grid=(batch, nc),
        in_specs=[row(D_MODEL), row(SSD_XBC), row(LANES), _full((SSD_CONV, SSD_XBC)), _full((1, SSD_XBC)),
                  _full((1, LANES)), _full((1, LANES)), _full((1, D_MODEL)), _full((1, D_MODEL)),
                  _full((LANES, D_MODEL))],
        out_specs=row(D_MODEL),
        out_shape=jax.ShapeDtypeStruct((t, D_MODEL), F32),
        scratch_shapes=[pltpu.VMEM((CHUNK + 8, SSD_XBC), F32),
                        pltpu.VMEM((SSD_HEADS // 2, SSD_STATE, LANES), F32)],
        compiler_params=_cparams(("parallel", "arbitrary")),
        name="ssd",
    )(z, xbc, dt_raw, conv_w, conv_b.reshape(1, -1), pad16(dt_bias), pad16(-jnp.exp(a_log.astype(F32))),
      jnp.repeat(d_skip.astype(F32), SSD_HEADDIM).reshape(1, -1), norm_w.reshape(1, -1),
      _head_expand(SSD_HEADS, SSD_HEADDIM))


def _hgrn_kernel(f_ref, lb_ref, nw_ref, o_ref, st_ref):
    c = pl.program_id(1)

    @pl.when(c == 0)
    def _():
        st_ref[...] = jnp.zeros(st_ref.shape, F32)

    d = D_MODEL
    lb = lb_ref[...]
    q = _silu(f_ref[:, 0:d])
    forget = lb + (1.0 - lb) * _sigmoid(f_ref[:, d:2 * d])
    k = 1.0 - forget
    v = f_ref[:, 2 * d:3 * d]
    bc = _sel_l(_tri(CHUNK).astype(BF16), jnp.log(forget))
    mid = bc[CHUNK // 2:CHUNK // 2 + 1, :]
    last = bc[CHUNK - 1:CHUNK, :]
    qe = q * jnp.exp(bc - mid)
    ke = k * jnp.exp(mid - bc)
    qd = q * jnp.exp(bc)
    kd = k * jnp.exp(last - bc)
    w_last = jnp.exp(last)
    causal = _tri(CHUNK)
    for h in range(HGRN_HEADS):
        sl = slice(h * HGRN_EXPAND, (h + 1) * HGRN_EXPAND)
        att = jnp.where(causal, _bdot(qe[:, sl], ke[:, sl], _NT), 0.0)
        st = st_ref[h]
        o = _bdot(att, v[:, sl]) + _bdot(qd[:, sl], st, _NT)
        st_ref[h] = st * w_last[:, sl] + _bdot(v[:, sl], kd[:, sl], _TN)
        o = o * lax.rsqrt(jnp.mean(o * o, axis=-1, keepdims=True) + RMS_EPS) * nw_ref[...]
        o_ref[:, sl] = o * _sigmoid(f_ref[:, 3 * d + h * HGRN_EXPAND:3 * d + (h + 1) * HGRN_EXPAND])


def _hgrn(feat, lb, norm_w, batch):
    t = feat.shape[0]
    nc = t // batch // CHUNK
    row = lambda w: pl.BlockSpec((CHUNK, w), lambda b, c: (b * nc + c, 0))
    return pl.pallas_call(
        _hgrn_kernel,
        grid=(batch, nc),
        in_specs=[row(4 * D_MODEL), _full((1, D_MODEL)), _full((1, HGRN_EXPAND))],
        out_specs=row(D_MODEL),
        out_shape=jax.ShapeDtypeStruct((t, D_MODEL), F32),
        scratch_shapes=[pltpu.VMEM((HGRN_HEADS, HGRN_EXPAND, HGRN_EXPAND), F32)],
        compiler_params=_cparams(("parallel", "arbitrary")),
        name="hgrn2",
    )(feat, lb.reshape(1, -1), norm_w.reshape(1, -1))


def _rwkv_kernel(f_ref, mu_ref, w0_ref, w2_ref, a0_ref, a2_ref, g2_ref, kk_ref, ka_ref, rk_ref, lnw_ref,
                 lnb_ref, e_ref, et_ref, o_ref, pad_ref, st_ref, y_ref):
    c = pl.program_id(1)
    d = D_MODEL
    n_pairs = RWKV_HEADS // 2

    @pl.when(c == 0)
    def _():
        pad_ref[0:8, :] = jnp.zeros((8, RWKV_COLS), F32)
        st_ref[...] = jnp.zeros(st_ref.shape, F32)

    pad_ref[8:8 + CHUNK, :] = f_ref[...]
    prev = pad_ref[pl.ds(7, CHUNK), :]
    pad_ref[0:8, :] = pad_ref[CHUNK:CHUNK + 8, :]
    cur = f_ref[...]
    x = cur + (prev - cur) * mu_ref[...]
    r, k, v = x[:, 0:d], x[:, d:2 * d], x[:, 2 * d:3 * d]
    lora_in = x[:, 3 * d:3 * d + LANES]
    xg = x[:, 3 * d + LANES:3 * d + 2 * LANES]

    e_mat, et_mat = e_ref[...], et_ref[...]
    w = -_softplus(-(w0_ref[...] + _bdot(jnp.tanh(lora_in), w2_ref[...]))) - 0.5
    ld = -jnp.exp(w)
    a = _sigmoid(a0_ref[...] + _bdot(lora_in, a2_ref[...]))
    gate = _bdot(_sigmoid(xg), g2_ref[...])
    kk = k * kk_ref[...]
    nrm = jnp.maximum(jnp.sqrt(_sel_r(kk * kk, et_mat)), 1e-12)
    kk = kk * _sel_r(1.0 / nrm, e_mat)
    k2 = k * (1.0 + (a - 1.0) * ka_ref[...])
    cw = _sel_l(_tri(CHUNK).astype(BF16), ld)
    e_neg = jnp.exp(-cw)
    a_t = -kk * jnp.exp(cw - ld)
    r_t = r * jnp.exp(cw)
    b_t = kk * a * e_neg
    k_t = k2 * e_neg
    w_last = jnp.exp(cw[CHUNK - 1:CHUNK, :])

    lane = lax.broadcasted_iota(jnp.int32, (CHUNK, LANES), 1)
    m0 = lane < RWKV_HEAD
    row2 = lax.broadcasted_iota(jnp.int32, (LANES, LANES), 0)
    col2 = lax.broadcasted_iota(jnp.int32, (LANES, LANES), 1)
    top = row2 < CHUNK
    same = (row2 < CHUNK) == (col2 < RWKV_HEAD)
    bd_strict = same & ((col2 & (CHUNK - 1)) < (row2 & (CHUNK - 1)))
    bd_incl = same & ((col2 & (CHUNK - 1)) <= (row2 & (CHUNK - 1)))

    def stack(p):
        return jnp.concatenate([jnp.where(m0, p, 0.0), jnp.where(m0, 0.0, p)], axis=0)

    for j in range(n_pairs):
        sl = slice(j * LANES, (j + 1) * LANES)
        s1, s2, vst = stack(a_t[:, sl]), stack(r_t[:, sl]), stack(v[:, sl])
        qmat = jnp.concatenate([b_t[:, sl], k_t[:, sl]], axis=0)
        gm = _bdot(s1, qmat, _NT)
        hm = _bdot(s2, qmat, _NT)
        gr = pltpu.roll(gm, CHUNK, 1)
        hr = pltpu.roll(hm, CHUNK, 1)
        n_ab = jnp.where(bd_strict, jnp.where(top, gm, gr), 0.0)
        a_ak = jnp.where(bd_strict, jnp.where(top, gr, gm), 0.0)
        a_rb = jnp.where(bd_incl, jnp.where(top, hm, hr), 0.0)
        a_rk = jnp.where(bd_incl, jnp.where(top, hr, hm), 0.0)
        mt = st_ref[j]
        xm = _bdot(s1, mt, _NT) + _bdot(a_ak, vst)
        tp = n_ab
        pw = n_ab
        for _ in range(int(math.log2(CHUNK)) - 1):
            pw = _bdot(pw, pw)
            tp = tp + pw + _bdot(tp, pw)
        u = xm + _bdot(tp, xm)
        yst = _bdot(s2, mt, _NT) + _bdot(a_rb, u) + _bdot(a_rk, vst)
        y_ref[:, sl] = yst[:CHUNK] + yst[CHUNK:]
        uv = jnp.concatenate([u[:CHUNK] + u[CHUNK:], v[:, sl]], axis=0)
        upd = jnp.where(same, _bdot(uv, qmat, _TN), 0.0)
        st_ref[j] = (mt + upd) * w_last[:, sl]

    y = y_ref[...]
    inv_n = 1.0 / RWKV_HEAD
    yc = y - _sel_r(_sel_r(y, et_mat) * inv_n, e_mat)
    rs = lax.rsqrt(_sel_r(yc * yc, et_mat) * inv_n + RWKV_GN_EPS)
    yn = yc * _sel_r(rs, e_mat) * lnw_ref[...] + lnb_ref[...]
    bonus = _sel_r(_sel_r(r * k2 * rk_ref[...], et_mat), e_mat) * v
    o_ref[...] = (yn + bonus) * gate


def _rwkv(feat, mu, w0, w2, a0, a2, g2, k_k, k_a, r_k, ln_w, ln_b, batch):
    t = feat.shape[0]
    nc = t // batch // CHUNK
    d = D_MODEL
    row = lambda w: pl.BlockSpec((CHUNK, w), lambda b, c: (b * nc + c, 0))
    vec = lambda v: v.astype(F32).reshape(1, -1)
    w2p = jnp.concatenate([w2, jnp.zeros_like(w2)], axis=0).astype(BF16)
    a2p = jnp.concatenate([jnp.zeros_like(a2), a2], axis=0).astype(BF16)
    e_mat = _head_expand(RWKV_HEADS, RWKV_HEAD)
    return pl.pallas_call(
        _rwkv_kernel,
        grid=(batch, nc),
        in_specs=[row(RWKV_COLS), _full((1, RWKV_COLS)), _full((1, d)), _full((LANES, d)), _full((1, d)),
                  _full((LANES, d)), _full((LANES, d)), _full((1, d)), _full((1, d)), _full((1, d)),
                  _full((1, d)), _full((1, d)), _full((LANES, d)), _full((d, LANES))],
        out_specs=row(d),
        out_shape=jax.ShapeDtypeStruct((t, d), F32),
        scratch_shapes=[pltpu.VMEM((CHUNK + 8, RWKV_COLS), F32),
                        pltpu.VMEM((RWKV_HEADS // 2, LANES, LANES), F32),
                        pltpu.VMEM((CHUNK, d), F32)],
        compiler_params=_cparams(("parallel", "arbitrary")),
        name="rwkv7",
    )(feat, vec(mu), vec(w0), w2p, vec(a0), a2p, g2.astype(BF16), vec(k_k), vec(k_a), vec(r_k), vec(ln_w),
      vec(ln_b), e_mat, e_mat.T)


def _merge_kernel(ya_ref, yb_ref, yc_ref, g_ref, h_ref, wa_ref, wb_ref, wc_ref, wo_ref, lg_ref, lb_ref,
                  o_ref, ob_ref):
    d = D_MODEL
    m = (_sigmoid(g_ref[:, 0:d]) * _bdot(ya_ref[...], wa_ref[...])
         + _sigmoid(g_ref[:, d:2 * d]) * _bdot(yb_ref[...], wb_ref[...])
         + _sigmoid(g_ref[:, 2 * d:3 * d]) * _bdot(yc_ref[...], wc_ref[...]))
    hn = DN_ALPHA * h_ref[...] + _bdot(m, wo_ref[...])
    y = _layernorm_rows(hn, lg_ref[...], lb_ref[...])
    o_ref[...] = y
    ob_ref[...] = y.astype(BF16)


def _merge(ya, yb, yc, gates, h, wa, wb, wc, wo, ln_g, ln_b, tm=256):
    t, d = h.shape
    tm = min(tm, t)
    row = lambda w: pl.BlockSpec((tm, w), lambda i: (i, 0))
    return pl.pallas_call(
        _merge_kernel,
        grid=(t // tm,),
        in_specs=[row(d), row(d), row(d), row(3 * d), row(d)] + [_full((d, d))] * 4 + [_full((1, d))] * 2,
        out_specs=[row(d), row(d)],
        out_shape=[jax.ShapeDtypeStruct((t, d), F32), jax.ShapeDtypeStruct((t, d), BF16)],
        compiler_params=_cparams(("parallel",)),
        name="merge_ln",
    )(ya, yb, yc, gates, h, wa.astype(BF16), wb.astype(BF16), wc.astype(BF16), wo.astype(BF16),
      ln_g.reshape(1, d), ln_b.reshape(1, d))


def _router_kernel(h_ref, wt_ref, bias_ref, o_ref):
    logits = lax.dot_general(wt_ref[...], h_ref[...], _NT, precision=lax.Precision.HIGHEST,
                             preferred_element_type=F32)
    mx = jnp.max(logits, axis=0, keepdims=True)
    ex = jnp.exp(logits - mx)
    probs = ex / jnp.sum(ex, axis=0, keepdims=True)
    sel = probs + bias_ref[...]
    rows = [sel[e:e + 1, :] for e in range(N_EXPERTS)]
    prow = [probs[e:e + 1, :] for e in range(N_EXPERTS)]
    gscore = []
    for g in range(N_GROUPS):
        m = rows[g * EXPERTS_PER_GROUP:(g + 1) * EXPERTS_PER_GROUP]
        best = None
        for i in range(EXPERTS_PER_GROUP):
            for j in range(i + 1, EXPERTS_PER_GROUP):
                s = m[i] + m[j]
                best = s if best is None else jnp.maximum(best, s)
        gscore.append(best)
    chosen = []
    for g in range(N_GROUPS):
        ok = None
        for g2 in range(N_GROUPS):
            if g2 == g:
                continue
            t = (gscore[g] > gscore[g2]) if g2 < g else (gscore[g] >= gscore[g2])
            ok = t if ok is None else (ok & t)
        chosen.append(ok)
    picked = []
    for e in range(N_EXPERTS):
        g = e // EXPERTS_PER_GROUP
        rank = None
        for e2 in range(g * EXPERTS_PER_GROUP, (g + 1) * EXPERTS_PER_GROUP):
            if e2 == e:
                continue
            ahead = (rows[e2] >= rows[e]) if e2 < e else (rows[e2] > rows[e])
            ahead = ahead.astype(F32)
            rank = ahead if rank is None else rank + ahead
        picked.append(jnp.where(chosen[g] & (rank < 2.0), prow[e], 0.0))
    tot = picked[0]
    for e in range(1, N_EXPERTS):
        tot = tot + picked[e]
    inv = 1.0 / tot
    for e in range(N_EXPERTS):
        o_ref[e:e + 1, :] = picked[e] * inv


def _router(h, router_w, router_bias, tm=512):
    t, d = h.shape
    tm = min(tm, t)
    return pl.pallas_call(
        _router_kernel,
        grid=(t // tm,),
        in_specs=[pl.BlockSpec((tm, d), lambda i: (i, 0)), _full((N_EXPERTS, d)), _full((N_EXPERTS, 1))],
        out_specs=pl.BlockSpec((N_EXPERTS, tm), lambda i: (0, i)),
        out_shape=jax.ShapeDtypeStruct((N_EXPERTS, t), F32),
        compiler_params=_cparams(("parallel",)),
        name="router",
    )(h, router_w.T, router_bias.reshape(N_EXPERTS, 1).astype(F32))


def _moe_kernel(xb_ref, h_ref, gt_ref, wg_ref, wu_ref, wd_ref, lg_ref, lb_ref, o_ref, ob_ref, acc_ref):
    e = pl.program_id(1)

    @pl.when(e == 0)
    def _():
        acc_ref[...] = jnp.zeros(acc_ref.shape, F32)

    x = xb_ref[...]
    act = _silu(_dot(x, wg_ref[0])) * _dot(x, wu_ref[0])
    acc_ref[...] += gt_ref[0][:, 0:1] * _bdot(act, wd_ref[0])

    @pl.when(e == N_EXPERTS - 1)
    def _():
        y = _layernorm_rows(DN_ALPHA * h_ref[...] + acc_ref[...], lg_ref[...], lb_ref[...])
        o_ref[...] = y
        ob_ref[...] = y.astype(BF16)


def _moe(h_bf, h, gates_rep, wg, wu, wd, ln_g, ln_b, tm=512):
    t, d = h.shape
    tm = min(tm, t)
    row = lambda w: pl.BlockSpec((tm, w), lambda i, e: (i, 0))
    return pl.pallas_call(
        _moe_kernel,
        grid=(t // tm, N_EXPERTS),
        in_specs=[row(d), row(d), pl.BlockSpec((1, tm, LANES), lambda i, e: (e, i, 0)),
                  pl.BlockSpec((1, d, D_EXPERT), lambda i, e: (e, 0, 0)),
                  pl.BlockSpec((1, d, D_EXPERT), lambda i, e: (e, 0, 0)),
                  pl.BlockSpec((1, D_EXPERT, d), lambda i, e: (e, 0, 0)),
                  _full((1, d)), _full((1, d))],
        out_specs=[row(d), row(d)],
        out_shape=[jax.ShapeDtypeStruct((t, d), F32), jax.ShapeDtypeStruct((t, d), BF16)],
        scratch_shapes=[pltpu.VMEM((tm, d), F32)],
        compiler_params=_cparams(("parallel", "arbitrary")),
        name="experts_ln",
    )(h_bf, h, gates_rep, wg, wu, wd, ln_g.reshape(1, d), ln_b.reshape(1, d))


def _in_weights(w_in_l):
    offs = [0]
    for s in IN_SIZES:
        offs.append(offs[-1] + s)
    seg = [w_in_l[:, offs[i]:offs[i + 1]] for i in range(len(IN_SIZES))]
    seg[2] = jnp.pad(seg[2], ((0, 0), (0, LANES - SSD_HEADS)))
    return [s.astype(BF16) for s in seg]


def kernel(x, ln_in_g, ln_in_b, w_in, ssd_conv_w, ssd_conv_b, ssd_dt_bias, ssd_a_log, ssd_d, ssd_norm_w, rwkv_mu, rwkv_w0, rwkv_w2, rwkv_a0, rwkv_a2, rwkv_g2, rwkv_k_k, rwkv_k_a, rwkv_r_k, rwkv_ln_w, rwkv_ln_b, hgrn_lb, hgrn_norm_w, w_br_ssd, w_br_rwkv, w_br_hgrn, w_out, ln1_g, ln1_b, router_w, router_bias, exp_w_gate, exp_w_up, exp_w_down, ln2_g, ln2_b):
    batch, seq, d = x.shape
    t = batch * seq
    h, h_bf = _layernorm(x.reshape(t, d), ln_in_g, ln_in_b)
    lsm = jax.nn.softmax(hgrn_lb.astype(F32), axis=0)
    lower_bounds = jnp.cumsum(lsm, axis=0) - lsm[0]
    for l in range(DEPTH):
        names = ("z", "xbc", "dt", "rwkv", "hgrn", "gates")
        z, xbc, dt_raw, f_rwkv, f_hgrn, gates = [
            _project(h_bf, w, f"proj_{n}") for w, n in zip(_in_weights(w_in[l]), names)]
        y_a = _ssd(z, xbc, dt_raw, ssd_conv_w[l], ssd_conv_b[l], ssd_dt_bias[l], ssd_a_log[l], ssd_d[l],
                   ssd_norm_w[l], batch)
        y_b = _rwkv(f_rwkv, rwkv_mu[l], rwkv_w0[l], rwkv_w2[l], rwkv_a0[l], rwkv_a2[l], rwkv_g2[l],
                    rwkv_k_k[l], rwkv_k_a[l], rwkv_r_k[l].reshape(-1), rwkv_ln_w[l], rwkv_ln_b[l], batch)
        y_c = _hgrn(f_hgrn, lower_bounds[l], hgrn_norm_w[l], batch)
        h, h_bf = _merge(y_a, y_b, y_c, gates, h, w_br_ssd[l], w_br_rwkv[l], w_br_hgrn[l], w_out[l],
                         ln1_g[l], ln1_b[l])
        gates_t = _router(h, router_w, router_bias)
        gates_rep = jnp.broadcast_to(gates_t[:, :, None], (N_EXPERTS, t, LANES))
        h, h_bf = _moe(h_bf, h, gates_rep, exp_w_gate[l].astype(BF16), exp_w_up[l].astype(BF16),
                       exp_w_down[l].astype(BF16), ln2_g[l], ln2_b[l])
    return h.reshape(batch, seq, d)
```

```python
import functools
import math

import jax
import jax.numpy as jnp
from jax import lax
from jax.experimental import pallas as pl
from jax.experimental.pallas import tpu as pltpu

F32 = jnp.float32
BF16 = jnp.bfloat16

D_MODEL = 1024
DEPTH = 2
CHUNK = 64
LANES = 128

SSD_HEADS = 16
SSD_HEADDIM = 64
SSD_GROUPS = 2
SSD_STATE = 128
SSD_CONV = 4
SSD_XBC = D_MODEL + 2 * SSD_GROUPS * SSD_STATE

RWKV_HEADS = 16
RWKV_HEAD = 64
RWKV_COLS = 3 * D_MODEL + 64 + 64 + 128
RWKV_GN_EPS = 64e-5

HGRN_HEADS = 8
HGRN_EXPAND = 128

IN_SIZES = (D_MODEL, SSD_XBC, SSD_HEADS, RWKV_COLS, 4 * D_MODEL, 3 * D_MODEL)

N_EXPERTS = 16
N_GROUPS = 4
EXPERTS_PER_GROUP = 4
D_EXPERT = 512

DN_ALPHA = (2 * DEPTH) ** 0.25
LN_EPS = 1e-5
RMS_EPS = 1e-6

VMEM_LIMIT = 56 * 1024 * 1024

_NN = (((1,), (0,)), ((), ()))
_NT = (((1,), (1,)), ((), ()))
_TN = (((0,), (0,)), ((), ()))


def _dot(a, b, dims=_NN):
    return lax.dot_general(a, b, dims, preferred_element_type=F32)


def _bdot(a, b, dims=_NN):
    return _dot(a.astype(BF16), b.astype(BF16), dims)


def _split3(x):
    x1 = x.astype(BF16)
    r1 = x - x1.astype(F32)
    x2 = r1.astype(BF16)
    x3 = (r1 - x2.astype(F32)).astype(BF16)
    return x1, x2, x3


def _sel_l(sel, x):
    x1, x2, x3 = _split3(x)
    return _dot(sel, x1) + _dot(sel, x2) + _dot(sel, x3)


def _sel_r(x, sel):
    x1, x2, x3 = _split3(x)
    return _dot(x1, sel) + _dot(x2, sel) + _dot(x3, sel)


def _sigmoid(x):
    return 1.0 / (1.0 + jnp.exp(-x))


def _silu(x):
    return x * _sigmoid(x)


def _softplus(x):
    return jnp.maximum(x, 0.0) + jnp.log1p(jnp.exp(-jnp.abs(x)))


def _layernorm_rows(x, g, b):
    mu = jnp.mean(x, axis=-1, keepdims=True)
    xc = x - mu
    var = jnp.mean(xc * xc, axis=-1, keepdims=True)
    return xc * lax.rsqrt(var + LN_EPS) * g + b


def _tri(n, strict=False):
    r = lax.broadcasted_iota(jnp.int32, (n, n), 0)
    c = lax.broadcasted_iota(jnp.int32, (n, n), 1)
    return (c < r) if strict else (c <= r)


def _cparams(sem):
    return pltpu.CompilerParams(dimension_semantics=sem, vmem_limit_bytes=VMEM_LIMIT)


def _full(shape):
    return pl.BlockSpec(shape, lambda *_: (0,) * len(shape))


def _ln_kernel(x_ref, g_ref, b_ref, o_ref, ob_ref):
    y = _layernorm_rows(x_ref[...], g_ref[...], b_ref[...])
    o_ref[...] = y
    ob_ref[...] = y.astype(BF16)


def _layernorm(x, g, b, tm=512):
    t, d = x.shape
    row = pl.BlockSpec((tm, d), lambda i: (i, 0))
    return pl.pallas_call(
        _ln_kernel,
        grid=(t // tm,),
        in_specs=[row, _full((1, d)), _full((1, d))],
        out_specs=[row, row],
        out_shape=[jax.ShapeDtypeStruct((t, d), F32), jax.ShapeDtypeStruct((t, d), BF16)],
        compiler_params=_cparams(("parallel",)),
        name="layernorm_in",
    )(x, g.reshape(1, d), b.reshape(1, d))


def _proj_kernel(x_ref, w_ref, o_ref):
    o_ref[...] = _dot(x_ref[...], w_ref[...]).astype(o_ref.dtype)


def _project(x_bf, w_bf, name, tm=1024):
    t, k = x_bf.shape
    n = w_bf.shape[1]
    tm = min(tm, t)
    tn = n
    for cand in range(min(n, 2048), 0, -LANES):
        if n % cand == 0:
            tn = cand
            break
    return pl.pallas_call(
        _proj_kernel,
        grid=(n // tn, t // tm),
        in_specs=[pl.BlockSpec((tm, k), lambda j, i: (i, 0)), pl.BlockSpec((k, tn), lambda j, i: (0, j))],
        out_specs=pl.BlockSpec((tm, tn), lambda j, i: (i, j)),
        out_shape=jax.ShapeDtypeStruct((t, n), F32),
        compiler_params=_cparams(("parallel", "parallel")),
        name=name,
    )(x_bf, w_bf)


def _head_expand(n_heads, width):
    h = lax.broadcasted_iota(jnp.int32, (LANES, n_heads * width), 0)
    c = lax.broadcasted_iota(jnp.int32, (LANES, n_heads * width), 1)
    return (c // width == h).astype(BF16)


def _ssd_kernel(z_ref, xbc_ref, dt_ref, cw_ref, cb_ref, dtb_ref, a_ref, dsk_ref, nw_ref, e_ref,
                o_ref, pad_ref, st_ref):
    c = pl.program_id(1)
    n_pairs = SSD_HEADS // 2

    @pl.when(c == 0)
    def _():
        pad_ref[0:8, :] = jnp.zeros((8, SSD_XBC), F32)
        st_ref[...] = jnp.zeros(st_ref.shape, F32)

    pad_ref[8:8 + CHUNK, :] = xbc_ref[...]
    acc = jnp.broadcast_to(cb_ref[...], (CHUNK, SSD_XBC))
    for j in range(SSD_CONV):
        acc = acc + cw_ref[j:j + 1, :] * pad_ref[pl.ds(8 - (SSD_CONV - 1) + j, CHUNK), :]
    pad_ref[0:8, :] = pad_ref[CHUNK:CHUNK + 8, :]
    xbc = _silu(acc)
    xs = xbc[:, :D_MODEL]

    e_mat = e_ref[...]
    dt = _softplus(dt_ref[...] + dtb_ref[...])
    da = dt * a_ref[...]
    tri = _tri(CHUNK).astype(BF16)
    acs = _sel_l(tri, da)
    acs_e = _sel_r(acs, e_mat)
    dt_e = _sel_r(dt, e_mat)

    li = lax.broadcasted_iota(jnp.int32, (CHUNK, D_MODEL), 0)
    si = lax.broadcasted_iota(jnp.int32, (CHUNK, D_MODEL), 1) & (SSD_HEADDIM - 1)
    acs_row = jnp.sum(jnp.where(li == si, acs_e, 0.0), axis=0, keepdims=True)
    decay = jnp.exp(jnp.where(si <= li, acs_e - acs_row, -jnp.inf))
    xs_dt = xs * dt_e
    exp_acs = jnp.exp(acs_e)
    acs_last = acs_e[CHUNK - 1:CHUNK, :]
    to_end = jnp.exp(acs_last - acs_e)
    exp_last = jnp.exp(acs_last)

    lane = lax.broadcasted_iota(jnp.int32, (CHUNK, LANES), 1)
    m0 = lane < SSD_HEADDIM

    pairs = range(n_pairs)
    sls = [slice(j * LANES, (j + 1) * LANES) for j in pairs]
    grp = [j // (n_pairs // SSD_GROUPS) for j in pairs]
    bm = [xbc[:, D_MODEL + g * SSD_STATE:D_MODEL + (g + 1) * SSD_STATE].astype(BF16) for g in range(SSD_GROUPS)]
    cm = [xbc[:, D_MODEL + (SSD_GROUPS + g) * SSD_STATE:D_MODEL + (SSD_GROUPS + g + 1) * SSD_STATE].astype(BF16)
          for g in range(SSD_GROUPS)]
    cb2 = [_dot(cm[g], jnp.concatenate([bm[g], bm[g]], axis=0), _NT) for g in range(SSD_GROUPS)]
    st = [st_ref[j] for j in pairs]
    xp = [xs_dt[:, sl] for sl in sls]
    xbd = [jnp.concatenate([jnp.where(m0, x, 0.0), jnp.where(m0, 0.0, x)], axis=0) for x in xp]
    y_parts = [_bdot(cb2[grp[j]] * decay[:, sls[j]], xbd[j]) + _bdot(cm[grp[j]], st[j]) * exp_acs[:, sls[j]]
               for j in pairs]
    upd = [_bdot(bm[grp[j]], xp[j] * to_end[:, sls[j]], _TN) for j in pairs]
    for j in pairs:
        st_ref[j] = st[j] * exp_last[:, sls[j]] + upd[j]
    y = jnp.concatenate(y_parts, axis=1) + xs * dsk_ref[...]
    y = y * _silu(z_ref[...])
    gw = D_MODEL // SSD_GROUPS
    for g in range(SSD_GROUPS):
        yg = y[:, g * gw:(g + 1) * gw]
        ms = jnp.mean(yg * yg, axis=-1, keepdims=True)
        o_ref[:, g * gw:(g + 1) * gw] = yg * lax.rsqrt(ms + RMS_EPS) * nw_ref[:, g * gw:(g + 1) * gw]


def _ssd(z, xbc, dt_raw, conv_w, conv_b, dt_bias, a_log, d_skip, norm_w, batch):
    t = z.shape[0]
    nc = t // batch // CHUNK
    pad16 = lambda v: jnp.pad(v.astype(F32), (0, LANES - SSD_HEADS)).reshape(1, LANES)
    row = lambda w: pl.BlockSpec((CHUNK, w), lambda b, c: (b * nc + c, 0))
    return pl.pallas_call(
        _ssd_kernel,
        grid=(batch, nc),
        in_specs=[row(D_MODEL), row(SSD_XBC), row(LANES), _full((SSD_CONV, SSD_XBC)), _full((1, SSD_XBC)),
                  _full((1, LANES)), _full((1, LANES)), _full((1, D_MODEL)), _full((1, D_MODEL)),
                  _full((LANES, D_MODEL))],
        out_specs=row(D_MODEL),
        out_shape=jax.ShapeDtypeStruct((t, D_MODEL), F32),
        scratch_shapes=[pltpu.VMEM((CHUNK + 8, SSD_XBC), F32),
                        pltpu.VMEM((SSD_HEADS // 2, SSD_STATE, LANES), F32)],
        compiler_params=_cparams(("parallel", "arbitrary")),
        name="ssd",
    )(z, xbc, dt_raw, conv_w, conv_b.reshape(1, -1), pad16(dt_bias), pad16(-jnp.exp(a_log.astype(F32))),
      jnp.repeat(d_skip.astype(F32), SSD_HEADDIM).reshape(1, -1), norm_w.reshape(1, -1),
      _head_expand(SSD_HEADS, SSD_HEADDIM))


def _hgrn_kernel(f_ref, lb_ref, nw_ref, o_ref, st_ref):
    c = pl.program_id(1)

    @pl.when(c == 0)
    def _():
        st_ref[...] = jnp.zeros(st_ref.shape, F32)

    d = D_MODEL
    lb = lb_ref[...]
    q = _silu(f_ref[:, 0:d])
    forget = lb + (1.0 - lb) * _sigmoid(f_ref[:, d:2 * d])
    k = 1.0 - forget
    v = f_ref[:, 2 * d:3 * d]
    bc = _sel_l(_tri(CHUNK).astype(BF16), jnp.log(forget))
    mid = bc[CHUNK // 2:CHUNK // 2 + 1, :]
    last = bc[CHUNK - 1:CHUNK, :]
    qe = q * jnp.exp(bc - mid)
    ke = k * jnp.exp(mid - bc)
    qd = q * jnp.exp(bc)
    kd = k * jnp.exp(last - bc)
    w_last = jnp.exp(last)
    causal = _tri(CHUNK)
    heads = range(HGRN_HEADS)
    sls = [slice(h * HGRN_EXPAND, (h + 1) * HGRN_EXPAND) for h in heads]
    vb = [v[:, sl].astype(BF16) for sl in sls]
    st = [st_ref[h] for h in heads]
    att = [jnp.where(causal, _bdot(qe[:, sl], ke[:, sl], _NT), 0.0) for sl in sls]
    o = [_bdot(att[h], vb[h]) + _bdot(qd[:, sls[h]], st[h], _NT) for h in heads]
    upd = [_bdot(vb[h], kd[:, sls[h]], _TN) for h in heads]
    for h in heads:
        st_ref[h] = st[h] * w_last[:, sls[h]] + upd[h]
        oh = o[h] * lax.rsqrt(jnp.mean(o[h] * o[h], axis=-1, keepdims=True) + RMS_EPS) * nw_ref[...]
        o_ref[:, sls[h]] = oh * _sigmoid(f_ref[:, 3 * d + h * HGRN_EXPAND:3 * d + (h + 1) * HGRN_EXPAND])


def _hgrn(feat, lb, norm_w, batch):
    t = feat.shape[0]
    nc = t // batch // CHUNK
    row = lambda w: pl.BlockSpec((CHUNK, w), lambda b, c: (b * nc + c, 0))
    return pl.pallas_call(
        _hgrn_kernel,
        grid=(batch, nc),
        in_specs=[row(4 * D_MODEL), _full((1, D_MODEL)), _full((1, HGRN_EXPAND))],
        out_specs=row(D_MODEL),
        out_shape=jax.ShapeDtypeStruct((t, D_MODEL), F32),
        scratch_shapes=[pltpu.VMEM((HGRN_HEADS, HGRN_EXPAND, HGRN_EXPAND), F32)],
        compiler_params=_cparams(("parallel", "arbitrary")),
        name="hgrn2",
    )(feat, lb.reshape(1, -1), norm_w.reshape(1, -1))


def _rwkv_kernel(f_ref, mu_ref, w0_ref, w2_ref, a0_ref, a2_ref, g2_ref, kk_ref, ka_ref, rk_ref, lnw_ref,
                 lnb_ref, e_ref, et_ref, o_ref, pad_ref, st_ref, y_ref):
    c = pl.program_id(1)
    d = D_MODEL
    n_pairs = RWKV_HEADS // 2

    @pl.when(c == 0)
    def _():
        pad_ref[0:8, :] = jnp.zeros((8, RWKV_COLS), F32)
        st_ref[...] = jnp.zeros(st_ref.shape, F32)

    pad_ref[8:8 + CHUNK, :] = f_ref[...]
    prev = pad_ref[pl.ds(7, CHUNK), :]
    pad_ref[0:8, :] = pad_ref[CHUNK:CHUNK + 8, :]
    cur = f_ref[...]
    x = cur + (prev - cur) * mu_ref[...]
    r, k, v = x[:, 0:d], x[:, d:2 * d], x[:, 2 * d:3 * d]
    lora_in = x[:, 3 * d:3 * d + LANES]
    xg = x[:, 3 * d + LANES:3 * d + 2 * LANES]

    e_mat, et_mat = e_ref[...], et_ref[...]
    w = -_softplus(-(w0_ref[...] + _bdot(jnp.tanh(lora_in), w2_ref[...]))) - 0.5
    ld = -jnp.exp(w)
    a = _sigmoid(a0_ref[...] + _bdot(lora_in, a2_ref[...]))
    gate = _bdot(_sigmoid(xg), g2_ref[...])
    kk = k * kk_ref[...]
    nrm = jnp.maximum(jnp.sqrt(_sel_r(kk * kk, et_mat)), 1e-12)
    kk = kk * _sel_r(1.0 / nrm, e_mat)
    k2 = k * (1.0 + (a - 1.0) * ka_ref[...])
    cw = _sel_l(_tri(CHUNK).astype(BF16), ld)
    e_neg = jnp.exp(-cw)
    a_t = -kk * jnp.exp(cw - ld)
    r_t = r * jnp.exp(cw)
    b_t = kk * a * e_neg
    k_t = k2 * e_neg
    w_last = jnp.exp(cw[CHUNK - 1:CHUNK, :])

    lane = lax.broadcasted_iota(jnp.int32, (CHUNK, LANES), 1)
    m0 = lane < RWKV_HEAD
    row2 = lax.broadcasted_iota(jnp.int32, (LANES, LANES), 0)
    col2 = lax.broadcasted_iota(jnp.int32, (LANES, LANES), 1)
    top = row2 < CHUNK
    same = (row2 < CHUNK) == (col2 < RWKV_HEAD)
    bd_strict = same & ((col2 & (CHUNK - 1)) < (row2 & (CHUNK - 1)))
    bd_incl = same & ((col2 & (CHUNK - 1)) <= (row2 & (CHUNK - 1)))

    def stack(p):
        return jnp.concatenate([jnp.where(m0, p, 0.0), jnp.where(m0, 0.0, p)], axis=0)

    pairs = range(n_pairs)
    sls = [slice(j * LANES, (j + 1) * LANES) for j in pairs]
    bf = lambda xs: [x.astype(BF16) for x in xs]
    s1 = bf(stack(a_t[:, sl]) for sl in sls)
    s2 = bf(stack(r_t[:, sl]) for sl in sls)
    vst = bf(stack(v[:, sl]) for sl in sls)
    qmat = bf(jnp.concatenate([b_t[:, sl], k_t[:, sl]], axis=0) for sl in sls)
    mt = [st_ref[j] for j in pairs]
    mtb = bf(mt)
    gm = [_dot(s1[j], qmat[j], _NT) for j in pairs]
    hm = [_dot(s2[j], qmat[j], _NT) for j in pairs]
    gr = [pltpu.roll(g, CHUNK, 1) for g in gm]
    hr = [pltpu.roll(h, CHUNK, 1) for h in hm]
    n_ab = [jnp.where(bd_strict, jnp.where(top, gm[j], gr[j]), 0.0) for j in pairs]
    a_ak = bf(jnp.where(bd_strict, jnp.where(top, gr[j], gm[j]), 0.0) for j in pairs)
    a_rb = bf(jnp.where(bd_incl, jnp.where(top, hm[j], hr[j]), 0.0) for j in pairs)
    a_rk = bf(jnp.where(bd_incl, jnp.where(top, hr[j], hm[j]), 0.0) for j in pairs)
    xm = [_dot(s1[j], mtb[j], _NT) + _dot(a_ak[j], vst[j]) for j in pairs]
    tp = n_ab
    pw = bf(n_ab)
    for _ in range(int(math.log2(CHUNK)) - 1):
        pwf = [_dot(p, p) for p in pw]
        pw = bf(pwf)
        tp = [tp[j] + pwf[j] + _bdot(tp[j], pw[j]) for j in pairs]
    u = [xm[j] + _bdot(tp[j], xm[j]) for j in pairs]
    yst = [_dot(s2[j], mtb[j], _NT) + _bdot(a_rb[j], u[j]) + _dot(a_rk[j], vst[j]) for j in pairs]
    for j in pairs:
        y_ref[:, sls[j]] = yst[j][:CHUNK] + yst[j][CHUNK:]
    uv = [jnp.concatenate([u[j][:CHUNK] + u[j][CHUNK:], v[:, sls[j]]], axis=0) for j in pairs]
    upd = [jnp.where(same, _bdot(uv[j], qmat[j], _TN), 0.0) for j in pairs]
    for j in pairs:
        st_ref[j] = (mt[j] + upd[j]) * w_last[:, sls[j]]

    y = y_ref[...]
    inv_n = 1.0 / RWKV_HEAD
    yc = y - _sel_r(_sel_r(y, et_mat) * inv_n, e_mat)
    rs = lax.rsqrt(_sel_r(yc * yc, et_mat) * inv_n + RWKV_GN_EPS)
    yn = yc * _sel_r(rs, e_mat) * lnw_ref[...] + lnb_ref[...]
    bonus = _sel_r(_sel_r(r * k2 * rk_ref[...], et_mat), e_mat) * v
    o_ref[...] = (yn + bonus) * gate


def _rwkv(feat, mu, w0, w2, a0, a2, g2, k_k, k_a, r_k, ln_w, ln_b, batch):
    t = feat.shape[0]
    nc = t // batch // CHUNK
    d = D_MODEL
    row = lambda w: pl.BlockSpec((CHUNK, w), lambda b, c: (b * nc + c, 0))
    vec = lambda v: v.astype(F32).reshape(1, -1)
    w2p = jnp.concatenate([w2, jnp.zeros_like(w2)], axis=0).astype(BF16)
    a2p = jnp.concatenate([jnp.zeros_like(a2), a2], axis=0).astype(BF16)
    e_mat = _head_expand(RWKV_HEADS, RWKV_HEAD)
    return pl.pallas_call(
        _rwkv_kernel,
        grid=(batch, nc),
        in_specs=[row(RWKV_COLS), _full((1, RWKV_COLS)), _full((1, d)), _full((LANES, d)), _full((1, d)),
                  _full((LANES, d)), _full((LANES, d)), _full((1, d)), _full((1, d)), _full((1, d)),
                  _full((1, d)), _full((1, d)), _full((LANES, d)), _full((d, LANES))],
        out_specs=row(d),
        out_shape=jax.ShapeDtypeStruct((t, d), F32),
        scratch_shapes=[pltpu.VMEM((CHUNK + 8, RWKV_COLS), F32),
                        pltpu.VMEM((RWKV_HEADS // 2, LANES, LANES), F32),
                        pltpu.VMEM((CHUNK, d), F32)],
        compiler_params=_cparams(("parallel", "arbitrary")),
        name="rwkv7",
    )(feat, vec(mu), vec(w0), w2p, vec(a0), a2p, g2.astype(BF16), vec(k_k), vec(k_a), vec(r_k), vec(ln_w),
      vec(ln_b), e_mat, e_mat.T)


def _merge_kernel(ya_ref, yb_ref, yc_ref, g_ref, h_ref, wa_ref, wb_ref, wc_ref, wo_ref, lg_ref, lb_ref,
                  o_ref, ob_ref):
    d = D_MODEL
    m = (_sigmoid(g_ref[:, 0:d]) * _bdot(ya_ref[...], wa_ref[...])
         + _sigmoid(g_ref[:, d:2 * d]) * _bdot(yb_ref[...], wb_ref[...])
         + _sigmoid(g_ref[:, 2 * d:3 * d]) * _bdot(yc_ref[...], wc_ref[...]))
    hn = DN_ALPHA * h_ref[...] + _bdot(m, wo_ref[...])
    y = _layernorm_rows(hn, lg_ref[...], lb_ref[...])
    o_ref[...] = y
    ob_ref[...] = y.astype(BF16)


def _merge(ya, yb, yc, gates, h, wa, wb, wc, wo, ln_g, ln_b, tm=256):
    t, d = h.shape
    tm = min(tm, t)
    row = lambda w: pl.BlockSpec((tm, w), lambda i: (i, 0))
    return pl.pallas_call(
        _merge_kernel,
        grid=(t // tm,),
        in_specs=[row(d), row(d), row(d), row(3 * d), row(d)] + [_full((d, d))] * 4 + [_full((1, d))] * 2,
        out_specs=[row(d), row(d)],
        out_shape=[jax.ShapeDtypeStruct((t, d), F32), jax.ShapeDtypeStruct((t, d), BF16)],
        compiler_params=_cparams(("parallel",)),
        name="merge_ln",
    )(ya, yb, yc, gates, h, wa.astype(BF16), wb.astype(BF16), wc.astype(BF16), wo.astype(BF16),
      ln_g.reshape(1, d), ln_b.reshape(1, d))


def _router_kernel(h_ref, wt_ref, bias_ref, o_ref):
    logits = lax.dot_general(wt_ref[...], h_ref[...], _NT, precision=lax.Precision.HIGHEST,
                             preferred_element_type=F32)
    mx = jnp.max(logits, axis=0, keepdims=True)
    ex = jnp.exp(logits - mx)
    probs = ex / jnp.sum(ex, axis=0, keepdims=True)
    sel = probs + bias_ref[...]
    rows = [sel[e:e + 1, :] for e in range(N_EXPERTS)]
    prow = [probs[e:e + 1, :] for e in range(N_EXPERTS)]
    gscore = []
    for g in range(N_GROUPS):
        m = rows[g * EXPERTS_PER_GROUP:(g + 1) * EXPERTS_PER_GROUP]
        best = None
        for i in range(EXPERTS_PER_GROUP):
            for j in range(i + 1, EXPERTS_PER_GROUP):
                s = m[i] + m[j]
                best = s if best is None else jnp.maximum(best, s)
        gscore.append(best)
    chosen = []
    for g in range(N_GROUPS):
        ok = None
        for g2 in range(N_GROUPS):
            if g2 == g:
                continue
            t = (gscore[g] > gscore[g2]) if g2 < g else (gscore[g] >= gscore[g2])
            ok = t if ok is None else (ok & t)
        chosen.append(ok)
    picked = []
    for e in range(N_EXPERTS):
        g = e // EXPERTS_PER_GROUP
        rank = None
        for e2 in range(g * EXPERTS_PER_GROUP, (g + 1) * EXPERTS_PER_GROUP):
            if e2 == e:
                continue
            ahead = (rows[e2] >= rows[e]) if e2 < e else (rows[e2] > rows[e])
            ahead = ahead.astype(F32)
            rank = ahead if rank is None else rank + ahead
        picked.append(jnp.where(chosen[g] & (rank < 2.0), prow[e], 0.0))
    tot = picked[0]
    for e in range(1, N_EXPERTS):
        tot = tot + picked[e]
    inv = 1.0 / tot
    for e in range(N_EXPERTS):
        o_ref[e:e + 1, :] = picked[e] * inv


def _router(h, router_w, router_bias, tm=512):
    t, d = h.shape
    tm = min(tm, t)
    return pl.pallas_call(
        _router_kernel,
        grid=(t // tm,),
        in_specs=[pl.BlockSpec((tm, d), lambda i: (i, 0)), _full((N_EXPERTS, d)), _full((N_EXPERTS, 1))],
        out_specs=pl.BlockSpec((N_EXPERTS, tm), lambda i: (0, i)),
        out_shape=jax.ShapeDtypeStruct((N_EXPERTS, t), F32),
        compiler_params=_cparams(("parallel",)),
        name="router",
    )(h, router_w.T, router_bias.reshape(N_EXPERTS, 1).astype(F32))


def _moe_kernel(xb_ref, h_ref, gt_ref, wg_ref, wu_ref, wd_ref, lg_ref, lb_ref, o_ref, ob_ref, acc_ref):
    e = pl.program_id(1)

    @pl.when(e == 0)
    def _():
        acc_ref[...] = jnp.zeros(acc_ref.shape, F32)

    x = xb_ref[...]
    act = _silu(_dot(x, wg_ref[0])) * _dot(x, wu_ref[0])
    acc_ref[...] += gt_ref[0][:, 0:1] * _bdot(act, wd_ref[0])

    @pl.when(e == N_EXPERTS - 1)
    def _():
        y = _layernorm_rows(DN_ALPHA * h_ref[...] + acc_ref[...], lg_ref[...], lb_ref[...])
        o_ref[...] = y
        ob_ref[...] = y.astype(BF16)


def _moe(h_bf, h, gates_rep, wg, wu, wd, ln_g, ln_b, tm=512):
    t, d = h.shape
    tm = min(tm, t)
    row = lambda w: pl.BlockSpec((tm, w), lambda i, e: (i, 0))
    return pl.pallas_call(
        _moe_kernel,
        grid=(t // tm, N_EXPERTS),
        in_specs=[row(d), row(d), pl.BlockSpec((1, tm, LANES), lambda i, e: (e, i, 0)),
                  pl.BlockSpec((1, d, D_EXPERT), lambda i, e: (e, 0, 0)),
                  pl.BlockSpec((1, d, D_EXPERT), lambda i, e: (e, 0, 0)),
                  pl.BlockSpec((1, D_EXPERT, d), lambda i, e: (e, 0, 0)),
                  _full((1, d)), _full((1, d))],
        out_specs=[row(d), row(d)],
        out_shape=[jax.ShapeDtypeStruct((t, d), F32), jax.ShapeDtypeStruct((t, d), BF16)],
        scratch_shapes=[pltpu.VMEM((tm, d), F32)],
        compiler_params=_cparams(("parallel", "arbitrary")),
        name="experts_ln",
    )(h_bf, h, gates_rep, wg, wu, wd, ln_g.reshape(1, d), ln_b.reshape(1, d))


def _in_weights(w_in_l):
    offs = [0]
    for s in IN_SIZES:
        offs.append(offs[-1] + s)
    seg = [w_in_l[:, offs[i]:offs[i + 1]] for i in range(len(IN_SIZES))]
    seg[2] = jnp.pad(seg[2], ((0, 0), (0, LANES - SSD_HEADS)))
    return [s.astype(BF16) for s in seg]


def kernel(x, ln_in_g, ln_in_b, w_in, ssd_conv_w, ssd_conv_b, ssd_dt_bias, ssd_a_log, ssd_d, ssd_norm_w, rwkv_mu, rwkv_w0, rwkv_w2, rwkv_a0, rwkv_a2, rwkv_g2, rwkv_k_k, rwkv_k_a, rwkv_r_k, rwkv_ln_w, rwkv_ln_b, hgrn_lb, hgrn_norm_w, w_br_ssd, w_br_rwkv, w_br_hgrn, w_out, ln1_g, ln1_b, router_w, router_bias, exp_w_gate, exp_w_up, exp_w_down, ln2_g, ln2_b):
    batch, seq, d = x.shape
    t = batch * seq
    h, h_bf = _layernorm(x.reshape(t, d), ln_in_g, ln_in_b)
    lsm = jax.nn.softmax(hgrn_lb.astype(F32), axis=0)
    lower_bounds = jnp.cumsum(lsm, axis=0) - lsm[0]
    for l in range(DEPTH):
        names = ("z", "xbc", "dt", "rwkv", "hgrn", "gates")
        z, xbc, dt_raw, f_rwkv, f_hgrn, gates = [
            _project(h_bf, w, f"proj_{n}") for w, n in zip(_in_weights(w_in[l]), names)]
        y_a = _ssd(z, xbc, dt_raw, ssd_conv_w[l], ssd_conv_b[l], ssd_dt_bias[l], ssd_a_log[l], ssd_d[l],
                   ssd_norm_w[l], batch)
        y_b = _rwkv(f_rwkv, rwkv_mu[l], rwkv_w0[l], rwkv_w2[l], rwkv_a0[l], rwkv_a2[l], rwkv_g2[l],
                    rwkv_k_k[l], rwkv_k_a[l], rwkv_r_k[l].reshape(-1), rwkv_ln_w[l], rwkv_ln_b[l], batch)
        y_c = _hgrn(f_hgrn, lower_bounds[l], hgrn_norm_w[l], batch)
        h, h_bf = _merge(y_a, y_b, y_c, gates, h, w_br_ssd[l], w_br_rwkv[l], w_br_hgrn[l], w_out[l],
                         ln1_g[l], ln1_b[l])
        gates_t = _router(h, router_w, router_bias)
        gates_rep = jnp.broadcast_to(gates_t[:, :, None], (N_EXPERTS, t, LANES))
        h, h_bf = _moe(h_bf, h, gates_rep, exp_w_gate[l].astype(BF16), exp_w_up[l].astype(BF16),
                       exp_w_down[l].astype(BF16), ln2_g[l], ln2_b[l])
    return h.reshape(batch, seq, d)
```

```python
import functools
import math

import jax
import jax.numpy as jnp
from jax import lax
from jax.experimental import pallas as pl
from jax.experimental.pallas import tpu as pltpu

F32 = jnp.float32
BF16 = jnp.bfloat16

D_MODEL = 1024
DEPTH = 2
CHUNK = 64
LANES = 128

SSD_HEADS = 16
SSD_HEADDIM = 64
SSD_GROUPS = 2
SSD_STATE = 128
SSD_CONV = 4
SSD_XBC = D_MODEL + 2 * SSD_GROUPS * SSD_STATE

RWKV_HEADS = 16
RWKV_HEAD = 64
RWKV_COLS = 3 * D_MODEL + 64 + 64 + 128
RWKV_GN_EPS = 64e-5

HGRN_HEADS = 8
HGRN_EXPAND = 128

IN_SIZES = (D_MODEL, SSD_XBC, SSD_HEADS, RWKV_COLS, 4 * D_MODEL, 3 * D_MODEL)

N_EXPERTS = 16
N_GROUPS = 4
EXPERTS_PER_GROUP = 4
D_EXPERT = 512

DN_ALPHA = (2 * DEPTH) ** 0.25
LN_EPS = 1e-5
RMS_EPS = 1e-6

VMEM_LIMIT = 56 * 1024 * 1024

_NN = (((1,), (0,)), ((), ()))
_NT = (((1,), (1,)), ((), ()))
_TN = (((0,), (0,)), ((), ()))


def _dot(a, b, dims=_NN):
    return lax.dot_general(a, b, dims, preferred_element_type=F32)


def _bdot(a, b, dims=_NN):
    return _dot(a.astype(BF16), b.astype(BF16), dims)


def _split3(x):
    x1 = x.astype(BF16)
    r1 = x - x1.astype(F32)
    x2 = r1.astype(BF16)
    x3 = (r1 - x2.astype(F32)).astype(BF16)
    return x1, x2, x3


def _sel_l(sel, x):
    x1, x2, x3 = _split3(x)
    return _dot(sel, x1) + _dot(sel, x2) + _dot(sel, x3)


def _sel_r(x, sel):
    x1, x2, x3 = _split3(x)
    return _dot(x1, sel) + _dot(x2, sel) + _dot(x3, sel)


def _sigmoid(x):
    return 1.0 / (1.0 + jnp.exp(-x))


def _silu(x):
    return x * _sigmoid(x)


def _softplus(x):
    return jnp.maximum(x, 0.0) + jnp.log1p(jnp.exp(-jnp.abs(x)))


def _layernorm_rows(x, g, b):
    mu = jnp.mean(x, axis=-1, keepdims=True)
    xc = x - mu
    var = jnp.mean(xc * xc, axis=-1, keepdims=True)
    return xc * lax.rsqrt(var + LN_EPS) * g + b


def _tri(n, strict=False):
    r = lax.broadcasted_iota(jnp.int32, (n, n), 0)
    c = lax.broadcasted_iota(jnp.int32, (n, n), 1)
    return (c < r) if strict else (c <= r)


def _cparams(sem):
    return pltpu.CompilerParams(dimension_semantics=sem, vmem_limit_bytes=VMEM_LIMIT)


def _full(shape):
    return pl.BlockSpec(shape, lambda *_: (0,) * len(shape))


def _ln_kernel(x_ref, g_ref, b_ref, o_ref, ob_ref):
    y = _layernorm_rows(x_ref[...], g_ref[...], b_ref[...])
    o_ref[...] = y
    ob_ref[...] = y.astype(BF16)


def _layernorm(x, g, b, tm=512):
    t, d = x.shape
    row = pl.BlockSpec((tm, d), lambda i: (i, 0))
    return pl.pallas_call(
        _ln_kernel,
        grid=(t // tm,),
        in_specs=[row, _full((1, d)), _full((1, d))],
        out_specs=[row, row],
        out_shape=[jax.ShapeDtypeStruct((t, d), F32), jax.ShapeDtypeStruct((t, d), BF16)],
        compiler_params=_cparams(("parallel",)),
        name="layernorm_in",
    )(x, g.reshape(1, d), b.reshape(1, d))


def _proj_kernel(x_ref, w_ref, o_ref):
    o_ref[...] = _dot(x_ref[...], w_ref[...]).astype(o_ref.dtype)


def _project(x_bf, w_bf, name, out_dtype=BF16, tm=1024):
    t, k = x_bf.shape
    n = w_bf.shape[1]
    tm = min(tm, t)
    tn = n
    for cand in range(min(n, 2048), 0, -LANES):
        if n % cand == 0:
            tn = cand
            break
    return pl.pallas_call(
        _proj_kernel,
        grid=(n // tn, t // tm),
        in_specs=[pl.BlockSpec((tm, k), lambda j, i: (i, 0)), pl.BlockSpec((k, tn), lambda j, i: (0, j))],
        out_specs=pl.BlockSpec((tm, tn), lambda j, i: (i, j)),
        out_shape=jax.ShapeDtypeStruct((t, n), out_dtype),
        compiler_params=_cparams(("parallel", "parallel")),
        name=name,
    )(x_bf, w_bf)


def _head_expand(n_heads, width):
    h = lax.broadcasted_iota(jnp.int32, (LANES, n_heads * width), 0)
    c = lax.broadcasted_iota(jnp.int32, (LANES, n_heads * width), 1)
    return (c // width == h).astype(BF16)


def _ssd_kernel(z_ref, xbc_ref, dt_ref, cw_ref, cb_ref, dtb_ref, a_ref, dsk_ref, nw_ref, e_ref,
                o_ref, pad_ref, st_ref):
    c = pl.program_id(1)
    n_pairs = SSD_HEADS // 2

    @pl.when(c == 0)
    def _():
        pad_ref[0:8, :] = jnp.zeros((8, SSD_XBC), F32)
        st_ref[...] = jnp.zeros(st_ref.shape, F32)

    pad_ref[8:8 + CHUNK, :] = xbc_ref[...].astype(F32)
    acc = jnp.broadcast_to(cb_ref[...], (CHUNK, SSD_XBC))
    for j in range(SSD_CONV):
        acc = acc + cw_ref[j:j + 1, :] * pad_ref[pl.ds(8 - (SSD_CONV - 1) + j, CHUNK), :]
    pad_ref[0:8, :] = pad_ref[CHUNK:CHUNK + 8, :]
    xbc = _silu(acc)
    xs = xbc[:, :D_MODEL]

    e_mat = e_ref[...]
    dt = _softplus(dt_ref[...] + dtb_ref[...])
    da = dt * a_ref[...]
    tri = _tri(CHUNK).astype(BF16)
    acs = _sel_l(tri, da)
    acs_e = _sel_r(acs, e_mat)
    dt_e = _sel_r(dt, e_mat)

    li = lax.broadcasted_iota(jnp.int32, (CHUNK, D_MODEL), 0)
    si = lax.broadcasted_iota(jnp.int32, (CHUNK, D_MODEL), 1) & (SSD_HEADDIM - 1)
    acs_row = jnp.sum(jnp.where(li == si, acs_e, 0.0), axis=0, keepdims=True)
    decay = jnp.exp(jnp.where(si <= li, acs_e - acs_row, -jnp.inf))
    xs_dt = xs * dt_e
    exp_acs = jnp.exp(acs_e)
    acs_last = acs_e[CHUNK - 1:CHUNK, :]
    to_end = jnp.exp(acs_last - acs_e)
    exp_last = jnp.exp(acs_last)

    lane = lax.broadcasted_iota(jnp.int32, (CHUNK, LANES), 1)
    m0 = lane < SSD_HEADDIM

    pairs = range(n_pairs)
    sls = [slice(j * LANES, (j + 1) * LANES) for j in pairs]
    grp = [j // (n_pairs // SSD_GROUPS) for j in pairs]
    bm = [xbc[:, D_MODEL + g * SSD_STATE:D_MODEL + (g + 1) * SSD_STATE].astype(BF16) for g in range(SSD_GROUPS)]
    cm = [xbc[:, D_MODEL + (SSD_GROUPS + g) * SSD_STATE:D_MODEL + (SSD_GROUPS + g + 1) * SSD_STATE].astype(BF16)
          for g in range(SSD_GROUPS)]
    cb2 = [_dot(cm[g], jnp.concatenate([bm[g], bm[g]], axis=0), _NT) for g in range(SSD_GROUPS)]
    st = [st_ref[j] for j in pairs]
    xp = [xs_dt[:, sl] for sl in sls]
    xbd = [jnp.concatenate([jnp.where(m0, x, 0.0), jnp.where(m0, 0.0, x)], axis=0) for x in xp]
    y_parts = [_bdot(cb2[grp[j]] * decay[:, sls[j]], xbd[j]) + _bdot(cm[grp[j]], st[j]) * exp_acs[:, sls[j]]
               for j in pairs]
    upd = [_bdot(bm[grp[j]], xp[j] * to_end[:, sls[j]], _TN) for j in pairs]
    for j in pairs:
        st_ref[j] = st[j] * exp_last[:, sls[j]] + upd[j]
    y = jnp.concatenate(y_parts, axis=1) + xs * dsk_ref[...]
    y = y * _silu(z_ref[...].astype(F32))
    gw = D_MODEL // SSD_GROUPS
    for g in range(SSD_GROUPS):
        yg = y[:, g * gw:(g + 1) * gw]
        ms = jnp.mean(yg * yg, axis=-1, keepdims=True)
        o_ref[:, g * gw:(g + 1) * gw] = (yg * lax.rsqrt(ms + RMS_EPS) * nw_ref[:, g * gw:(g + 1) * gw]
                                         ).astype(o_ref.dtype)


def _ssd(z, xbc, dt_raw, conv_w, conv_b, dt_bias, a_log, d_skip, norm_w, batch):
    t = z.shape[0]
    nc = t // batch // CHUNK
    pad16 = lambda v: jnp.pad(v.astype(F32), (0, LANES - SSD_HEADS)).reshape(1, LANES)
    row = lambda w: pl.BlockSpec((CHUNK, w), lambda b, c: (b * nc + c, 0))
    return pl.pallas_call(
        _ssd_kernel,
        grid=(batch, nc),
        in_specs=[row(D_MODEL), row(SSD_XBC), row(LANES), _full((SSD_CONV, SSD_XBC)), _full((1, SSD_XBC)),
                  _full((1, LANES)), _full((1, LANES)), _full((1, D_MODEL)), _full((1, D_MODEL)),
                  _full((LANES, D_MODEL))],
        out_specs=row(D_MODEL),
        out_shape=jax.ShapeDtypeStruct((t, D_MODEL), BF16),
        scratch_shapes=[pltpu.VMEM((CHUNK + 8, SSD_XBC), F32),
                        pltpu.VMEM((SSD_HEADS // 2, SSD_STATE, LANES), F32)],
        compiler_params=_cparams(("parallel", "arbitrary")),
        name="ssd",
    )(z, xbc, dt_raw, conv_w, conv_b.reshape(1, -1), pad16(dt_bias), pad16(-jnp.exp(a_log.astype(F32))),
      jnp.repeat(d_skip.astype(F32), SSD_HEADDIM).reshape(1, -1), norm_w.reshape(1, -1),
      _head_expand(SSD_HEADS, SSD_HEADDIM))


def _hgrn_kernel(f_ref, lb_ref, nw_ref, o_ref, st_ref):
    c = pl.program_id(1)

    @pl.when(c == 0)
    def _():
        st_ref[...] = jnp.zeros(st_ref.shape, F32)

    d = D_MODEL
    lb = lb_ref[...]
    q = _silu(f_ref[:, 0:d].astype(F32))
    forget = lb + (1.0 - lb) * _sigmoid(f_ref[:, d:2 * d].astype(F32))
    k = 1.0 - forget
    v = f_ref[:, 2 * d:3 * d]
    bc = _sel_l(_tri(CHUNK).astype(BF16), jnp.log(forget))
    mid = bc[CHUNK // 2:CHUNK // 2 + 1, :]
    last = bc[CHUNK - 1:CHUNK, :]
    qe = q * jnp.exp(bc - mid)
    ke = k * jnp.exp(mid - bc)
    qd = q * jnp.exp(bc)
    kd = k * jnp.exp(last - bc)
    w_last = jnp.exp(last)
    causal = _tri(CHUNK)
    heads = range(HGRN_HEADS)
    sls = [slice(h * HGRN_EXPAND, (h + 1) * HGRN_EXPAND) for h in heads]
    vb = [v[:, sl].astype(BF16) for sl in sls]
    st = [st_ref[h] for h in heads]
    att = [jnp.where(causal, _bdot(qe[:, sl], ke[:, sl], _NT), 0.0) for sl in sls]
    o = [_bdot(att[h], vb[h]) + _bdot(qd[:, sls[h]], st[h], _NT) for h in heads]
    upd = [_bdot(vb[h], kd[:, sls[h]], _TN) for h in heads]
    for h in heads:
        st_ref[h] = st[h] * w_last[:, sls[h]] + upd[h]
        oh = o[h] * lax.rsqrt(jnp.mean(o[h] * o[h], axis=-1, keepdims=True) + RMS_EPS) * nw_ref[...]
        gate = _sigmoid(f_ref[:, 3 * d + h * HGRN_EXPAND:3 * d + (h + 1) * HGRN_EXPAND].astype(F32))
        o_ref[:, sls[h]] = (oh * gate).astype(o_ref.dtype)


def _hgrn(feat, lb, norm_w, batch):
    t = feat.shape[0]
    nc = t // batch // CHUNK
    row = lambda w: pl.BlockSpec((CHUNK, w), lambda b, c: (b * nc + c, 0))
    return pl.pallas_call(
        _hgrn_kernel,
        grid=(batch, nc),
        in_specs=[row(4 * D_MODEL), _full((1, D_MODEL)), _full((1, HGRN_EXPAND))],
        out_specs=row(D_MODEL),
        out_shape=jax.ShapeDtypeStruct((t, D_MODEL), BF16),
        scratch_shapes=[pltpu.VMEM((HGRN_HEADS, HGRN_EXPAND, HGRN_EXPAND), F32)],
        compiler_params=_cparams(("parallel", "arbitrary")),
        name="hgrn2",
    )(feat, lb.reshape(1, -1), norm_w.reshape(1, -1))


def _rwkv_kernel(f_ref, mu_ref, w0_ref, w2_ref, a0_ref, a2_ref, g2_ref, kk_ref, ka_ref, rk_ref, lnw_ref,
                 lnb_ref, e_ref, et_ref, o_ref, pad_ref, st_ref, y_ref):
    c = pl.program_id(1)
    d = D_MODEL
    n_pairs = RWKV_HEADS // 2

    @pl.when(c == 0)
    def _():
        pad_ref[0:8, :] = jnp.zeros((8, RWKV_COLS), F32)
        st_ref[...] = jnp.zeros(st_ref.shape, F32)

    cur = f_ref[...].astype(F32)
    pad_ref[8:8 + CHUNK, :] = cur
    prev = pad_ref[pl.ds(7, CHUNK), :]
    pad_ref[0:8, :] = pad_ref[CHUNK:CHUNK + 8, :]
    x = cur + (prev - cur) * mu_ref[...]
    r, k, v = x[:, 0:d], x[:, d:2 * d], x[:, 2 * d:3 * d]
    lora_in = x[:, 3 * d:3 * d + LANES]
    xg = x[:, 3 * d + LANES:3 * d + 2 * LANES]

    e_mat, et_mat = e_ref[...], et_ref[...]
    w = -_softplus(-(w0_ref[...] + _bdot(jnp.tanh(lora_in), w2_ref[...]))) - 0.5
    ld = -jnp.exp(w)
    a = _sigmoid(a0_ref[...] + _bdot(lora_in, a2_ref[...]))
    gate = _bdot(_sigmoid(xg), g2_ref[...])
    kk = k * kk_ref[...]
    nrm = jnp.maximum(jnp.sqrt(_sel_r(kk * kk, et_mat)), 1e-12)
    kk = kk * _sel_r(1.0 / nrm, e_mat)
    k2 = k * (1.0 + (a - 1.0) * ka_ref[...])
    cw = _sel_l(_tri(CHUNK).astype(BF16), ld)
    e_neg = jnp.exp(-cw)
    a_t = -kk * jnp.exp(cw - ld)
    r_t = r * jnp.exp(cw)
    b_t = kk * a * e_neg
    k_t = k2 * e_neg
    w_last = jnp.exp(cw[CHUNK - 1:CHUNK, :])

    lane = lax.broadcasted_iota(jnp.int32, (CHUNK, LANES), 1)
    m0 = lane < RWKV_HEAD
    row2 = lax.broadcasted_iota(jnp.int32, (LANES, LANES), 0)
    col2 = lax.broadcasted_iota(jnp.int32, (LANES, LANES), 1)
    top = row2 < CHUNK
    same = (row2 < CHUNK) == (col2 < RWKV_HEAD)
    bd_strict = same & ((col2 & (CHUNK - 1)) < (row2 & (CHUNK - 1)))
    bd_incl = same & ((col2 & (CHUNK - 1)) <= (row2 & (CHUNK - 1)))

    def stack(p):
        return jnp.concatenate([jnp.where(m0, p, 0.0), jnp.where(m0, 0.0, p)], axis=0)

    pairs = range(n_pairs)
    sls = [slice(j * LANES, (j + 1) * LANES) for j in pairs]
    bf = lambda xs: [x.astype(BF16) for x in xs]
    s1 = bf(stack(a_t[:, sl]) for sl in sls)
    s2 = bf(stack(r_t[:, sl]) for sl in sls)
    vst = bf(stack(v[:, sl]) for sl in sls)
    qmat = bf(jnp.concatenate([b_t[:, sl], k_t[:, sl]], axis=0) for sl in sls)
    mt = [st_ref[j] for j in pairs]
    mtb = bf(mt)
    gm = [_dot(s1[j], qmat[j], _NT) for j in pairs]
    hm = [_dot(s2[j], qmat[j], _NT) for j in pairs]
    gr = [pltpu.roll(g, CHUNK, 1) for g in gm]
    hr = [pltpu.roll(h, CHUNK, 1) for h in hm]
    n_ab = [jnp.where(bd_strict, jnp.where(top, gm[j], gr[j]), 0.0) for j in pairs]
    a_ak = bf(jnp.where(bd_strict, jnp.where(top, gr[j], gm[j]), 0.0) for j in pairs)
    a_rb = bf(jnp.where(bd_incl, jnp.where(top, hm[j], hr[j]), 0.0) for j in pairs)
    a_rk = bf(jnp.where(bd_incl, jnp.where(top, hr[j], hm[j]), 0.0) for j in pairs)
    xm = [_dot(s1[j], mtb[j], _NT) + _dot(a_ak[j], vst[j]) for j in pairs]
    tp = n_ab
    pw = bf(n_ab)
    for _ in range(int(math.log2(CHUNK)) - 1):
        pwf = [_dot(p, p) for p in pw]
        pw = bf(pwf)
        tp = [tp[j] + pwf[j] + _bdot(tp[j], pw[j]) for j in pairs]
    u = [xm[j] + _bdot(tp[j], xm[j]) for j in pairs]
    yst = [_dot(s2[j], mtb[j], _NT) + _bdot(a_rb[j], u[j]) + _dot(a_rk[j], vst[j]) for j in pairs]
    for j in pairs:
        y_ref[:, sls[j]] = yst[j][:CHUNK] + yst[j][CHUNK:]
    uv = [jnp.concatenate([u[j][:CHUNK] + u[j][CHUNK:], v[:, sls[j]]], axis=0) for j in pairs]
    upd = [jnp.where(same, _bdot(uv[j], qmat[j], _TN), 0.0) for j in pairs]
    for j in pairs:
        st_ref[j] = (mt[j] + upd[j]) * w_last[:, sls[j]]

    y = y_ref[...]
    inv_n = 1.0 / RWKV_HEAD
    yc = y - _sel_r(_sel_r(y, et_mat) * inv_n, e_mat)
    rs = lax.rsqrt(_sel_r(yc * yc, et_mat) * inv_n + RWKV_GN_EPS)
    yn = yc * _sel_r(rs, e_mat) * lnw_ref[...] + lnb_ref[...]
    bonus = _sel_r(_sel_r(r * k2 * rk_ref[...], et_mat), e_mat) * v
    o_ref[...] = ((yn + bonus) * gate).astype(o_ref.dtype)


def _rwkv(feat, mu, w0, w2, a0, a2, g2, k_k, k_a, r_k, ln_w, ln_b, batch):
    t = feat.shape[0]
    nc = t // batch // CHUNK
    d = D_MODEL
    row = lambda w: pl.BlockSpec((CHUNK, w), lambda b, c: (b * nc + c, 0))
    vec = lambda v: v.astype(F32).reshape(1, -1)
    w2p = jnp.concatenate([w2, jnp.zeros_like(w2)], axis=0).astype(BF16)
    a2p = jnp.concatenate([jnp.zeros_like(a2), a2], axis=0).astype(BF16)
    e_mat = _head_expand(RWKV_HEADS, RWKV_HEAD)
    return pl.pallas_call(
        _rwkv_kernel,
        grid=(batch, nc),
        in_specs=[row(RWKV_COLS), _full((1, RWKV_COLS)), _full((1, d)), _full((LANES, d)), _full((1, d)),
                  _full((LANES, d)), _full((LANES, d)), _full((1, d)), _full((1, d)), _full((1, d)),
                  _full((1, d)), _full((1, d)), _full((LANES, d)), _full((d, LANES))],
        out_specs=row(d),
        out_shape=jax.ShapeDtypeStruct((t, d), BF16),
        scratch_shapes=[pltpu.VMEM((CHUNK + 8, RWKV_COLS), F32),
                        pltpu.VMEM((RWKV_HEADS // 2, LANES, LANES), F32),
                        pltpu.VMEM((CHUNK, d), F32)],
        compiler_params=_cparams(("parallel", "arbitrary")),
        name="rwkv7",
    )(feat, vec(mu), vec(w0), w2p, vec(a0), a2p, g2.astype(BF16), vec(k_k), vec(k_a), vec(r_k), vec(ln_w),
      vec(ln_b), e_mat, e_mat.T)


def _merge_kernel(ya_ref, yb_ref, yc_ref, g_ref, h_ref, wa_ref, wb_ref, wc_ref, wo_ref, lg_ref, lb_ref, o_ref):
    d = D_MODEL
    gate = lambda j: _sigmoid(g_ref[:, j * d:(j + 1) * d].astype(F32))
    m = (gate(0) * _dot(ya_ref[...], wa_ref[...]) + gate(1) * _dot(yb_ref[...], wb_ref[...])
         + gate(2) * _dot(yc_ref[...], wc_ref[...]))
    hn = DN_ALPHA * h_ref[...] + _bdot(m, wo_ref[...])
    o_ref[...] = _layernorm_rows(hn, lg_ref[...], lb_ref[...])


def _merge(ya, yb, yc, gates, h, wa, wb, wc, wo, ln_g, ln_b, tm=512):
    t, d = h.shape
    tm = min(tm, t)
    row = lambda w: pl.BlockSpec((tm, w), lambda i: (i, 0))
    return pl.pallas_call(
        _merge_kernel,
        grid=(t // tm,),
        in_specs=[row(d), row(d), row(d), row(3 * d), row(d)] + [_full((d, d))] * 4 + [_full((1, d))] * 2,
        out_specs=row(d),
        out_shape=jax.ShapeDtypeStruct((t, d), F32),
        compiler_params=_cparams(("parallel",)),
        name="merge_ln",
    )(ya, yb, yc, gates, h, wa.astype(BF16), wb.astype(BF16), wc.astype(BF16), wo.astype(BF16),
      ln_g.reshape(1, d), ln_b.reshape(1, d))


def _router_kernel(h_ref, wt_ref, bias_ref, o_ref):
    logits = lax.dot_general(wt_ref[...], h_ref[...], _NT, precision=lax.Precision.HIGHEST,
                             preferred_element_type=F32)
    mx = jnp.max(logits, axis=0, keepdims=True)
    ex = jnp.exp(logits - mx)
    probs = ex / jnp.sum(ex, axis=0, keepdims=True)
    sel = probs + bias_ref[...]
    rows = [sel[e:e + 1, :] for e in range(N_EXPERTS)]
    prow = [probs[e:e + 1, :] for e in range(N_EXPERTS)]
    gscore = []
    for g in range(N_GROUPS):
        m = rows[g * EXPERTS_PER_GROUP:(g + 1) * EXPERTS_PER_GROUP]
        best = None
        for i in range(EXPERTS_PER_GROUP):
            for j in range(i + 1, EXPERTS_PER_GROUP):
                s = m[i] + m[j]
                best = s if best is None else jnp.maximum(best, s)
        gscore.append(best)
    chosen = []
    for g in range(N_GROUPS):
        ok = None
        for g2 in range(N_GROUPS):
            if g2 == g:
                continue
            t = (gscore[g] > gscore[g2]) if g2 < g else (gscore[g] >= gscore[g2])
            ok = t if ok is None else (ok & t)
        chosen.append(ok)
    picked = []
    for e in range(N_EXPERTS):
        g = e // EXPERTS_PER_GROUP
        rank = None
        for e2 in range(g * EXPERTS_PER_GROUP, (g + 1) * EXPERTS_PER_GROUP):
            if e2 == e:
                continue
            ahead = (rows[e2] >= rows[e]) if e2 < e else (rows[e2] > rows[e])
            ahead = ahead.astype(F32)
            rank = ahead if rank is None else rank + ahead
        picked.append(jnp.where(chosen[g] & (rank < 2.0), prow[e], 0.0))
    tot = picked[0]
    for e in range(1, N_EXPERTS):
        tot = tot + picked[e]
    inv = 1.0 / tot
    for e in range(N_EXPERTS):
        o_ref[e:e + 1, :] = picked[e] * inv


def _router(h, router_w, router_bias, tm=512):
    t, d = h.shape
    tm = min(tm, t)
    return pl.pallas_call(
        _router_kernel,
        grid=(t // tm,),
        in_specs=[pl.BlockSpec((tm, d), lambda i: (i, 0)), _full((N_EXPERTS, d)), _full((N_EXPERTS, 1))],
        out_specs=pl.BlockSpec((N_EXPERTS, tm), lambda i: (0, i)),
        out_shape=jax.ShapeDtypeStruct((N_EXPERTS, t), F32),
        compiler_params=_cparams(("parallel",)),
        name="router",
    )(h, router_w.T, router_bias.reshape(N_EXPERTS, 1).astype(F32))


MOE_TILE = 256
_PAIRS = [(a, b) for a in range(EXPERTS_PER_GROUP) for b in range(a + 1, EXPERTS_PER_GROUP)]
N_CLASSES = N_GROUPS * len(_PAIRS)


def _moe_num_tiles(t):
    return -(-(t + N_CLASSES * (MOE_TILE - 1)) // MOE_TILE)


def _route_meta(gates_t):
    e_n, t = gates_t.shape
    i32 = jnp.int32
    mask = gates_t > 0
    eidx = lax.broadcasted_iota(i32, (e_n, t), 0)
    e0 = jnp.minimum(jnp.min(jnp.where(mask, eidx, e_n), axis=0), e_n - 1)
    e1 = jnp.maximum(jnp.max(jnp.where(mask, eidx, -1), axis=0), e0)
    grp = e0 // EXPERTS_PER_GROUP
    a = e0 % EXPERTS_PER_GROUP
    b = jnp.where(e1 // EXPERTS_PER_GROUP == grp, e1 % EXPERTS_PER_GROUP, a)
    b = jnp.where(b == a, (a + 1) % EXPERTS_PER_GROUP, b)
    lo, hi = jnp.minimum(a, b), jnp.maximum(a, b)
    n_pairs = len(_PAIRS)
    cls = grp * n_pairs + (lo * (2 * EXPERTS_PER_GROUP - 1 - lo)) // 2 + (hi - lo - 1)
    g_lo = jnp.take_along_axis(gates_t, (grp * EXPERTS_PER_GROUP + lo)[None, :], axis=0)[0]
    g_hi = jnp.take_along_axis(gates_t, (grp * EXPERTS_PER_GROUP + hi)[None, :], axis=0)[0]
    onehot = (cls[None, :] == jnp.arange(N_CLASSES, dtype=i32)[:, None]).astype(i32)
    csum = jnp.cumsum(onehot, axis=1)
    cnt = csum[:, -1]
    rank = jnp.sum(onehot * (csum - 1), axis=0)
    ptiles = (cnt + MOE_TILE - 1) // MOE_TILE
    tile_end = jnp.cumsum(ptiles)
    tile_off = tile_end - ptiles
    pos = jnp.take(tile_off, cls) * MOE_TILE + rank
    n_tiles = _moe_num_tiles(t)
    row_token = jnp.zeros((n_tiles * MOE_TILE,), i32).at[pos].set(jnp.arange(t, dtype=i32), unique_indices=True)
    tid = jnp.arange(n_tiles, dtype=i32)
    tcls = jnp.minimum(jnp.sum((tid[:, None] >= tile_end[None, :]).astype(i32), axis=1), N_CLASSES - 1)
    n_valid = jnp.clip(jnp.take(cnt, tcls) - (tid - jnp.take(tile_off, tcls)) * MOE_TILE, 0, MOE_TILE)
    n_valid = jnp.where(tid < tile_end[-1], n_valid, 0).astype(i32)
    cls_lo = jnp.array([g * EXPERTS_PER_GROUP + p[0] for g in range(N_GROUPS) for p in _PAIRS], i32)
    cls_hi = jnp.array([g * EXPERTS_PER_GROUP + p[1] for g in range(N_GROUPS) for p in _PAIRS], i32)
    row_gates = jnp.stack([jnp.take(g_lo, row_token), jnp.take(g_hi, row_token)], axis=1)
    return row_token, jnp.take(cls_lo, tcls), jnp.take(cls_hi, tcls), n_valid, row_gates


def _moe_kernel(rt_ref, ea_ref, eb_ref, nv_ref, h_hbm, rg_ref, wga_ref, wua_ref, wda_ref, wgb_ref, wub_ref,
                wdb_ref, lg_ref, lb_ref, out_hbm, xbuf, obuf, sem_in, sem_out):
    i = pl.program_id(0)
    n = pl.num_programs(0)
    slot = lax.rem(i, 2)

    def row_in(row, s, k):
        return pltpu.make_async_copy(h_hbm.at[pl.ds(row, 1)], xbuf.at[s, pl.ds(k, 1)], sem_in.at[s])

    def row_out(row, s, k):
        return pltpu.make_async_copy(obuf.at[s, pl.ds(k, 1)], out_hbm.at[pl.ds(row, 1)], sem_out.at[s])

    def start_gather(tile, s):
        def body(k, c):
            row_in(rt_ref[tile * MOE_TILE + k], s, k).start()
            return c
        lax.fori_loop(0, nv_ref[tile], body, 0)

    def wait_rows(copy, count):
        def body(k, c):
            copy.wait()
            return c
        lax.fori_loop(0, count, body, 0)

    @pl.when(i == 0)
    def _():
        xbuf[...] = jnp.zeros(xbuf.shape, F32)
        start_gather(0, 0)

    @pl.when(i + 1 < n)
    def _():
        start_gather(i + 1, 1 - slot)

    wait_rows(row_in(0, slot, 0), nv_ref[i])

    @pl.when(i >= 2)
    def _():
        wait_rows(row_out(0, slot, 0), nv_ref[i - 2])

    @pl.when(nv_ref[i] > 0)
    def _():
        x = xbuf[slot]
        xb = x.astype(BF16)
        rg = rg_ref[...]
        act_a = (_silu(_dot(xb, wga_ref[0])) * _dot(xb, wua_ref[0])).astype(BF16)
        act_b = (_silu(_dot(xb, wgb_ref[0])) * _dot(xb, wub_ref[0])).astype(BF16)
        moe = rg[:, 0:1] * _dot(act_a, wda_ref[0]) + rg[:, 1:2] * _dot(act_b, wdb_ref[0])
        obuf[slot] = _layernorm_rows(DN_ALPHA * x + moe, lg_ref[...], lb_ref[...])

        def body(k, c):
            row_out(rt_ref[i * MOE_TILE + k], slot, k).start()
            return c
        lax.fori_loop(0, nv_ref[i], body, 0)

    @pl.when(i == n - 1)
    def _():
        @pl.when(i >= 1)
        def _():
            wait_rows(row_out(0, 1 - slot, 0), nv_ref[i - 1])
        wait_rows(row_out(0, slot, 0), nv_ref[i])


def _moe(h, gates_t, wg, wu, wd, ln_g, ln_b):
    t, d = h.shape
    n_tiles = _moe_num_tiles(t)
    row_token, exp_a, exp_b, n_valid, row_gates = _route_meta(gates_t)
    w_up = lambda which: pl.BlockSpec((1, d, D_EXPERT), lambda i, rt, ea, eb, nv: ((ea, eb)[which][i], 0, 0))
    w_dn = lambda which: pl.BlockSpec((1, D_EXPERT, d), lambda i, rt, ea, eb, nv: ((ea, eb)[which][i], 0, 0))
    const = lambda shape: pl.BlockSpec(shape, lambda i, *_: (0,) * len(shape))
    grid_spec = pltpu.PrefetchScalarGridSpec(
        num_scalar_prefetch=4,
        grid=(n_tiles,),
        in_specs=[pl.BlockSpec(memory_space=pl.ANY),
                  pl.BlockSpec((MOE_TILE, 2), lambda i, *_: (i, 0)),
                  w_up(0), w_up(0), w_dn(0), w_up(1), w_up(1), w_dn(1), const((1, d)), const((1, d))],
        out_specs=pl.BlockSpec(memory_space=pl.ANY),
        scratch_shapes=[pltpu.VMEM((2, MOE_TILE, d), F32), pltpu.VMEM((2, MOE_TILE, d), F32),
                        pltpu.SemaphoreType.DMA((2,)), pltpu.SemaphoreType.DMA((2,))],
    )
    return pl.pallas_call(
        _moe_kernel,
        grid_spec=grid_spec,
        out_shape=jax.ShapeDtypeStruct((t, d), F32),
        compiler_params=_cparams(("arbitrary",)),
        name="experts_ln",
    )(row_token, exp_a, exp_b, n_valid, h, row_gates, wg, wu, wd, wg, wu, wd, ln_g.reshape(1, d), ln_b.reshape(1, d))


def _in_weights(w_in_l):
    offs = [0]
    for s in IN_SIZES:
        offs.append(offs[-1] + s)
    seg = [w_in_l[:, offs[i]:offs[i + 1]] for i in range(len(IN_SIZES))]
    seg[2] = jnp.pad(seg[2], ((0, 0), (0, LANES - SSD_HEADS)))
    return [s.astype(BF16) for s in seg]


def kernel(x, ln_in_g, ln_in_b, w_in, ssd_conv_w, ssd_conv_b, ssd_dt_bias, ssd_a_log, ssd_d, ssd_norm_w, rwkv_mu, rwkv_w0, rwkv_w2, rwkv_a0, rwkv_a2, rwkv_g2, rwkv_k_k, rwkv_k_a, rwkv_r_k, rwkv_ln_w, rwkv_ln_b, hgrn_lb, hgrn_norm_w, w_br_ssd, w_br_rwkv, w_br_hgrn, w_out, ln1_g, ln1_b, router_w, router_bias, exp_w_gate, exp_w_up, exp_w_down, ln2_g, ln2_b):
    batch, seq, d = x.shape
    t = batch * seq
    h, h_bf = _layernorm(x.reshape(t, d), ln_in_g, ln_in_b)
    lsm = jax.nn.softmax(hgrn_lb.astype(F32), axis=0)
    lower_bounds = jnp.cumsum(lsm, axis=0) - lsm[0]
    for l in range(DEPTH):
        names = ("z", "xbc", "dt", "rwkv", "hgrn", "gates")
        z, xbc, dt_raw, f_rwkv, f_hgrn, gates = [
            _project(h_bf, w, f"proj_{n}", F32 if n == "dt" else BF16)
            for w, n in zip(_in_weights(w_in[l]), names)]
        y_a = _ssd(z, xbc, dt_raw, ssd_conv_w[l], ssd_conv_b[l], ssd_dt_bias[l], ssd_a_log[l], ssd_d[l],
                   ssd_norm_w[l], batch)
        y_b = _rwkv(f_rwkv, rwkv_mu[l], rwkv_w0[l], rwkv_w2[l], rwkv_a0[l], rwkv_a2[l], rwkv_g2[l],
                    rwkv_k_k[l], rwkv_k_a[l], rwkv_r_k[l].reshape(-1), rwkv_ln_w[l], rwkv_ln_b[l], batch)
        y_c = _hgrn(f_hgrn, lower_bounds[l], hgrn_norm_w[l], batch)
        h = _merge(y_a, y_b, y_c, gates, h, w_br_ssd[l], w_br_rwkv[l], w_br_hgrn[l], w_out[l],
                   ln1_g[l], ln1_b[l])
        gates_t = _router(h, router_w, router_bias)
        h = _moe(h, gates_t, exp_w_gate[l].astype(BF16), exp_w_up[l].astype(BF16),
                 exp_w_down[l].astype(BF16), ln2_g[l], ln2_b[l])
        h_bf = h.astype(BF16)
    return h.reshape(batch, seq, d)
```

```python
import functools
import math

import jax
import jax.numpy as jnp
from jax import lax
from jax.experimental import pallas as pl
from jax.experimental.pallas import tpu as pltpu

F32 = jnp.float32
BF16 = jnp.bfloat16

D_MODEL = 1024
DEPTH = 2
CHUNK = 64
LANES = 128

SSD_HEADS = 16
SSD_HEADDIM = 64
SSD_GROUPS = 2
SSD_STATE = 128
SSD_CONV = 4
SSD_XBC = D_MODEL + 2 * SSD_GROUPS * SSD_STATE

RWKV_HEADS = 16
RWKV_HEAD = 64
RWKV_COLS = 3 * D_MODEL + 64 + 64 + 128
RWKV_GN_EPS = 64e-5

HGRN_HEADS = 8
HGRN_EXPAND = 128

IN_SIZES = (D_MODEL, SSD_XBC, SSD_HEADS, RWKV_COLS, 4 * D_MODEL, 3 * D_MODEL)

N_EXPERTS = 16
N_GROUPS = 4
EXPERTS_PER_GROUP = 4
D_EXPERT = 512

DN_ALPHA = (2 * DEPTH) ** 0.25
LN_EPS = 1e-5
RMS_EPS = 1e-6

VMEM_LIMIT = 56 * 1024 * 1024

_NN = (((1,), (0,)), ((), ()))
_NT = (((1,), (1,)), ((), ()))
_TN = (((0,), (0,)), ((), ()))


def _dot(a, b, dims=_NN):
    return lax.dot_general(a, b, dims, preferred_element_type=F32)


def _bdot(a, b, dims=_NN):
    return _dot(a.astype(BF16), b.astype(BF16), dims)


def _split3(x):
    x1 = x.astype(BF16)
    r1 = x - x1.astype(F32)
    x2 = r1.astype(BF16)
    x3 = (r1 - x2.astype(F32)).astype(BF16)
    return x1, x2, x3


def _sel_r(x, sel, pieces=3):
    n = x.shape[0]
    if pieces == 1:
        return _bdot(x, sel)
    y = _dot(jnp.concatenate(_split3(x)[:pieces], axis=0), sel)
    out = y[:n] + y[n:2 * n]
    return out + y[2 * n:] if pieces == 3 else out


def _cumsum_rows(x):
    row = lax.broadcasted_iota(jnp.int32, x.shape, 0)
    shift = 1
    while shift < x.shape[0]:
        x = x + jnp.where(row >= shift, pltpu.roll(x, shift, 0), 0.0)
        shift *= 2
    return x


def _sigmoid(x):
    return 1.0 / (1.0 + jnp.exp(-x))


def _silu(x):
    return x * _sigmoid(x)


def _softplus(x):
    return jnp.maximum(x, 0.0) + jnp.log1p(jnp.exp(-jnp.abs(x)))


def _layernorm_rows(x, g, b):
    mu = jnp.mean(x, axis=-1, keepdims=True)
    xc = x - mu
    var = jnp.mean(xc * xc, axis=-1, keepdims=True)
    return xc * lax.rsqrt(var + LN_EPS) * g + b


def _tri(n, strict=False):
    r = lax.broadcasted_iota(jnp.int32, (n, n), 0)
    c = lax.broadcasted_iota(jnp.int32, (n, n), 1)
    return (c < r) if strict else (c <= r)


def _cparams(sem):
    return pltpu.CompilerParams(dimension_semantics=sem, vmem_limit_bytes=VMEM_LIMIT)


def _full(shape):
    return pl.BlockSpec(shape, lambda *_: (0,) * len(shape))


def _ln_kernel(x_ref, g_ref, b_ref, o_ref, ob_ref):
    y = _layernorm_rows(x_ref[...], g_ref[...], b_ref[...])
    o_ref[...] = y
    ob_ref[...] = y.astype(BF16)


def _layernorm(x, g, b, tm=512):
    t, d = x.shape
    row = pl.BlockSpec((tm, d), lambda i: (i, 0))
    return pl.pallas_call(
        _ln_kernel,
        grid=(t // tm,),
        in_specs=[row, _full((1, d)), _full((1, d))],
        out_specs=[row, row],
        out_shape=[jax.ShapeDtypeStruct((t, d), F32), jax.ShapeDtypeStruct((t, d), BF16)],
        compiler_params=_cparams(("parallel",)),
        name="layernorm_in",
    )(x, g.reshape(1, d), b.reshape(1, d))


def _proj_kernel(x_ref, w_ref, o_ref):
    o_ref[...] = _dot(x_ref[...], w_ref[...]).astype(o_ref.dtype)


def _project(x_bf, w_bf, name, out_dtype=BF16, tm=1024):
    t, k = x_bf.shape
    n = w_bf.shape[1]
    tm = min(tm, t)
    tn = n
    for cand in range(min(n, 2048), 0, -LANES):
        if n % cand == 0:
            tn = cand
            break
    return pl.pallas_call(
        _proj_kernel,
        grid=(n // tn, t // tm),
        in_specs=[pl.BlockSpec((tm, k), lambda j, i: (i, 0)), pl.BlockSpec((k, tn), lambda j, i: (0, j))],
        out_specs=pl.BlockSpec((tm, tn), lambda j, i: (i, j)),
        out_shape=jax.ShapeDtypeStruct((t, n), out_dtype),
        compiler_params=_cparams(("parallel", "parallel")),
        name=name,
    )(x_bf, w_bf)


def _head_expand(n_heads, width):
    h = lax.broadcasted_iota(jnp.int32, (LANES, n_heads * width), 0)
    c = lax.broadcasted_iota(jnp.int32, (LANES, n_heads * width), 1)
    return (c // width == h).astype(BF16)


def _ssd_kernel(z_ref, xbc_ref, dt_ref, cw_ref, cb_ref, dtb_ref, a_ref, dsk_ref, nw_ref, e_ref,
                o_ref, pad_ref, st_ref):
    c = pl.program_id(1)
    n_pairs = SSD_HEADS // 2

    @pl.when(c == 0)
    def _():
        pad_ref[0:8, :] = jnp.zeros((8, SSD_XBC), F32)
        st_ref[...] = jnp.zeros(st_ref.shape, F32)

    pad_ref[8:8 + CHUNK, :] = xbc_ref[...].astype(F32)
    acc = jnp.broadcast_to(cb_ref[...], (CHUNK, SSD_XBC))
    for j in range(SSD_CONV):
        acc = acc + cw_ref[j:j + 1, :] * pad_ref[pl.ds(8 - (SSD_CONV - 1) + j, CHUNK), :]
    pad_ref[0:8, :] = pad_ref[CHUNK:CHUNK + 8, :]
    xbc = _silu(acc)
    xs = xbc[:, :D_MODEL]

    e_mat = e_ref[...]
    dt = _softplus(dt_ref[...] + dtb_ref[...])
    da = dt * a_ref[...]
    acs = _cumsum_rows(da)
    acs_e = _sel_r(acs, e_mat)
    dt_e = _sel_r(dt, e_mat)

    li = lax.broadcasted_iota(jnp.int32, (CHUNK, D_MODEL), 0)
    si = lax.broadcasted_iota(jnp.int32, (CHUNK, D_MODEL), 1) & (SSD_HEADDIM - 1)
    acs_row = jnp.sum(jnp.where(li == si, acs_e, 0.0), axis=0, keepdims=True)
    decay = jnp.exp(jnp.where(si <= li, acs_e - acs_row, -jnp.inf))
    xs_dt = xs * dt_e
    exp_acs = jnp.exp(acs_e)
    acs_last = acs_e[CHUNK - 1:CHUNK, :]
    to_end = jnp.exp(acs_last - acs_e)
    exp_last = jnp.exp(acs_last)

    lane = lax.broadcasted_iota(jnp.int32, (CHUNK, LANES), 1)
    m0 = lane < SSD_HEADDIM

    pairs = range(n_pairs)
    sls = [slice(j * LANES, (j + 1) * LANES) for j in pairs]
    grp = [j // (n_pairs // SSD_GROUPS) for j in pairs]
    bm = [xbc[:, D_MODEL + g * SSD_STATE:D_MODEL + (g + 1) * SSD_STATE].astype(BF16) for g in range(SSD_GROUPS)]
    cm = [xbc[:, D_MODEL + (SSD_GROUPS + g) * SSD_STATE:D_MODEL + (SSD_GROUPS + g + 1) * SSD_STATE].astype(BF16)
          for g in range(SSD_GROUPS)]
    cb2 = [_dot(cm[g], jnp.concatenate([bm[g], bm[g]], axis=0), _NT) for g in range(SSD_GROUPS)]
    st = [st_ref[j] for j in pairs]
    xp = [xs_dt[:, sl] for sl in sls]
    xbd = [jnp.concatenate([jnp.where(m0, x, 0.0), jnp.where(m0, 0.0, x)], axis=0) for x in xp]
    y_parts = [_bdot(cb2[grp[j]] * decay[:, sls[j]], xbd[j]) + _bdot(cm[grp[j]], st[j]) * exp_acs[:, sls[j]]
               for j in pairs]
    upd = [_bdot(bm[grp[j]], xp[j] * to_end[:, sls[j]], _TN) for j in pairs]
    for j in pairs:
        st_ref[j] = st[j] * exp_last[:, sls[j]] + upd[j]
    y = jnp.concatenate(y_parts, axis=1) + xs * dsk_ref[...]
    y = y * _silu(z_ref[...].astype(F32))
    gw = D_MODEL // SSD_GROUPS
    for g in range(SSD_GROUPS):
        yg = y[:, g * gw:(g + 1) * gw]
        ms = jnp.mean(yg * yg, axis=-1, keepdims=True)
        o_ref[:, g * gw:(g + 1) * gw] = (yg * lax.rsqrt(ms + RMS_EPS) * nw_ref[:, g * gw:(g + 1) * gw]
                                         ).astype(o_ref.dtype)


def _ssd(z, xbc, dt_raw, conv_w, conv_b, dt_bias, a_log, d_skip, norm_w, batch):
    t = z.shape[0]
    nc = t // batch // CHUNK
    pad16 = lambda v: jnp.pad(v.astype(F32), (0, LANES - SSD_HEADS)).reshape(1, LANES)
    row = lambda w: pl.BlockSpec((CHUNK, w), lambda b, c: (b * nc + c, 0))
    return pl.pallas_call(
        _ssd_kernel,
        grid=(batch, nc),
        in_specs=[row(D_MODEL), row(SSD_XBC), row(LANES), _full((SSD_CONV, SSD_XBC)), _full((1, SSD_XBC)),
                  _full((1, LANES)), _full((1, LANES)), _full((1, D_MODEL)), _full((1, D_MODEL)),
                  _full((LANES, D_MODEL))],
        out_specs=row(D_MODEL),
        out_shape=jax.ShapeDtypeStruct((t, D_MODEL), BF16),
        scratch_shapes=[pltpu.VMEM((CHUNK + 8, SSD_XBC), F32),
                        pltpu.VMEM((SSD_HEADS // 2, SSD_STATE, LANES), F32)],
        compiler_params=_cparams(("parallel", "arbitrary")),
        name="ssd",
    )(z, xbc, dt_raw, conv_w, conv_b.reshape(1, -1), pad16(dt_bias), pad16(-jnp.exp(a_log.astype(F32))),
      jnp.repeat(d_skip.astype(F32), SSD_HEADDIM).reshape(1, -1), norm_w.reshape(1, -1),
      _head_expand(SSD_HEADS, SSD_HEADDIM))


def _hgrn_kernel(f_ref, lb_ref, nw_ref, o_ref, st_ref):
    c = pl.program_id(1)

    @pl.when(c == 0)
    def _():
        st_ref[...] = jnp.zeros(st_ref.shape, F32)

    d = D_MODEL
    lb = lb_ref[...]
    q = _silu(f_ref[:, 0:d].astype(F32))
    forget = lb + (1.0 - lb) * _sigmoid(f_ref[:, d:2 * d].astype(F32))
    k = 1.0 - forget
    v = f_ref[:, 2 * d:3 * d]
    bc = _cumsum_rows(jnp.log(forget))
    mid = bc[CHUNK // 2:CHUNK // 2 + 1, :]
    last = bc[CHUNK - 1:CHUNK, :]
    qe = q * jnp.exp(bc - mid)
    ke = k * jnp.exp(mid - bc)
    qd = q * jnp.exp(bc)
    kd = k * jnp.exp(last - bc)
    w_last = jnp.exp(last)
    causal = _tri(CHUNK)
    heads = range(HGRN_HEADS)
    sls = [slice(h * HGRN_EXPAND, (h + 1) * HGRN_EXPAND) for h in heads]
    vb = [v[:, sl].astype(BF16) for sl in sls]
    st = [st_ref[h] for h in heads]
    att = [jnp.where(causal, _bdot(qe[:, sl], ke[:, sl], _NT), 0.0) for sl in sls]
    o = [_bdot(att[h], vb[h]) + _bdot(qd[:, sls[h]], st[h], _NT) for h in heads]
    upd = [_bdot(vb[h], kd[:, sls[h]], _TN) for h in heads]
    for h in heads:
        st_ref[h] = st[h] * w_last[:, sls[h]] + upd[h]
        oh = o[h] * lax.rsqrt(jnp.mean(o[h] * o[h], axis=-1, keepdims=True) + RMS_EPS) * nw_ref[...]
        gate = _sigmoid(f_ref[:, 3 * d + h * HGRN_EXPAND:3 * d + (h + 1) * HGRN_EXPAND].astype(F32))
        o_ref[:, sls[h]] = (oh * gate).astype(o_ref.dtype)


def _hgrn(feat, lb, norm_w, batch):
    t = feat.shape[0]
    nc = t // batch // CHUNK
    row = lambda w: pl.BlockSpec((CHUNK, w), lambda b, c: (b * nc + c, 0))
    return pl.pallas_call(
        _hgrn_kernel,
        grid=(batch, nc),
        in_specs=[row(4 * D_MODEL), _full((1, D_MODEL)), _full((1, HGRN_EXPAND))],
        out_specs=row(D_MODEL),
        out_shape=jax.ShapeDtypeStruct((t, D_MODEL), BF16),
        scratch_shapes=[pltpu.VMEM((HGRN_HEADS, HGRN_EXPAND, HGRN_EXPAND), F32)],
        compiler_params=_cparams(("parallel", "arbitrary")),
        name="hgrn2",
    )(feat, lb.reshape(1, -1), norm_w.reshape(1, -1))


def _rwkv_kernel(f_ref, mu_ref, w0_ref, w2_ref, a0_ref, a2_ref, g2_ref, kk_ref, ka_ref, rk_ref, lnw_ref,
                 lnb_ref, e_ref, et_ref, o_ref, pad_ref, st_ref, y_ref):
    c = pl.program_id(1)
    d = D_MODEL
    n_pairs = RWKV_HEADS // 2

    @pl.when(c == 0)
    def _():
        pad_ref[0:8, :] = jnp.zeros((8, RWKV_COLS), F32)
        st_ref[...] = jnp.zeros(st_ref.shape, F32)

    cur = f_ref[...].astype(F32)
    pad_ref[8:8 + CHUNK, :] = cur
    prev = pad_ref[pl.ds(7, CHUNK), :]
    pad_ref[0:8, :] = pad_ref[CHUNK:CHUNK + 8, :]
    x = cur + (prev - cur) * mu_ref[...]
    r, k, v = x[:, 0:d], x[:, d:2 * d], x[:, 2 * d:3 * d]
    lora_in = x[:, 3 * d:3 * d + LANES]
    xg = x[:, 3 * d + LANES:3 * d + 2 * LANES]

    e_mat, et_mat = e_ref[...], et_ref[...]
    w = -_softplus(-(w0_ref[...] + _bdot(jnp.tanh(lora_in), w2_ref[...]))) - 0.5
    ld = -jnp.exp(w)
    a = _sigmoid(a0_ref[...] + _bdot(lora_in, a2_ref[...]))
    gate = _bdot(_sigmoid(xg), g2_ref[...])
    kk = k * kk_ref[...]
    nrm = jnp.maximum(jnp.sqrt(_sel_r(kk * kk, et_mat, 1)), 1e-12)
    kk = kk * _sel_r(1.0 / nrm, e_mat, 2)
    k2 = k * (1.0 + (a - 1.0) * ka_ref[...])
    cw = _cumsum_rows(ld)
    e_neg = jnp.exp(-cw)
    a_t = -kk * jnp.exp(cw - ld)
    r_t = r * jnp.exp(cw)
    b_t = kk * a * e_neg
    k_t = k2 * e_neg
    w_last = jnp.exp(cw[CHUNK - 1:CHUNK, :])

    lane = lax.broadcasted_iota(jnp.int32, (CHUNK, LANES), 1)
    m0 = lane < RWKV_HEAD
    row2 = lax.broadcasted_iota(jnp.int32, (LANES, LANES), 0)
    col2 = lax.broadcasted_iota(jnp.int32, (LANES, LANES), 1)
    same = (row2 < CHUNK) == (col2 < RWKV_HEAD)
    row4 = lax.broadcasted_iota(jnp.int32, (2 * LANES, LANES), 0)
    col4 = lax.broadcasted_iota(jnp.int32, (2 * LANES, LANES), 1)
    top4 = (row4 & (LANES - 1)) < CHUNK
    tt, ss = row4 & (CHUNK - 1), col4 & (CHUNK - 1)
    bd4 = (top4 == (col4 < RWKV_HEAD)) & ((ss < tt) | ((row4 >= LANES) & (ss == tt)))

    def stack(p):
        return jnp.concatenate([jnp.where(m0, p, 0.0), jnp.where(m0, 0.0, p)], axis=0)

    pairs = range(n_pairs)
    sls = [slice(j * LANES, (j + 1) * LANES) for j in pairs]
    bf = lambda xs: [x.astype(BF16) for x in xs]
    s12 = bf(jnp.concatenate([stack(a_t[:, sl]), stack(r_t[:, sl])], axis=0) for sl in sls)
    vst = bf(stack(v[:, sl]) for sl in sls)
    qmat = bf(jnp.concatenate([b_t[:, sl], k_t[:, sl]], axis=0) for sl in sls)
    mt = [st_ref[j] for j in pairs]
    mtb = bf(mt)
    gh = [_dot(s12[j], qmat[j], _NT) for j in pairs]
    ghr = [pltpu.roll(x, CHUNK, 1) for x in gh]
    n_ab, a_rb, akrk = [], [], []
    for j in pairs:
        diag = jnp.where(bd4, jnp.where(top4, gh[j], ghr[j]), 0.0)
        n_ab.append(diag[:LANES])
        a_rb.append(diag[LANES:].astype(BF16))
        akrk.append(jnp.where(bd4, jnp.where(top4, ghr[j], gh[j]), 0.0).astype(BF16))
    sm = [_dot(s12[j], mtb[j], _NT) for j in pairs]
    av = [_dot(akrk[j], vst[j]) for j in pairs]
    xm = [sm[j][:LANES] + av[j][:LANES] for j in pairs]
    tp = n_ab
    pwf = [_bdot(x, x) for x in n_ab]
    n_steps = int(math.log2(CHUNK)) - 1
    for step in range(n_steps):
        pw = bf(pwf)
        if step + 1 < n_steps:
            both = [_dot(jnp.concatenate([pw[j], tp[j].astype(BF16)], axis=0), pw[j]) for j in pairs]
            tp = [tp[j] + pwf[j] + both[j][LANES:] for j in pairs]
            pwf = [both[j][:LANES] for j in pairs]
        else:
            tp = [tp[j] + pwf[j] + _bdot(tp[j], pw[j]) for j in pairs]
    u = [xm[j] + _bdot(tp[j], xm[j]) for j in pairs]
    yst = [sm[j][LANES:] + av[j][LANES:] + _bdot(a_rb[j], u[j]) for j in pairs]
    for j in pairs:
        y_ref[:, sls[j]] = yst[j][:CHUNK] + yst[j][CHUNK:]
    uv = [jnp.concatenate([u[j][:CHUNK] + u[j][CHUNK:], v[:, sls[j]]], axis=0) for j in pairs]
    upd = [jnp.where(same, _bdot(uv[j], qmat[j], _TN), 0.0) for j in pairs]
    for j in pairs:
        st_ref[j] = (mt[j] + upd[j]) * w_last[:, sls[j]]

    y = y_ref[...]
    inv_n = 1.0 / RWKV_HEAD
    yc = y - _sel_r(_sel_r(y, et_mat, 1) * inv_n, e_mat, 2)
    rs = lax.rsqrt(_sel_r(yc * yc, et_mat, 1) * inv_n + RWKV_GN_EPS)
    yn = yc * _sel_r(rs, e_mat, 2) * lnw_ref[...] + lnb_ref[...]
    bonus = _sel_r(_sel_r(r * k2 * rk_ref[...], et_mat, 1), e_mat, 2) * v
    o_ref[...] = ((yn + bonus) * gate).astype(o_ref.dtype)


def _rwkv(feat, mu, w0, w2, a0, a2, g2, k_k, k_a, r_k, ln_w, ln_b, batch):
    t = feat.shape[0]
    nc = t // batch // CHUNK
    d = D_MODEL
    row = lambda w: pl.BlockSpec((CHUNK, w), lambda b, c: (b * nc + c, 0))
    vec = lambda v: v.astype(F32).reshape(1, -1)
    w2p = jnp.concatenate([w2, jnp.zeros_like(w2)], axis=0).astype(BF16)
    a2p = jnp.concatenate([jnp.zeros_like(a2), a2], axis=0).astype(BF16)
    e_mat = _head_expand(RWKV_HEADS, RWKV_HEAD)
    return pl.pallas_call(
        _rwkv_kernel,
        grid=(batch, nc),
        in_specs=[row(RWKV_COLS), _full((1, RWKV_COLS)), _full((1, d)), _full((LANES, d)), _full((1, d)),
                  _full((LANES, d)), _full((LANES, d)), _full((1, d)), _full((1, d)), _full((1, d)),
                  _full((1, d)), _full((1, d)), _full((LANES, d)), _full((d, LANES))],
        out_specs=row(d),
        out_shape=jax.ShapeDtypeStruct((t, d), BF16),
        scratch_shapes=[pltpu.VMEM((CHUNK + 8, RWKV_COLS), F32),
                        pltpu.VMEM((RWKV_HEADS // 2, LANES, LANES), F32),
                        pltpu.VMEM((CHUNK, d), F32)],
        compiler_params=_cparams(("parallel", "arbitrary")),
        name="rwkv7",
    )(feat, vec(mu), vec(w0), w2p, vec(a0), a2p, g2.astype(BF16), vec(k_k), vec(k_a), vec(r_k), vec(ln_w),
      vec(ln_b), e_mat, e_mat.T)


def _merge_kernel(ya_ref, yb_ref, yc_ref, g_ref, h_ref, wa_ref, wb_ref, wc_ref, wo_ref, lg_ref, lb_ref, o_ref):
    d = D_MODEL
    gate = lambda j: _sigmoid(g_ref[:, j * d:(j + 1) * d].astype(F32))
    m = (gate(0) * _dot(ya_ref[...], wa_ref[...]) + gate(1) * _dot(yb_ref[...], wb_ref[...])
         + gate(2) * _dot(yc_ref[...], wc_ref[...]))
    hn = DN_ALPHA * h_ref[...] + _bdot(m, wo_ref[...])
    o_ref[...] = _layernorm_rows(hn, lg_ref[...], lb_ref[...])


def _merge(ya, yb, yc, gates, h, wa, wb, wc, wo, ln_g, ln_b, tm=512):
    t, d = h.shape
    tm = min(tm, t)
    row = lambda w: pl.BlockSpec((tm, w), lambda i: (i, 0))
    return pl.pallas_call(
        _merge_kernel,
        grid=(t // tm,),
        in_specs=[row(d), row(d), row(d), row(3 * d), row(d)] + [_full((d, d))] * 4 + [_full((1, d))] * 2,
        out_specs=row(d),
        out_shape=jax.ShapeDtypeStruct((t, d), F32),
        compiler_params=_cparams(("parallel",)),
        name="merge_ln",
    )(ya, yb, yc, gates, h, wa.astype(BF16), wb.astype(BF16), wc.astype(BF16), wo.astype(BF16),
      ln_g.reshape(1, d), ln_b.reshape(1, d))


def _router_kernel(h_ref, wt_ref, bias_ref, o_ref):
    logits = lax.dot_general(wt_ref[...], h_ref[...], _NT, precision=lax.Precision.HIGHEST,
                             preferred_element_type=F32)
    mx = jnp.max(logits, axis=0, keepdims=True)
    ex = jnp.exp(logits - mx)
    probs = ex / jnp.sum(ex, axis=0, keepdims=True)
    sel = probs + bias_ref[...]
    rows = [sel[e:e + 1, :] for e in range(N_EXPERTS)]
    prow = [probs[e:e + 1, :] for e in range(N_EXPERTS)]
    gscore = []
    for g in range(N_GROUPS):
        m = rows[g * EXPERTS_PER_GROUP:(g + 1) * EXPERTS_PER_GROUP]
        best = None
        for i in range(EXPERTS_PER_GROUP):
            for j in range(i + 1, EXPERTS_PER_GROUP):
                s = m[i] + m[j]
                best = s if best is None else jnp.maximum(best, s)
        gscore.append(best)
    chosen = []
    for g in range(N_GROUPS):
        ok = None
        for g2 in range(N_GROUPS):
            if g2 == g:
                continue
            t = (gscore[g] > gscore[g2]) if g2 < g else (gscore[g] >= gscore[g2])
            ok = t if ok is None else (ok & t)
        chosen.append(ok)
    picked = []
    for e in range(N_EXPERTS):
        g = e // EXPERTS_PER_GROUP
        rank = None
        for e2 in range(g * EXPERTS_PER_GROUP, (g + 1) * EXPERTS_PER_GROUP):
            if e2 == e:
                continue
            ahead = (rows[e2] >= rows[e]) if e2 < e else (rows[e2] > rows[e])
            ahead = ahead.astype(F32)
            rank = ahead if rank is None else rank + ahead
        picked.append(jnp.where(chosen[g] & (rank < 2.0), prow[e], 0.0))
    tot = picked[0]
    for e in range(1, N_EXPERTS):
        tot = tot + picked[e]
    inv = 1.0 / tot
    for e in range(N_EXPERTS):
        o_ref[e:e + 1, :] = picked[e] * inv


def _router(h, router_w, router_bias, tm=512):
    t, d = h.shape
    tm = min(tm, t)
    return pl.pallas_call(
        _router_kernel,
        grid=(t // tm,),
        in_specs=[pl.BlockSpec((tm, d), lambda i: (i, 0)), _full((N_EXPERTS, d)), _full((N_EXPERTS, 1))],
        out_specs=pl.BlockSpec((N_EXPERTS, tm), lambda i: (0, i)),
        out_shape=jax.ShapeDtypeStruct((N_EXPERTS, t), F32),
        compiler_params=_cparams(("parallel",)),
        name="router",
    )(h, router_w.T, router_bias.reshape(N_EXPERTS, 1).astype(F32))


MOE_TILE = 256
MOE_ISSUE_UNROLL = 8
_PAIRS = [(a, b) for a in range(EXPERTS_PER_GROUP) for b in range(a + 1, EXPERTS_PER_GROUP)]
N_CLASSES = N_GROUPS * len(_PAIRS)


def _moe_num_tiles(t):
    return -(-(t + N_CLASSES * (MOE_TILE - 1)) // MOE_TILE)


def _route_meta(gates_t):
    e_n, t = gates_t.shape
    i32 = jnp.int32
    mask = gates_t > 0
    eidx = lax.broadcasted_iota(i32, (e_n, t), 0)
    e0 = jnp.minimum(jnp.min(jnp.where(mask, eidx, e_n), axis=0), e_n - 1)
    e1 = jnp.maximum(jnp.max(jnp.where(mask, eidx, -1), axis=0), e0)
    grp = e0 // EXPERTS_PER_GROUP
    a = e0 % EXPERTS_PER_GROUP
    b = jnp.where(e1 // EXPERTS_PER_GROUP == grp, e1 % EXPERTS_PER_GROUP, a)
    b = jnp.where(b == a, (a + 1) % EXPERTS_PER_GROUP, b)
    lo, hi = jnp.minimum(a, b), jnp.maximum(a, b)
    n_pairs = len(_PAIRS)
    cls = grp * n_pairs + (lo * (2 * EXPERTS_PER_GROUP - 1 - lo)) // 2 + (hi - lo - 1)
    g_lo = jnp.take_along_axis(gates_t, (grp * EXPERTS_PER_GROUP + lo)[None, :], axis=0)[0]
    g_hi = jnp.take_along_axis(gates_t, (grp * EXPERTS_PER_GROUP + hi)[None, :], axis=0)[0]
    onehot = (cls[None, :] == jnp.arange(N_CLASSES, dtype=i32)[:, None]).astype(i32)
    csum = jnp.cumsum(onehot, axis=1)
    cnt = csum[:, -1]
    rank = jnp.sum(onehot * (csum - 1), axis=0)
    ptiles = (cnt + MOE_TILE - 1) // MOE_TILE
    tile_end = jnp.cumsum(ptiles)
    tile_off = tile_end - ptiles
    pos = jnp.take(tile_off, cls) * MOE_TILE + rank
    n_tiles = _moe_num_tiles(t)
    as_i32 = lambda g: lax.bitcast_convert_type(g.astype(F32), i32)
    per_token = jnp.stack([jnp.arange(t, dtype=i32), as_i32(g_lo), as_i32(g_hi)], axis=1)
    per_row = jnp.zeros((n_tiles * MOE_TILE, 3), i32).at[pos].set(per_token, unique_indices=True)
    row_token = per_row[:, 0]
    row_gates = lax.bitcast_convert_type(per_row[:, 1:3], F32)
    tid = jnp.arange(n_tiles, dtype=i32)
    tcls = jnp.minimum(jnp.sum((tid[:, None] >= tile_end[None, :]).astype(i32), axis=1), N_CLASSES - 1)
    n_valid = jnp.clip(jnp.take(cnt, tcls) - (tid - jnp.take(tile_off, tcls)) * MOE_TILE, 0, MOE_TILE)
    n_valid = jnp.where(tid < tile_end[-1], n_valid, 0).astype(i32)
    cls_lo = jnp.array([g * EXPERTS_PER_GROUP + p[0] for g in range(N_GROUPS) for p in _PAIRS], i32)
    cls_hi = jnp.array([g * EXPERTS_PER_GROUP + p[1] for g in range(N_GROUPS) for p in _PAIRS], i32)
    return row_token, jnp.take(cls_lo, tcls), jnp.take(cls_hi, tcls), n_valid, row_gates


def _moe_kernel(rt_ref, ea_ref, eb_ref, nv_ref, h_hbm, rg_ref, wga_ref, wua_ref, wda_ref, wgb_ref, wub_ref,
                wdb_ref, lg_ref, lb_ref, out_hbm, xbuf, obuf, sem_in, sem_out):
    i = pl.program_id(0)
    n = pl.num_programs(0)
    slot = lax.rem(i, 2)

    def rows_in(row, s, k, rows=1):
        return pltpu.make_async_copy(h_hbm.at[pl.ds(row, rows)], xbuf.at[s, pl.ds(k, rows)], sem_in.at[s])

    def rows_out(row, s, k, rows=1):
        return pltpu.make_async_copy(obuf.at[s, pl.ds(k, rows)], out_hbm.at[pl.ds(row, rows)], sem_out.at[s])

    def issue(copy_at, tile, count):
        base = tile * MOE_TILE

        def group(q, c):
            for r in range(MOE_ISSUE_UNROLL):
                copy_at(base, q * MOE_ISSUE_UNROLL + r).start()
            return c

        def single(k, c):
            copy_at(base, k).start()
            return c
        full = lax.shift_right_logical(count, int(math.log2(MOE_ISSUE_UNROLL)))
        lax.fori_loop(0, full, group, 0)
        lax.fori_loop(full * MOE_ISSUE_UNROLL, count, single, 0)

    def wait_rows(copies, count):
        rows = MOE_TILE
        while rows >= 1:
            @pl.when((count & rows) != 0)
            def _(rows=rows):
                copies(0, 0, rows).wait()
            rows //= 2

    def start_gather(tile, s):
        issue(lambda base, k: rows_in(rt_ref[base + k], s, k), tile, nv_ref[tile])

    @pl.when(i == 0)
    def _():
        xbuf[...] = jnp.zeros(xbuf.shape, F32)
        start_gather(0, 0)

    @pl.when(i + 1 < n)
    def _():
        start_gather(i + 1, 1 - slot)

    wait_rows(lambda row, k, rows: rows_in(row, slot, k, rows), nv_ref[i])

    @pl.when(i >= 2)
    def _():
        wait_rows(lambda row, k, rows: rows_out(row, slot, k, rows), nv_ref[i - 2])

    @pl.when(nv_ref[i] > 0)
    def _():
        x = xbuf[slot]
        xb = x.astype(BF16)
        rg = rg_ref[...]
        act_a = (_silu(_dot(xb, wga_ref[0])) * _dot(xb, wua_ref[0])).astype(BF16)
        act_b = (_silu(_dot(xb, wgb_ref[0])) * _dot(xb, wub_ref[0])).astype(BF16)
        moe = rg[:, 0:1] * _dot(act_a, wda_ref[0]) + rg[:, 1:2] * _dot(act_b, wdb_ref[0])
        obuf[slot] = _layernorm_rows(DN_ALPHA * x + moe, lg_ref[...], lb_ref[...])
        issue(lambda base, k: rows_out(rt_ref[base + k], slot, k), i, nv_ref[i])

    @pl.when(i == n - 1)
    def _():
        @pl.when(i >= 1)
        def _():
            wait_rows(lambda row, k, rows: rows_out(row, 1 - slot, k, rows), nv_ref[i - 1])
        wait_rows(lambda row, k, rows: rows_out(row, slot, k, rows), nv_ref[i])


def _moe(h, gates_t, wg, wu, wd, ln_g, ln_b):
    t, d = h.shape
    n_tiles = _moe_num_tiles(t)
    row_token, exp_a, exp_b, n_valid, row_gates = _route_meta(gates_t)
    w_up = lambda which: pl.BlockSpec((1, d, D_EXPERT), lambda i, rt, ea, eb, nv: ((ea, eb)[which][i], 0, 0))
    w_dn = lambda which: pl.BlockSpec((1, D_EXPERT, d), lambda i, rt, ea, eb, nv: ((ea, eb)[which][i], 0, 0))
    const = lambda shape: pl.BlockSpec(shape, lambda i, *_: (0,) * len(shape))
    grid_spec = pltpu.PrefetchScalarGridSpec(
        num_scalar_prefetch=4,
        grid=(n_tiles,),
        in_specs=[pl.BlockSpec(memory_space=pl.ANY),
                  pl.BlockSpec((MOE_TILE, 2), lambda i, *_: (i, 0)),
                  w_up(0), w_up(0), w_dn(0), w_up(1), w_up(1), w_dn(1), const((1, d)), const((1, d))],
        out_specs=pl.BlockSpec(memory_space=pl.ANY),
        scratch_shapes=[pltpu.VMEM((2, MOE_TILE, d), F32), pltpu.VMEM((2, MOE_TILE, d), F32),
                        pltpu.SemaphoreType.DMA((2,)), pltpu.SemaphoreType.DMA((2,))],
    )
    return pl.pallas_call(
        _moe_kernel,
        grid_spec=grid_spec,
        out_shape=jax.ShapeDtypeStruct((t, d), F32),
        compiler_params=_cparams(("arbitrary",)),
        name="experts_ln",
    )(row_token, exp_a, exp_b, n_valid, h, row_gates, wg, wu, wd, wg, wu, wd, ln_g.reshape(1, d), ln_b.reshape(1, d))


def _in_weights(w_in_l):
    offs = [0]
    for s in IN_SIZES:
        offs.append(offs[-1] + s)
    seg = [w_in_l[:, offs[i]:offs[i + 1]] for i in range(len(IN_SIZES))]
    seg[2] = jnp.pad(seg[2], ((0, 0), (0, LANES - SSD_HEADS)))
    return [s.astype(BF16) for s in seg]


def kernel(x, ln_in_g, ln_in_b, w_in, ssd_conv_w, ssd_conv_b, ssd_dt_bias, ssd_a_log, ssd_d, ssd_norm_w, rwkv_mu, rwkv_w0, rwkv_w2, rwkv_a0, rwkv_a2, rwkv_g2, rwkv_k_k, rwkv_k_a, rwkv_r_k, rwkv_ln_w, rwkv_ln_b, hgrn_lb, hgrn_norm_w, w_br_ssd, w_br_rwkv, w_br_hgrn, w_out, ln1_g, ln1_b, router_w, router_bias, exp_w_gate, exp_w_up, exp_w_down, ln2_g, ln2_b):
    batch, seq, d = x.shape
    t = batch * seq
    h, h_bf = _layernorm(x.reshape(t, d), ln_in_g, ln_in_b)
    lsm = jax.nn.softmax(hgrn_lb.astype(F32), axis=0)
    lower_bounds = jnp.cumsum(lsm, axis=0) - lsm[0]
    for l in range(DEPTH):
        names = ("z", "xbc", "dt", "rwkv", "hgrn", "gates")
        z, xbc, dt_raw, f_rwkv, f_hgrn, gates = [
            _project(h_bf, w, f"proj_{n}", F32 if n == "dt" else BF16)
            for w, n in zip(_in_weights(w_in[l]), names)]
        y_a = _ssd(z, xbc, dt_raw, ssd_conv_w[l], ssd_conv_b[l], ssd_dt_bias[l], ssd_a_log[l], ssd_d[l],
                   ssd_norm_w[l], batch)
        y_b = _rwkv(f_rwkv, rwkv_mu[l], rwkv_w0[l], rwkv_w2[l], rwkv_a0[l], rwkv_a2[l], rwkv_g2[l],
                    rwkv_k_k[l], rwkv_k_a[l], rwkv_r_k[l].reshape(-1), rwkv_ln_w[l], rwkv_ln_b[l], batch)
        y_c = _hgrn(f_hgrn, lower_bounds[l], hgrn_norm_w[l], batch)
        h = _merge(y_a, y_b, y_c, gates, h, w_br_ssd[l], w_br_rwkv[l], w_br_hgrn[l], w_out[l],
                   ln1_g[l], ln1_b[l])
        gates_t = _router(h, router_w, router_bias)
        h = _moe(h, gates_t, exp_w_gate[l].astype(BF16), exp_w_up[l].astype(BF16),
                 exp_w_down[l].astype(BF16), ln2_g[l], ln2_b[l])
        h_bf = h.astype(BF16)
    return h.reshape(batch, seq, d)
```

```python
import functools
import math

import jax
import jax.numpy as jnp
from jax import lax
from jax.experimental import pallas as pl
from jax.experimental.pallas import tpu as pltpu

F32 = jnp.float32
BF16 = jnp.bfloat16

D_MODEL = 1024
DEPTH = 2
CHUNK = 64
STEP_CHUNKS = 4
STEP_ROWS = STEP_CHUNKS * CHUNK
LANES = 128

SSD_HEADS = 16
SSD_HEADDIM = 64
SSD_GROUPS = 2
SSD_STATE = 128
SSD_CONV = 4
SSD_XBC = D_MODEL + 2 * SSD_GROUPS * SSD_STATE

RWKV_HEADS = 16
RWKV_HEAD = 64
RWKV_COLS = 3 * D_MODEL + 64 + 64 + 128
RWKV_GN_EPS = 64e-5

HGRN_HEADS = 8
HGRN_EXPAND = 128

IN_SIZES = (D_MODEL, SSD_XBC, SSD_HEADS, RWKV_COLS, 4 * D_MODEL, 3 * D_MODEL)

N_EXPERTS = 16
N_GROUPS = 4
EXPERTS_PER_GROUP = 4
D_EXPERT = 512

DN_ALPHA = (2 * DEPTH) ** 0.25
LN_EPS = 1e-5
RMS_EPS = 1e-6

VMEM_LIMIT = 56 * 1024 * 1024

_NN = (((1,), (0,)), ((), ()))
_NT = (((1,), (1,)), ((), ()))
_TN = (((0,), (0,)), ((), ()))


def _dot(a, b, dims=_NN):
    return lax.dot_general(a, b, dims, preferred_element_type=F32)


def _bdot(a, b, dims=_NN):
    return _dot(a.astype(BF16), b.astype(BF16), dims)


def _split3(x):
    x1 = x.astype(BF16)
    r1 = x - x1.astype(F32)
    x2 = r1.astype(BF16)
    x3 = (r1 - x2.astype(F32)).astype(BF16)
    return x1, x2, x3


def _sel_r(x, sel, pieces=3):
    n = x.shape[0]
    if pieces == 1:
        return _bdot(x, sel)
    y = _dot(jnp.concatenate(_split3(x)[:pieces], axis=0), sel)
    out = y[:n] + y[n:2 * n]
    return out + y[2 * n:] if pieces == 3 else out


def _chunk_cumsum(x):
    row = lax.broadcasted_iota(jnp.int32, x.shape, 0) & (CHUNK - 1)
    shift = 1
    while shift < CHUNK:
        x = x + jnp.where(row >= shift, pltpu.roll(x, shift, 0), 0.0)
        shift *= 2
    return x


def _sigmoid(x):
    return 1.0 / (1.0 + jnp.exp(-x))


def _silu(x):
    return x * _sigmoid(x)


def _softplus(x):
    return jnp.maximum(x, 0.0) + jnp.log1p(jnp.exp(-jnp.abs(x)))


def _layernorm_rows(x, g, b):
    mu = jnp.mean(x, axis=-1, keepdims=True)
    xc = x - mu
    var = jnp.mean(xc * xc, axis=-1, keepdims=True)
    return xc * lax.rsqrt(var + LN_EPS) * g + b


def _tri(n, strict=False):
    r = lax.broadcasted_iota(jnp.int32, (n, n), 0)
    c = lax.broadcasted_iota(jnp.int32, (n, n), 1)
    return (c < r) if strict else (c <= r)


def _cparams(sem):
    return pltpu.CompilerParams(dimension_semantics=sem, vmem_limit_bytes=VMEM_LIMIT)


def _full(shape):
    return pl.BlockSpec(shape, lambda *_: (0,) * len(shape))


def _ln_kernel(x_ref, g_ref, b_ref, o_ref, ob_ref):
    y = _layernorm_rows(x_ref[...], g_ref[...], b_ref[...])
    o_ref[...] = y
    ob_ref[...] = y.astype(BF16)


def _layernorm(x, g, b, tm=512):
    t, d = x.shape
    row = pl.BlockSpec((tm, d), lambda i: (i, 0))
    return pl.pallas_call(
        _ln_kernel,
        grid=(t // tm,),
        in_specs=[row, _full((1, d)), _full((1, d))],
        out_specs=[row, row],
        out_shape=[jax.ShapeDtypeStruct((t, d), F32), jax.ShapeDtypeStruct((t, d), BF16)],
        compiler_params=_cparams(("parallel",)),
        name="layernorm_in",
    )(x, g.reshape(1, d), b.reshape(1, d))


def _proj_kernel(x_ref, w_ref, o_ref):
    o_ref[...] = _dot(x_ref[...], w_ref[...]).astype(o_ref.dtype)


def _project(x_bf, w_bf, name, out_dtype=BF16, tm=1024):
    t, k = x_bf.shape
    n = w_bf.shape[1]
    tm = min(tm, t)
    tn = n
    for cand in range(min(n, 2048), 0, -LANES):
        if n % cand == 0:
            tn = cand
            break
    return pl.pallas_call(
        _proj_kernel,
        grid=(n // tn, t // tm),
        in_specs=[pl.BlockSpec((tm, k), lambda j, i: (i, 0)), pl.BlockSpec((k, tn), lambda j, i: (0, j))],
        out_specs=pl.BlockSpec((tm, tn), lambda j, i: (i, j)),
        out_shape=jax.ShapeDtypeStruct((t, n), out_dtype),
        compiler_params=_cparams(("parallel", "parallel")),
        name=name,
    )(x_bf, w_bf)


def _head_expand(n_heads, width):
    h = lax.broadcasted_iota(jnp.int32, (LANES, n_heads * width), 0)
    c = lax.broadcasted_iota(jnp.int32, (LANES, n_heads * width), 1)
    return (c // width == h).astype(BF16)


def _ssd_kernel(z_ref, xbc_ref, dt_ref, cw_ref, cb_ref, dtb_ref, a_ref, dsk_ref, nw_ref, e_ref,
                o_ref, pad_ref, st_ref):
    c = pl.program_id(1)
    n_pairs = SSD_HEADS // 2

    @pl.when(c == 0)
    def _():
        pad_ref[0:8, :] = jnp.zeros((8, SSD_XBC), F32)
        st_ref[...] = jnp.zeros(st_ref.shape, F32)

    pad_ref[8:8 + STEP_ROWS, :] = xbc_ref[...].astype(F32)
    acc = jnp.broadcast_to(cb_ref[...], (STEP_ROWS, SSD_XBC))
    for j in range(SSD_CONV):
        acc = acc + cw_ref[j:j + 1, :] * pad_ref[pl.ds(8 - (SSD_CONV - 1) + j, STEP_ROWS), :]
    pad_ref[0:8, :] = pad_ref[STEP_ROWS:STEP_ROWS + 8, :]
    xbc = _silu(acc)
    xs = xbc[:, :D_MODEL]

    e_mat = e_ref[...]
    dt = _softplus(dt_ref[...] + dtb_ref[...])
    acs = _chunk_cumsum(dt * a_ref[...])
    acs_all = _sel_r(acs, e_mat)
    xs_dt_all = xs * _sel_r(dt, e_mat)

    li = lax.broadcasted_iota(jnp.int32, (CHUNK, D_MODEL), 0)
    si = lax.broadcasted_iota(jnp.int32, (CHUNK, D_MODEL), 1) & (SSD_HEADDIM - 1)
    lane = lax.broadcasted_iota(jnp.int32, (CHUNK, LANES), 1)
    m0 = lane < SSD_HEADDIM

    pairs = range(n_pairs)
    sls = [slice(j * LANES, (j + 1) * LANES) for j in pairs]
    grp = [j // (n_pairs // SSD_GROUPS) for j in pairs]
    st = [st_ref[j] for j in pairs]
    y_rows = []
    for sub in range(STEP_CHUNKS):
        rs = slice(sub * CHUNK, (sub + 1) * CHUNK)
        acs_e, xs_dt = acs_all[rs], xs_dt_all[rs]
        acs_row = jnp.sum(jnp.where(li == si, acs_e, 0.0), axis=0, keepdims=True)
        decay = jnp.exp(jnp.where(si <= li, acs_e - acs_row, -jnp.inf))
        exp_acs = jnp.exp(acs_e)
        acs_last = acs_e[CHUNK - 1:CHUNK, :]
        to_end = jnp.exp(acs_last - acs_e)
        exp_last = jnp.exp(acs_last)
        bm = [xbc[rs, D_MODEL + g * SSD_STATE:D_MODEL + (g + 1) * SSD_STATE].astype(BF16)
              for g in range(SSD_GROUPS)]
        cm = [xbc[rs, D_MODEL + (SSD_GROUPS + g) * SSD_STATE:D_MODEL + (SSD_GROUPS + g + 1) * SSD_STATE
                  ].astype(BF16) for g in range(SSD_GROUPS)]
        cb2 = [_dot(cm[g], jnp.concatenate([bm[g], bm[g]], axis=0), _NT) for g in range(SSD_GROUPS)]
        xp = [xs_dt[:, sl] for sl in sls]
        xbd = [jnp.concatenate([jnp.where(m0, x, 0.0), jnp.where(m0, 0.0, x)], axis=0) for x in xp]
        y_parts = [_bdot(cb2[grp[j]] * decay[:, sls[j]], xbd[j]) + _bdot(cm[grp[j]], st[j]) * exp_acs[:, sls[j]]
                   for j in pairs]
        upd = [_bdot(bm[grp[j]], xp[j] * to_end[:, sls[j]], _TN) for j in pairs]
        st = [st[j] * exp_last[:, sls[j]] + upd[j] for j in pairs]
        y_rows.append(jnp.concatenate(y_parts, axis=1))
    for j in pairs:
        st_ref[j] = st[j]
    y = jnp.concatenate(y_rows, axis=0) + xs * dsk_ref[...]
    y = y * _silu(z_ref[...].astype(F32))
    gw = D_MODEL // SSD_GROUPS
    for g in range(SSD_GROUPS):
        yg = y[:, g * gw:(g + 1) * gw]
        ms = jnp.mean(yg * yg, axis=-1, keepdims=True)
        o_ref[:, g * gw:(g + 1) * gw] = (yg * lax.rsqrt(ms + RMS_EPS) * nw_ref[:, g * gw:(g + 1) * gw]
                                         ).astype(o_ref.dtype)


def _ssd(z, xbc, dt_raw, conv_w, conv_b, dt_bias, a_log, d_skip, norm_w, batch):
    t = z.shape[0]
    nc = t // batch // STEP_ROWS
    pad16 = lambda v: jnp.pad(v.astype(F32), (0, LANES - SSD_HEADS)).reshape(1, LANES)
    row = lambda w: pl.BlockSpec((STEP_ROWS, w), lambda b, c: (b * nc + c, 0))
    return pl.pallas_call(
        _ssd_kernel,
        grid=(batch, nc),
        in_specs=[row(D_MODEL), row(SSD_XBC), row(LANES), _full((SSD_CONV, SSD_XBC)), _full((1, SSD_XBC)),
                  _full((1, LANES)), _full((1, LANES)), _full((1, D_MODEL)), _full((1, D_MODEL)),
                  _full((LANES, D_MODEL))],
        out_specs=row(D_MODEL),
        out_shape=jax.ShapeDtypeStruct((t, D_MODEL), BF16),
        scratch_shapes=[pltpu.VMEM((STEP_ROWS + 8, SSD_XBC), F32),
                        pltpu.VMEM((SSD_HEADS // 2, SSD_STATE, LANES), F32)],
        compiler_params=_cparams(("parallel", "arbitrary")),
        name="ssd",
    )(z, xbc, dt_raw, conv_w, conv_b.reshape(1, -1), pad16(dt_bias), pad16(-jnp.exp(a_log.astype(F32))),
      jnp.repeat(d_skip.astype(F32), SSD_HEADDIM).reshape(1, -1), norm_w.reshape(1, -1),
      _head_expand(SSD_HEADS, SSD_HEADDIM))


def _hgrn_kernel(f_ref, lb_ref, nw_ref, o_ref, st_ref):
    c = pl.program_id(1)

    @pl.when(c == 0)
    def _():
        st_ref[...] = jnp.zeros(st_ref.shape, F32)

    d = D_MODEL
    lb = lb_ref[...]
    q = _silu(f_ref[:, 0:d].astype(F32))
    forget = lb + (1.0 - lb) * _sigmoid(f_ref[:, d:2 * d].astype(F32))
    k = 1.0 - forget
    bc_all = _chunk_cumsum(jnp.log(forget))
    qd_all = q * jnp.exp(bc_all)
    causal = _tri(CHUNK)
    heads = range(HGRN_HEADS)
    sls = [slice(h * HGRN_EXPAND, (h + 1) * HGRN_EXPAND) for h in heads]
    st = [st_ref[h] for h in heads]
    for sub in range(STEP_CHUNKS):
        rs = slice(sub * CHUNK, (sub + 1) * CHUNK)
        bc, qs, ks, qd = bc_all[rs], q[rs], k[rs], qd_all[rs]
        mid = bc[CHUNK // 2:CHUNK // 2 + 1, :]
        last = bc[CHUNK - 1:CHUNK, :]
        qe = qs * jnp.exp(bc - mid)
        ke = ks * jnp.exp(mid - bc)
        kd = ks * jnp.exp(last - bc)
        w_last = jnp.exp(last)
        vb = [f_ref[rs, 2 * d + h * HGRN_EXPAND:2 * d + (h + 1) * HGRN_EXPAND].astype(BF16) for h in heads]
        att = [jnp.where(causal, _bdot(qe[:, sl], ke[:, sl], _NT), 0.0) for sl in sls]
        o = [_bdot(att[h], vb[h]) + _bdot(qd[:, sls[h]], st[h], _NT) for h in heads]
        upd = [_bdot(vb[h], kd[:, sls[h]], _TN) for h in heads]
        st = [st[h] * w_last[:, sls[h]] + upd[h] for h in heads]
        for h in heads:
            oh = o[h] * lax.rsqrt(jnp.mean(o[h] * o[h], axis=-1, keepdims=True) + RMS_EPS) * nw_ref[...]
            gate = _sigmoid(f_ref[rs, 3 * d + h * HGRN_EXPAND:3 * d + (h + 1) * HGRN_EXPAND].astype(F32))
            o_ref[rs, sls[h]] = (oh * gate).astype(o_ref.dtype)
    for h in heads:
        st_ref[h] = st[h]


def _hgrn(feat, lb, norm_w, batch):
    t = feat.shape[0]
    nc = t // batch // STEP_ROWS
    row = lambda w: pl.BlockSpec((STEP_ROWS, w), lambda b, c: (b * nc + c, 0))
    return pl.pallas_call(
        _hgrn_kernel,
        grid=(batch, nc),
        in_specs=[row(4 * D_MODEL), _full((1, D_MODEL)), _full((1, HGRN_EXPAND))],
        out_specs=row(D_MODEL),
        out_shape=jax.ShapeDtypeStruct((t, D_MODEL), BF16),
        scratch_shapes=[pltpu.VMEM((HGRN_HEADS, HGRN_EXPAND, HGRN_EXPAND), F32)],
        compiler_params=_cparams(("parallel", "arbitrary")),
        name="hgrn2",
    )(feat, lb.reshape(1, -1), norm_w.reshape(1, -1))


def _rwkv_kernel(f_ref, mu_ref, w0_ref, w2_ref, a0_ref, a2_ref, g2_ref, kk_ref, ka_ref, rk_ref, lnw_ref,
                 lnb_ref, e_ref, et_ref, o_ref, pad_ref, st_ref, y_ref):
    c = pl.program_id(1)
    d = D_MODEL
    n_pairs = RWKV_HEADS // 2

    @pl.when(c == 0)
    def _():
        pad_ref[0:8, :] = jnp.zeros((8, RWKV_COLS), F32)
        st_ref[...] = jnp.zeros(st_ref.shape, F32)

    cur = f_ref[...].astype(F32)
    pad_ref[8:8 + STEP_ROWS, :] = cur
    prev = pad_ref[pl.ds(7, STEP_ROWS), :]
    pad_ref[0:8, :] = pad_ref[STEP_ROWS:STEP_ROWS + 8, :]
    x = cur + (prev - cur) * mu_ref[...]
    r, k, v = x[:, 0:d], x[:, d:2 * d], x[:, 2 * d:3 * d]
    lora_in = x[:, 3 * d:3 * d + LANES]
    xg = x[:, 3 * d + LANES:3 * d + 2 * LANES]

    e_mat, et_mat = e_ref[...], et_ref[...]
    w = -_softplus(-(w0_ref[...] + _bdot(jnp.tanh(lora_in), w2_ref[...]))) - 0.5
    ld = -jnp.exp(w)
    a = _sigmoid(a0_ref[...] + _bdot(lora_in, a2_ref[...]))
    gate = _bdot(_sigmoid(xg), g2_ref[...])
    kk = k * kk_ref[...]
    nrm = jnp.maximum(jnp.sqrt(_sel_r(kk * kk, et_mat, 1)), 1e-12)
    kk = kk * _sel_r(1.0 / nrm, e_mat, 2)
    k2 = k * (1.0 + (a - 1.0) * ka_ref[...])
    cw = _chunk_cumsum(ld)
    e_neg = jnp.exp(-cw)
    a_t = -kk * jnp.exp(cw - ld)
    r_t = r * jnp.exp(cw)
    b_t = kk * a * e_neg
    k_t = k2 * e_neg

    lane = lax.broadcasted_iota(jnp.int32, (CHUNK, LANES), 1)
    m0 = lane < RWKV_HEAD
    row2 = lax.broadcasted_iota(jnp.int32, (LANES, LANES), 0)
    col2 = lax.broadcasted_iota(jnp.int32, (LANES, LANES), 1)
    same = (row2 < CHUNK) == (col2 < RWKV_HEAD)
    row4 = lax.broadcasted_iota(jnp.int32, (2 * LANES, LANES), 0)
    col4 = lax.broadcasted_iota(jnp.int32, (2 * LANES, LANES), 1)
    top4 = (row4 & (LANES - 1)) < CHUNK
    tt, ss = row4 & (CHUNK - 1), col4 & (CHUNK - 1)
    bd4 = (top4 == (col4 < RWKV_HEAD)) & ((ss < tt) | ((row4 >= LANES) & (ss == tt)))

    def stack(p):
        return jnp.concatenate([jnp.where(m0, p, 0.0), jnp.where(m0, 0.0, p)], axis=0)

    pairs = range(n_pairs)
    sls = [slice(j * LANES, (j + 1) * LANES) for j in pairs]
    bf = lambda xs: [x.astype(BF16) for x in xs]
    mt = [st_ref[j] for j in pairs]
    for sub in range(STEP_CHUNKS):
        rs = slice(sub * CHUNK, (sub + 1) * CHUNK)
        w_last = jnp.exp(cw[(sub + 1) * CHUNK - 1:(sub + 1) * CHUNK, :])
        s12 = bf(jnp.concatenate([stack(a_t[rs, sl]), stack(r_t[rs, sl])], axis=0) for sl in sls)
        vst = bf(stack(v[rs, sl]) for sl in sls)
        qmat = bf(jnp.concatenate([b_t[rs, sl], k_t[rs, sl]], axis=0) for sl in sls)
        mtb = bf(mt)
        gh = [_dot(s12[j], qmat[j], _NT) for j in pairs]
        ghr = [pltpu.roll(x, CHUNK, 1) for x in gh]
        n_ab, a_rb, akrk = [], [], []
        for j in pairs:
            diag = jnp.where(bd4, jnp.where(top4, gh[j], ghr[j]), 0.0)
            n_ab.append(diag[:LANES])
            a_rb.append(diag[LANES:].astype(BF16))
            akrk.append(jnp.where(bd4, jnp.where(top4, ghr[j], gh[j]), 0.0).astype(BF16))
        sm = [_dot(s12[j], mtb[j], _NT) for j in pairs]
        av = [_dot(akrk[j], vst[j]) for j in pairs]
        xm = [sm[j][:LANES] + av[j][:LANES] for j in pairs]
        tp = n_ab
        pwf = [_bdot(x, x) for x in n_ab]
        n_steps = int(math.log2(CHUNK)) - 1
        for step in range(n_steps):
            pw = bf(pwf)
            if step + 1 < n_steps:
                both = [_dot(jnp.concatenate([pw[j], tp[j].astype(BF16)], axis=0), pw[j]) for j in pairs]
                tp = [tp[j] + pwf[j] + both[j][LANES:] for j in pairs]
                pwf = [both[j][:LANES] for j in pairs]
            else:
                tp = [tp[j] + pwf[j] + _bdot(tp[j], pw[j]) for j in pairs]
        u = [xm[j] + _bdot(tp[j], xm[j]) for j in pairs]
        yst = [sm[j][LANES:] + av[j][LANES:] + _bdot(a_rb[j], u[j]) for j in pairs]
        for j in pairs:
            y_ref[rs, sls[j]] = yst[j][:CHUNK] + yst[j][CHUNK:]
        uv = [jnp.concatenate([u[j][:CHUNK] + u[j][CHUNK:], v[rs, sls[j]]], axis=0) for j in pairs]
        upd = [jnp.where(same, _bdot(uv[j], qmat[j], _TN), 0.0) for j in pairs]
        mt = [(mt[j] + upd[j]) * w_last[:, sls[j]] for j in pairs]
    for j in pairs:
        st_ref[j] = mt[j]

    y = y_ref[...]
    inv_n = 1.0 / RWKV_HEAD
    yc = y - _sel_r(_sel_r(y, et_mat, 1) * inv_n, e_mat, 2)
    rs = lax.rsqrt(_sel_r(yc * yc, et_mat, 1) * inv_n + RWKV_GN_EPS)
    yn = yc * _sel_r(rs, e_mat, 2) * lnw_ref[...] + lnb_ref[...]
    bonus = _sel_r(_sel_r(r * k2 * rk_ref[...], et_mat, 1), e_mat, 2) * v
    o_ref[...] = ((yn + bonus) * gate).astype(o_ref.dtype)


def _rwkv(feat, mu, w0, w2, a0, a2, g2, k_k, k_a, r_k, ln_w, ln_b, batch):
    t = feat.shape[0]
    nc = t // batch // STEP_ROWS
    d = D_MODEL
    row = lambda w: pl.BlockSpec((STEP_ROWS, w), lambda b, c: (b * nc + c, 0))
    vec = lambda v: v.astype(F32).reshape(1, -1)
    w2p = jnp.concatenate([w2, jnp.zeros_like(w2)], axis=0).astype(BF16)
    a2p = jnp.concatenate([jnp.zeros_like(a2), a2], axis=0).astype(BF16)
    e_mat = _head_expand(RWKV_HEADS, RWKV_HEAD)
    return pl.pallas_call(
        _rwkv_kernel,
        grid=(batch, nc),
        in_specs=[row(RWKV_COLS), _full((1, RWKV_COLS)), _full((1, d)), _full((LANES, d)), _full((1, d)),
                  _full((LANES, d)), _full((LANES, d)), _full((1, d)), _full((1, d)), _full((1, d)),
                  _full((1, d)), _full((1, d)), _full((LANES, d)), _full((d, LANES))],
        out_specs=row(d),
        out_shape=jax.ShapeDtypeStruct((t, d), BF16),
        scratch_shapes=[pltpu.VMEM((STEP_ROWS + 8, RWKV_COLS), F32),
                        pltpu.VMEM((RWKV_HEADS // 2, LANES, LANES), F32),
                        pltpu.VMEM((STEP_ROWS, d), F32)],
        compiler_params=_cparams(("parallel", "arbitrary")),
        name="rwkv7",
    )(feat, vec(mu), vec(w0), w2p, vec(a0), a2p, g2.astype(BF16), vec(k_k), vec(k_a), vec(r_k), vec(ln_w),
      vec(ln_b), e_mat, e_mat.T)


def _merge_kernel(ya_ref, yb_ref, yc_ref, g_ref, h_ref, wa_ref, wb_ref, wc_ref, wo_ref, lg_ref, lb_ref, o_ref):
    d = D_MODEL
    gate = lambda j: _sigmoid(g_ref[:, j * d:(j + 1) * d].astype(F32))
    m = (gate(0) * _dot(ya_ref[...], wa_ref[...]) + gate(1) * _dot(yb_ref[...], wb_ref[...])
         + gate(2) * _dot(yc_ref[...], wc_ref[...]))
    hn = DN_ALPHA * h_ref[...] + _bdot(m, wo_ref[...])
    o_ref[...] = _layernorm_rows(hn, lg_ref[...], lb_ref[...])


def _merge(ya, yb, yc, gates, h, wa, wb, wc, wo, ln_g, ln_b, tm=512):
    t, d = h.shape
    tm = min(tm, t)
    row = lambda w: pl.BlockSpec((tm, w), lambda i: (i, 0))
    return pl.pallas_call(
        _merge_kernel,
        grid=(t // tm,),
        in_specs=[row(d), row(d), row(d), row(3 * d), row(d)] + [_full((d, d))] * 4 + [_full((1, d))] * 2,
        out_specs=row(d),
        out_shape=jax.ShapeDtypeStruct((t, d), F32),
        compiler_params=_cparams(("parallel",)),
        name="merge_ln",
    )(ya, yb, yc, gates, h, wa.astype(BF16), wb.astype(BF16), wc.astype(BF16), wo.astype(BF16),
      ln_g.reshape(1, d), ln_b.reshape(1, d))


def _router_kernel(h_ref, wt_ref, bias_ref, o_ref):
    logits = lax.dot_general(wt_ref[...], h_ref[...], _NT, precision=lax.Precision.HIGHEST,
                             preferred_element_type=F32)
    mx = jnp.max(logits, axis=0, keepdims=True)
    ex = jnp.exp(logits - mx)
    probs = ex / jnp.sum(ex, axis=0, keepdims=True)
    sel = probs + bias_ref[...]
    rows = [sel[e:e + 1, :] for e in range(N_EXPERTS)]
    prow = [probs[e:e + 1, :] for e in range(N_EXPERTS)]
    gscore = []
    for g in range(N_GROUPS):
        m = rows[g * EXPERTS_PER_GROUP:(g + 1) * EXPERTS_PER_GROUP]
        best = None
        for i in range(EXPERTS_PER_GROUP):
            for j in range(i + 1, EXPERTS_PER_GROUP):
                s = m[i] + m[j]
                best = s if best is None else jnp.maximum(best, s)
        gscore.append(best)
    chosen = []
    for g in range(N_GROUPS):
        ok = None
        for g2 in range(N_GROUPS):
            if g2 == g:
                continue
            t = (gscore[g] > gscore[g2]) if g2 < g else (gscore[g] >= gscore[g2])
            ok = t if ok is None else (ok & t)
        chosen.append(ok)
    picked = []
    for e in range(N_EXPERTS):
        g = e // EXPERTS_PER_GROUP
        rank = None
        for e2 in range(g * EXPERTS_PER_GROUP, (g + 1) * EXPERTS_PER_GROUP):
            if e2 == e:
                continue
            ahead = (rows[e2] >= rows[e]) if e2 < e else (rows[e2] > rows[e])
            ahead = ahead.astype(F32)
            rank = ahead if rank is None else rank + ahead
        picked.append(jnp.where(chosen[g] & (rank < 2.0), prow[e], 0.0))
    tot = picked[0]
    for e in range(1, N_EXPERTS):
        tot = tot + picked[e]
    inv = 1.0 / tot
    for e in range(N_EXPERTS):
        o_ref[e:e + 1, :] = picked[e] * inv


def _router(h, router_w, router_bias, tm=512):
    t, d = h.shape
    tm = min(tm, t)
    return pl.pallas_call(
        _router_kernel,
        grid=(t // tm,),
        in_specs=[pl.BlockSpec((tm, d), lambda i: (i, 0)), _full((N_EXPERTS, d)), _full((N_EXPERTS, 1))],
        out_specs=pl.BlockSpec((N_EXPERTS, tm), lambda i: (0, i)),
        out_shape=jax.ShapeDtypeStruct((N_EXPERTS, t), F32),
        compiler_params=_cparams(("parallel",)),
        name="router",
    )(h, router_w.T, router_bias.reshape(N_EXPERTS, 1).astype(F32))


MOE_TILE = 256
MOE_ISSUE_UNROLL = 8
_PAIRS = [(a, b) for a in range(EXPERTS_PER_GROUP) for b in range(a + 1, EXPERTS_PER_GROUP)]
N_CLASSES = N_GROUPS * len(_PAIRS)


def _moe_num_tiles(t):
    return -(-(t + N_CLASSES * (MOE_TILE - 1)) // MOE_TILE)


def _route_meta(gates_t):
    e_n, t = gates_t.shape
    i32 = jnp.int32
    mask = gates_t > 0
    eidx = lax.broadcasted_iota(i32, (e_n, t), 0)
    e0 = jnp.minimum(jnp.min(jnp.where(mask, eidx, e_n), axis=0), e_n - 1)
    e1 = jnp.maximum(jnp.max(jnp.where(mask, eidx, -1), axis=0), e0)
    grp = e0 // EXPERTS_PER_GROUP
    a = e0 % EXPERTS_PER_GROUP
    b = jnp.where(e1 // EXPERTS_PER_GROUP == grp, e1 % EXPERTS_PER_GROUP, a)
    b = jnp.where(b == a, (a + 1) % EXPERTS_PER_GROUP, b)
    lo, hi = jnp.minimum(a, b), jnp.maximum(a, b)
    n_pairs = len(_PAIRS)
    cls = grp * n_pairs + (lo * (2 * EXPERTS_PER_GROUP - 1 - lo)) // 2 + (hi - lo - 1)
    g_lo = jnp.take_along_axis(gates_t, (grp * EXPERTS_PER_GROUP + lo)[None, :], axis=0)[0]
    g_hi = jnp.take_along_axis(gates_t, (grp * EXPERTS_PER_GROUP + hi)[None, :], axis=0)[0]
    onehot = (cls[None, :] == jnp.arange(N_CLASSES, dtype=i32)[:, None]).astype(i32)
    csum = jnp.cumsum(onehot, axis=1)
    cnt = csum[:, -1]
    rank = jnp.sum(onehot * (csum - 1), axis=0)
    ptiles = (cnt + MOE_TILE - 1) // MOE_TILE
    tile_end = jnp.cumsum(ptiles)
    tile_off = tile_end - ptiles
    pos = jnp.take(tile_off, cls) * MOE_TILE + rank
    n_tiles = _moe_num_tiles(t)
    as_i32 = lambda g: lax.bitcast_convert_type(g.astype(F32), i32)
    per_token = jnp.stack([jnp.arange(t, dtype=i32), as_i32(g_lo), as_i32(g_hi)], axis=1)
    per_row = jnp.zeros((n_tiles * MOE_TILE, 3), i32).at[pos].set(per_token, unique_indices=True)
    row_token = per_row[:, 0]
    row_gates = lax.bitcast_convert_type(per_row[:, 1:3], F32)
    tid = jnp.arange(n_tiles, dtype=i32)
    tcls = jnp.minimum(jnp.sum((tid[:, None] >= tile_end[None, :]).astype(i32), axis=1), N_CLASSES - 1)
    n_valid = jnp.clip(jnp.take(cnt, tcls) - (tid - jnp.take(tile_off, tcls)) * MOE_TILE, 0, MOE_TILE)
    n_valid = jnp.where(tid < tile_end[-1], n_valid, 0).astype(i32)
    cls_lo = jnp.array([g * EXPERTS_PER_GROUP + p[0] for g in range(N_GROUPS) for p in _PAIRS], i32)
    cls_hi = jnp.array([g * EXPERTS_PER_GROUP + p[1] for g in range(N_GROUPS) for p in _PAIRS], i32)
    return row_token, jnp.take(cls_lo, tcls), jnp.take(cls_hi, tcls), n_valid, row_gates


def _moe_kernel(rt_ref, ea_ref, eb_ref, nv_ref, h_hbm, rg_ref, wga_ref, wua_ref, wda_ref, wgb_ref, wub_ref,
                wdb_ref, lg_ref, lb_ref, out_hbm, xbuf, obuf, wup_buf, wdn_buf, sem_in, sem_out):
    i = pl.program_id(0)
    n = pl.num_programs(0)
    slot = lax.rem(i, 2)

    def rows_in(row, s, k, rows=1):
        return pltpu.make_async_copy(h_hbm.at[pl.ds(row, rows)], xbuf.at[s, pl.ds(k, rows)], sem_in.at[s])

    def rows_out(row, s, k, rows=1):
        return pltpu.make_async_copy(obuf.at[s, pl.ds(k, rows)], out_hbm.at[pl.ds(row, rows)], sem_out.at[s])

    def issue(copy_at, tile, count):
        base = tile * MOE_TILE

        def group(q, c):
            for r in range(MOE_ISSUE_UNROLL):
                copy_at(base, q * MOE_ISSUE_UNROLL + r).start()
            return c

        def single(k, c):
            copy_at(base, k).start()
            return c
        full = lax.shift_right_logical(count, int(math.log2(MOE_ISSUE_UNROLL)))
        lax.fori_loop(0, full, group, 0)
        lax.fori_loop(full * MOE_ISSUE_UNROLL, count, single, 0)

    def wait_rows(copies, count):
        rows = MOE_TILE
        while rows >= 1:
            @pl.when((count & rows) != 0)
            def _(rows=rows):
                copies(0, 0, rows).wait()
            rows //= 2

    def start_gather(tile, s):
        issue(lambda base, k: rows_in(rt_ref[base + k], s, k), tile, nv_ref[tile])

    @pl.when(i == 0)
    def _():
        xbuf[...] = jnp.zeros(xbuf.shape, F32)
        start_gather(0, 0)

    @pl.when(i + 1 < n)
    def _():
        start_gather(i + 1, 1 - slot)

    wait_rows(lambda row, k, rows: rows_in(row, slot, k, rows), nv_ref[i])

    @pl.when(i >= 2)
    def _():
        wait_rows(lambda row, k, rows: rows_out(row, slot, k, rows), nv_ref[i - 2])

    @pl.when(nv_ref[i] > 0)
    def _():
        prev = jnp.maximum(i - 1, 0)
        for which, e_ref, (g_ref, u_ref, d_ref) in ((0, ea_ref, (wga_ref, wua_ref, wda_ref)),
                                                    (1, eb_ref, (wgb_ref, wub_ref, wdb_ref))):
            @pl.when((i == 0) | (e_ref[i] != e_ref[prev]))
            def _(which=which, g_ref=g_ref, u_ref=u_ref, d_ref=d_ref):
                wup_buf[2 * which] = g_ref[0, 0].astype(BF16)
                wup_buf[2 * which + 1] = u_ref[0, 0].astype(BF16)
                wdn_buf[which] = d_ref[0, 0].astype(BF16)

        x = xbuf[slot]
        xb = x.astype(BF16)
        rg = rg_ref[...]
        act_a = (_silu(_dot(xb, wup_buf[0])) * _dot(xb, wup_buf[1])).astype(BF16)
        act_b = (_silu(_dot(xb, wup_buf[2])) * _dot(xb, wup_buf[3])).astype(BF16)
        moe = rg[:, 0:1] * _dot(act_a, wdn_buf[0]) + rg[:, 1:2] * _dot(act_b, wdn_buf[1])
        obuf[slot] = _layernorm_rows(DN_ALPHA * x + moe, lg_ref[...], lb_ref[...])
        issue(lambda base, k: rows_out(rt_ref[base + k], slot, k), i, nv_ref[i])

    @pl.when(i == n - 1)
    def _():
        @pl.when(i >= 1)
        def _():
            wait_rows(lambda row, k, rows: rows_out(row, 1 - slot, k, rows), nv_ref[i - 1])
        wait_rows(lambda row, k, rows: rows_out(row, slot, k, rows), nv_ref[i])


def _moe(h, gates_t, wg, wu, wd, layer, ln_g, ln_b):
    t, d = h.shape
    n_tiles = _moe_num_tiles(t)
    row_token, exp_a, exp_b, n_valid, row_gates = _route_meta(gates_t)
    w_up = lambda which: pl.BlockSpec((1, 1, d, D_EXPERT),
                                      lambda i, rt, ea, eb, nv: (layer, (ea, eb)[which][i], 0, 0))
    w_dn = lambda which: pl.BlockSpec((1, 1, D_EXPERT, d),
                                      lambda i, rt, ea, eb, nv: (layer, (ea, eb)[which][i], 0, 0))
    const = lambda shape: pl.BlockSpec(shape, lambda i, *_: (0,) * len(shape))
    grid_spec = pltpu.PrefetchScalarGridSpec(
        num_scalar_prefetch=4,
        grid=(n_tiles,),
        in_specs=[pl.BlockSpec(memory_space=pl.ANY),
                  pl.BlockSpec((MOE_TILE, 2), lambda i, *_: (i, 0)),
                  w_up(0), w_up(0), w_dn(0), w_up(1), w_up(1), w_dn(1), const((1, d)), const((1, d))],
        out_specs=pl.BlockSpec(memory_space=pl.ANY),
        scratch_shapes=[pltpu.VMEM((2, MOE_TILE, d), F32), pltpu.VMEM((2, MOE_TILE, d), F32),
                        pltpu.VMEM((4, d, D_EXPERT), BF16), pltpu.VMEM((2, D_EXPERT, d), BF16),
                        pltpu.SemaphoreType.DMA((2,)), pltpu.SemaphoreType.DMA((2,))],
    )
    return pl.pallas_call(
        _moe_kernel,
        grid_spec=grid_spec,
        out_shape=jax.ShapeDtypeStruct((t, d), F32),
        compiler_params=_cparams(("arbitrary",)),
        name="experts_ln",
    )(row_token, exp_a, exp_b, n_valid, h, row_gates, wg, wu, wd, wg, wu, wd, ln_g.reshape(1, d), ln_b.reshape(1, d))


def _in_weights(w_in_l):
    offs = [0]
    for s in IN_SIZES:
        offs.append(offs[-1] + s)
    w_bf = w_in_l.astype(BF16)
    seg = [w_bf[:, offs[i]:offs[i + 1]] for i in range(len(IN_SIZES))]
    seg[2] = jnp.pad(seg[2], ((0, 0), (0, LANES - SSD_HEADS)))
    return seg


def kernel(x, ln_in_g, ln_in_b, w_in, ssd_conv_w, ssd_conv_b, ssd_dt_bias, ssd_a_log, ssd_d, ssd_norm_w, rwkv_mu, rwkv_w0, rwkv_w2, rwkv_a0, rwkv_a2, rwkv_g2, rwkv_k_k, rwkv_k_a, rwkv_r_k, rwkv_ln_w, rwkv_ln_b, hgrn_lb, hgrn_norm_w, w_br_ssd, w_br_rwkv, w_br_hgrn, w_out, ln1_g, ln1_b, router_w, router_bias, exp_w_gate, exp_w_up, exp_w_down, ln2_g, ln2_b):
    batch, seq, d = x.shape
    t = batch * seq
    h, h_bf = _layernorm(x.reshape(t, d), ln_in_g, ln_in_b)
    lsm = jax.nn.softmax(hgrn_lb.astype(F32), axis=0)
    lower_bounds = jnp.cumsum(lsm, axis=0) - lsm[0]
    for l in range(DEPTH):
        names = ("z", "xbc", "dt", "rwkv", "hgrn", "gates")
        z, xbc, dt_raw, f_rwkv, f_hgrn, gates = [
            _project(h_bf, w, f"proj_{n}", F32 if n == "dt" else BF16)
            for w, n in zip(_in_weights(w_in[l]), names)]
        y_a = _ssd(z, xbc, dt_raw, ssd_conv_w[l], ssd_conv_b[l], ssd_dt_bias[l], ssd_a_log[l], ssd_d[l],
                   ssd_norm_w[l], batch)
        y_b = _rwkv(f_rwkv, rwkv_mu[l], rwkv_w0[l], rwkv_w2[l], rwkv_a0[l], rwkv_a2[l], rwkv_g2[l],
                    rwkv_k_k[l], rwkv_k_a[l], rwkv_r_k[l].reshape(-1), rwkv_ln_w[l], rwkv_ln_b[l], batch)
        y_c = _hgrn(f_hgrn, lower_bounds[l], hgrn_norm_w[l], batch)
        h = _merge(y_a, y_b, y_c, gates, h, w_br_ssd[l], w_br_rwkv[l], w_br_hgrn[l], w_out[l],
                   ln1_g[l], ln1_b[l])
        gates_t = _router(h, router_w, router_bias)
        h = _moe(h, gates_t, exp_w_gate, exp_w_up, exp_w_down, l, ln2_g[l], ln2_b[l])
        h_bf = h.astype(BF16)
    return h.reshape(batch, seq, d)
```

```python
import functools
import math

import jax
import jax.numpy as jnp
from jax import lax
from jax.experimental import pallas as pl
from jax.experimental.pallas import tpu as pltpu

F32 = jnp.float32
BF16 = jnp.bfloat16

D_MODEL = 1024
DEPTH = 2
CHUNK = 64
STEP_CHUNKS = 4
STEP_ROWS = STEP_CHUNKS * CHUNK
LANES = 128

SSD_HEADS = 16
SSD_HEADDIM = 64
SSD_GROUPS = 2
SSD_STATE = 128
SSD_CONV = 4
SSD_XBC = D_MODEL + 2 * SSD_GROUPS * SSD_STATE

RWKV_HEADS = 16
RWKV_HEAD = 64
RWKV_COLS = 3 * D_MODEL + 64 + 64 + 128
RWKV_GN_EPS = 64e-5

HGRN_HEADS = 8
HGRN_EXPAND = 128

IN_SIZES = (D_MODEL, SSD_XBC, SSD_HEADS, RWKV_COLS, 4 * D_MODEL, 3 * D_MODEL)

N_EXPERTS = 16
N_GROUPS = 4
EXPERTS_PER_GROUP = 4
D_EXPERT = 512

DN_ALPHA = (2 * DEPTH) ** 0.25
LN_EPS = 1e-5
RMS_EPS = 1e-6

VMEM_LIMIT = 56 * 1024 * 1024

_NN = (((1,), (0,)), ((), ()))
_NT = (((1,), (1,)), ((), ()))
_TN = (((0,), (0,)), ((), ()))


def _dot(a, b, dims=_NN):
    return lax.dot_general(a, b, dims, preferred_element_type=F32)


def _bdot(a, b, dims=_NN):
    return _dot(a.astype(BF16), b.astype(BF16), dims)


def _split3(x):
    x1 = x.astype(BF16)
    r1 = x - x1.astype(F32)
    x2 = r1.astype(BF16)
    x3 = (r1 - x2.astype(F32)).astype(BF16)
    return x1, x2, x3


def _sel_r(x, sel, pieces=3):
    n = x.shape[0]
    if pieces == 1:
        return _bdot(x, sel)
    y = _dot(jnp.concatenate(_split3(x)[:pieces], axis=0), sel)
    out = y[:n] + y[n:2 * n]
    return out + y[2 * n:] if pieces == 3 else out


def _chunk_cumsum(x):
    row = lax.broadcasted_iota(jnp.int32, x.shape, 0) & (CHUNK - 1)
    shift = 1
    while shift < CHUNK:
        x = x + jnp.where(row >= shift, pltpu.roll(x, shift, 0), 0.0)
        shift *= 2
    return x


def _sigmoid(x):
    return 1.0 / (1.0 + jnp.exp(-x))


def _silu(x):
    return x * _sigmoid(x)


def _softplus(x):
    return jnp.maximum(x, 0.0) + jnp.log1p(jnp.exp(-jnp.abs(x)))


def _layernorm_rows(x, g, b):
    mu = jnp.mean(x, axis=-1, keepdims=True)
    xc = x - mu
    var = jnp.mean(xc * xc, axis=-1, keepdims=True)
    return xc * lax.rsqrt(var + LN_EPS) * g + b


def _tri(n, strict=False):
    r = lax.broadcasted_iota(jnp.int32, (n, n), 0)
    c = lax.broadcasted_iota(jnp.int32, (n, n), 1)
    return (c < r) if strict else (c <= r)


def _cparams(sem):
    return pltpu.CompilerParams(dimension_semantics=sem, vmem_limit_bytes=VMEM_LIMIT)


def _full(shape):
    return pl.BlockSpec(shape, lambda *_: (0,) * len(shape))


def _ln_kernel(x_ref, g_ref, b_ref, o_ref, ob_ref):
    y = _layernorm_rows(x_ref[...], g_ref[...], b_ref[...])
    o_ref[...] = y
    ob_ref[...] = y.astype(BF16)


def _layernorm(x, g, b, tm=512):
    t, d = x.shape
    row = pl.BlockSpec((tm, d), lambda i: (i, 0))
    return pl.pallas_call(
        _ln_kernel,
        grid=(t // tm,),
        in_specs=[row, _full((1, d)), _full((1, d))],
        out_specs=[row, row],
        out_shape=[jax.ShapeDtypeStruct((t, d), F32), jax.ShapeDtypeStruct((t, d), BF16)],
        compiler_params=_cparams(("parallel",)),
        name="layernorm_in",
    )(x, g.reshape(1, d), b.reshape(1, d))


def _proj_kernel(x_ref, w_ref, o_ref):
    o_ref[...] = _dot(x_ref[...], w_ref[0]).astype(o_ref.dtype)


_IN_LAYOUT = (("hgrn", 4, 4 * D_MODEL, 2048), ("gates", 5, 3 * D_MODEL, 1024), ("z", 0, D_MODEL, 1024),
              ("xbc", 1, SSD_XBC, 512), ("pad", None, 256, 256), ("rwkv", 3, RWKV_COLS, RWKV_COLS // 2),
              ("dt", 2, LANES, LANES))


def _in_layout_offsets():
    offs, acc = {}, 0
    for name, _, width, tn in _IN_LAYOUT:
        assert acc % tn == 0 and width % tn == 0, name
        offs[name] = acc
        acc += width
    return offs, acc


def _pack_in_weights(w_in):
    src, acc = [], 0
    for s in IN_SIZES:
        src.append((acc, s))
        acc += s
    parts = []
    for _, idx, width, _ in _IN_LAYOUT:
        if idx is None:
            parts.append(jnp.zeros(w_in.shape[:2] + (width,), w_in.dtype))
            continue
        start, size = src[idx]
        part = w_in[:, :, start:start + size]
        if size < width:
            part = jnp.pad(part, ((0, 0), (0, 0), (0, width - size)))
        parts.append(part)
    return jnp.concatenate(parts, axis=-1).astype(BF16)


def _project(x_bf, w_all, layer, seg, out_dtype=BF16, tm=1024):
    t, k = x_bf.shape
    tm = min(tm, t)
    offs, _ = _in_layout_offsets()
    n, tn = next((width, tn) for name, _, width, tn in _IN_LAYOUT if name == seg)
    first = offs[seg] // tn
    return pl.pallas_call(
        _proj_kernel,
        grid=(n // tn, t // tm),
        in_specs=[pl.BlockSpec((tm, k), lambda j, i: (i, 0)),
                  pl.BlockSpec((1, k, tn), lambda j, i: (layer, 0, first + j))],
        out_specs=pl.BlockSpec((tm, tn), lambda j, i: (i, j)),
        out_shape=jax.ShapeDtypeStruct((t, n), out_dtype),
        compiler_params=_cparams(("parallel", "parallel")),
        name=f"proj_{seg}",
    )(x_bf, w_all)


def _head_expand(n_heads, width):
    h = lax.broadcasted_iota(jnp.int32, (LANES, n_heads * width), 0)
    c = lax.broadcasted_iota(jnp.int32, (LANES, n_heads * width), 1)
    return (c // width == h).astype(BF16)


def _ssd_kernel(z_ref, xbc_ref, dt_ref, cw_ref, cb_ref, dtb_ref, a_ref, dsk_ref, nw_ref, e_ref,
                o_ref, pad_ref, st_ref):
    c = pl.program_id(1)
    n_pairs = SSD_HEADS // 2

    @pl.when(c == 0)
    def _():
        pad_ref[0:8, :] = jnp.zeros((8, SSD_XBC), F32)
        st_ref[...] = jnp.zeros(st_ref.shape, F32)

    pad_ref[8:8 + STEP_ROWS, :] = xbc_ref[...].astype(F32)
    padded = pad_ref[...]
    acc = cb_ref[...] + cw_ref[SSD_CONV - 1:SSD_CONV, :] * padded[8:, :]
    for shift in range(1, SSD_CONV):
        j = SSD_CONV - 1 - shift
        acc = acc + cw_ref[j:j + 1, :] * pltpu.roll(padded, shift, 0)[8:, :]
    pad_ref[0:8, :] = pad_ref[STEP_ROWS:STEP_ROWS + 8, :]
    xbc = _silu(acc)
    xs = xbc[:, :D_MODEL]

    e_mat = e_ref[...]
    dt = _softplus(dt_ref[...] + dtb_ref[...])
    acs = _chunk_cumsum(dt * a_ref[...])
    acs_all = _sel_r(acs, e_mat)
    xs_dt_all = xs * _sel_r(dt, e_mat)

    li = lax.broadcasted_iota(jnp.int32, (CHUNK, D_MODEL), 0)
    si = lax.broadcasted_iota(jnp.int32, (CHUNK, D_MODEL), 1) & (SSD_HEADDIM - 1)
    lane = lax.broadcasted_iota(jnp.int32, (CHUNK, LANES), 1)
    m0 = lane < SSD_HEADDIM

    pairs = range(n_pairs)
    sls = [slice(j * LANES, (j + 1) * LANES) for j in pairs]
    grp = [j // (n_pairs // SSD_GROUPS) for j in pairs]
    st = [st_ref[j] for j in pairs]
    y_rows = []
    for sub in range(STEP_CHUNKS):
        rs = slice(sub * CHUNK, (sub + 1) * CHUNK)
        acs_e, xs_dt = acs_all[rs], xs_dt_all[rs]
        acs_row = jnp.sum(jnp.where(li == si, acs_e, 0.0), axis=0, keepdims=True)
        decay = jnp.exp(jnp.where(si <= li, acs_e - acs_row, -jnp.inf))
        exp_acs = jnp.exp(acs_e)
        acs_last = acs_e[CHUNK - 1:CHUNK, :]
        to_end = jnp.exp(acs_last - acs_e)
        exp_last = jnp.exp(acs_last)
        bm = [xbc[rs, D_MODEL + g * SSD_STATE:D_MODEL + (g + 1) * SSD_STATE].astype(BF16)
              for g in range(SSD_GROUPS)]
        cm = [xbc[rs, D_MODEL + (SSD_GROUPS + g) * SSD_STATE:D_MODEL + (SSD_GROUPS + g + 1) * SSD_STATE
                  ].astype(BF16) for g in range(SSD_GROUPS)]
        cb2 = [_dot(cm[g], jnp.concatenate([bm[g], bm[g]], axis=0), _NT) for g in range(SSD_GROUPS)]
        xp = [xs_dt[:, sl] for sl in sls]
        xbd = [jnp.concatenate([jnp.where(m0, x, 0.0), jnp.where(m0, 0.0, x)], axis=0) for x in xp]
        y_parts = [_bdot(cb2[grp[j]] * decay[:, sls[j]], xbd[j]) + _bdot(cm[grp[j]], st[j]) * exp_acs[:, sls[j]]
                   for j in pairs]
        upd = [_bdot(bm[grp[j]], xp[j] * to_end[:, sls[j]], _TN) for j in pairs]
        st = [st[j] * exp_last[:, sls[j]] + upd[j] for j in pairs]
        y_rows.append(jnp.concatenate(y_parts, axis=1))
    for j in pairs:
        st_ref[j] = st[j]
    y = jnp.concatenate(y_rows, axis=0) + xs * dsk_ref[...]
    y = y * _silu(z_ref[...].astype(F32))
    gw = D_MODEL // SSD_GROUPS
    for g in range(SSD_GROUPS):
        yg = y[:, g * gw:(g + 1) * gw]
        ms = jnp.mean(yg * yg, axis=-1, keepdims=True)
        o_ref[:, g * gw:(g + 1) * gw] = (yg * lax.rsqrt(ms + RMS_EPS) * nw_ref[:, g * gw:(g + 1) * gw]
                                         ).astype(o_ref.dtype)


def _ssd(z, xbc, dt_raw, conv_w, conv_b, dt_bias, a_log, d_skip, norm_w, batch):
    t = z.shape[0]
    nc = t // batch // STEP_ROWS
    pad16 = lambda v: jnp.pad(v.astype(F32), (0, LANES - SSD_HEADS)).reshape(1, LANES)
    row = lambda w: pl.BlockSpec((STEP_ROWS, w), lambda b, c: (b * nc + c, 0))
    return pl.pallas_call(
        _ssd_kernel,
        grid=(batch, nc),
        in_specs=[row(D_MODEL), row(SSD_XBC), row(LANES), _full((SSD_CONV, SSD_XBC)), _full((1, SSD_XBC)),
                  _full((1, LANES)), _full((1, LANES)), _full((1, D_MODEL)), _full((1, D_MODEL)),
                  _full((LANES, D_MODEL))],
        out_specs=row(D_MODEL),
        out_shape=jax.ShapeDtypeStruct((t, D_MODEL), BF16),
        scratch_shapes=[pltpu.VMEM((STEP_ROWS + 8, SSD_XBC), F32),
                        pltpu.VMEM((SSD_HEADS // 2, SSD_STATE, LANES), F32)],
        compiler_params=_cparams(("parallel", "arbitrary")),
        name="ssd",
    )(z, xbc, dt_raw, conv_w, conv_b.reshape(1, -1), pad16(dt_bias), pad16(-jnp.exp(a_log.astype(F32))),
      jnp.repeat(d_skip.astype(F32), SSD_HEADDIM).reshape(1, -1), norm_w.reshape(1, -1),
      _head_expand(SSD_HEADS, SSD_HEADDIM))


def _hgrn_kernel(f_ref, lb_ref, nw_ref, o_ref, st_ref):
    c = pl.program_id(1)

    @pl.when(c == 0)
    def _():
        st_ref[...] = jnp.zeros(st_ref.shape, F32)

    d = D_MODEL
    lb = lb_ref[...]
    q = _silu(f_ref[:, 0:d].astype(F32))
    forget = lb + (1.0 - lb) * _sigmoid(f_ref[:, d:2 * d].astype(F32))
    k = 1.0 - forget
    bc_all = _chunk_cumsum(jnp.log(forget))
    qd_all = q * jnp.exp(bc_all)
    causal = _tri(CHUNK)
    heads = range(HGRN_HEADS)
    sls = [slice(h * HGRN_EXPAND, (h + 1) * HGRN_EXPAND) for h in heads]
    st = [st_ref[h] for h in heads]
    for sub in range(STEP_CHUNKS):
        rs = slice(sub * CHUNK, (sub + 1) * CHUNK)
        bc, qs, ks, qd = bc_all[rs], q[rs], k[rs], qd_all[rs]
        mid = bc[CHUNK // 2:CHUNK // 2 + 1, :]
        last = bc[CHUNK - 1:CHUNK, :]
        qe = qs * jnp.exp(bc - mid)
        ke = ks * jnp.exp(mid - bc)
        kd = ks * jnp.exp(last - bc)
        w_last = jnp.exp(last)
        vb = [f_ref[rs, 2 * d + h * HGRN_EXPAND:2 * d + (h + 1) * HGRN_EXPAND].astype(BF16) for h in heads]
        att = [jnp.where(causal, _bdot(qe[:, sl], ke[:, sl], _NT), 0.0) for sl in sls]
        o = [_bdot(att[h], vb[h]) + _bdot(qd[:, sls[h]], st[h], _NT) for h in heads]
        upd = [_bdot(vb[h], kd[:, sls[h]], _TN) for h in heads]
        st = [st[h] * w_last[:, sls[h]] + upd[h] for h in heads]
        for h in heads:
            oh = o[h] * lax.rsqrt(jnp.mean(o[h] * o[h], axis=-1, keepdims=True) + RMS_EPS) * nw_ref[...]
            gate = _sigmoid(f_ref[rs, 3 * d + h * HGRN_EXPAND:3 * d + (h + 1) * HGRN_EXPAND].astype(F32))
            o_ref[rs, sls[h]] = (oh * gate).astype(o_ref.dtype)
    for h in heads:
        st_ref[h] = st[h]


def _hgrn(feat, lb, norm_w, batch):
    t = feat.shape[0]
    nc = t // batch // STEP_ROWS
    row = lambda w: pl.BlockSpec((STEP_ROWS, w), lambda b, c: (b * nc + c, 0))
    return pl.pallas_call(
        _hgrn_kernel,
        grid=(batch, nc),
        in_specs=[row(4 * D_MODEL), _full((1, D_MODEL)), _full((1, HGRN_EXPAND))],
        out_specs=row(D_MODEL),
        out_shape=jax.ShapeDtypeStruct((t, D_MODEL), BF16),
        scratch_shapes=[pltpu.VMEM((HGRN_HEADS, HGRN_EXPAND, HGRN_EXPAND), F32)],
        compiler_params=_cparams(("parallel", "arbitrary")),
        name="hgrn2",
    )(feat, lb.reshape(1, -1), norm_w.reshape(1, -1))


def _rwkv_kernel(f_ref, mu_ref, w0_ref, w2_ref, a0_ref, a2_ref, g2_ref, kk_ref, ka_ref, rk_ref, lnw_ref,
                 lnb_ref, e_ref, et_ref, o_ref, pad_ref, st_ref, y_ref):
    c = pl.program_id(1)
    d = D_MODEL
    n_pairs = RWKV_HEADS // 2

    @pl.when(c == 0)
    def _():
        pad_ref[0:8, :] = jnp.zeros((8, RWKV_COLS), F32)
        st_ref[...] = jnp.zeros(st_ref.shape, F32)

    cur = f_ref[...].astype(F32)
    pad_ref[8:8 + STEP_ROWS, :] = cur
    prev = pltpu.roll(pad_ref[...], 1, 0)[8:, :]
    pad_ref[0:8, :] = pad_ref[STEP_ROWS:STEP_ROWS + 8, :]
    x = cur + (prev - cur) * mu_ref[...]
    r, k, v = x[:, 0:d], x[:, d:2 * d], x[:, 2 * d:3 * d]
    lora_in = x[:, 3 * d:3 * d + LANES]
    xg = x[:, 3 * d + LANES:3 * d + 2 * LANES]

    e_mat, et_mat = e_ref[...], et_ref[...]
    w = -_softplus(-(w0_ref[...] + _bdot(jnp.tanh(lora_in), w2_ref[...]))) - 0.5
    ld = -jnp.exp(w)
    a = _sigmoid(a0_ref[...] + _bdot(lora_in, a2_ref[...]))
    gate = _bdot(_sigmoid(xg), g2_ref[...])
    kk = k * kk_ref[...]
    nrm = jnp.maximum(jnp.sqrt(_sel_r(kk * kk, et_mat, 1)), 1e-12)
    kk = kk * _sel_r(1.0 / nrm, e_mat, 2)
    k2 = k * (1.0 + (a - 1.0) * ka_ref[...])
    cw = _chunk_cumsum(ld)
    e_neg = jnp.exp(-cw)
    a_t = -kk * jnp.exp(cw - ld)
    r_t = r * jnp.exp(cw)
    b_t = kk * a * e_neg
    k_t = k2 * e_neg

    lane = lax.broadcasted_iota(jnp.int32, (CHUNK, LANES), 1)
    m0 = lane < RWKV_HEAD
    row2 = lax.broadcasted_iota(jnp.int32, (LANES, LANES), 0)
    col2 = lax.broadcasted_iota(jnp.int32, (LANES, LANES), 1)
    same = (row2 < CHUNK) == (col2 < RWKV_HEAD)
    row4 = lax.broadcasted_iota(jnp.int32, (2 * LANES, LANES), 0)
    col4 = lax.broadcasted_iota(jnp.int32, (2 * LANES, LANES), 1)
    top4 = (row4 & (LANES - 1)) < CHUNK
    tt, ss = row4 & (CHUNK - 1), col4 & (CHUNK - 1)
    bd4 = (top4 == (col4 < RWKV_HEAD)) & ((ss < tt) | ((row4 >= LANES) & (ss == tt)))

    def stack(p):
        return jnp.concatenate([jnp.where(m0, p, 0.0), jnp.where(m0, 0.0, p)], axis=0)

    pairs = range(n_pairs)
    sls = [slice(j * LANES, (j + 1) * LANES) for j in pairs]
    bf = lambda xs: [x.astype(BF16) for x in xs]
    mt = [st_ref[j] for j in pairs]
    for sub in range(STEP_CHUNKS):
        rs = slice(sub * CHUNK, (sub + 1) * CHUNK)
        w_last = jnp.exp(cw[(sub + 1) * CHUNK - 1:(sub + 1) * CHUNK, :])
        s12 = bf(jnp.concatenate([stack(a_t[rs, sl]), stack(r_t[rs, sl])], axis=0) for sl in sls)
        vst = bf(stack(v[rs, sl]) for sl in sls)
        qmat = bf(jnp.concatenate([b_t[rs, sl], k_t[rs, sl]], axis=0) for sl in sls)
        mtb = bf(mt)
        gh = [_dot(s12[j], qmat[j], _NT) for j in pairs]
        ghr = [pltpu.roll(x, CHUNK, 1) for x in gh]
        n_ab, a_rb, akrk = [], [], []
        for j in pairs:
            diag = jnp.where(bd4, jnp.where(top4, gh[j], ghr[j]), 0.0)
            n_ab.append(diag[:LANES])
            a_rb.append(diag[LANES:].astype(BF16))
            akrk.append(jnp.where(bd4, jnp.where(top4, ghr[j], gh[j]), 0.0).astype(BF16))
        sm = [_dot(s12[j], mtb[j], _NT) for j in pairs]
        av = [_dot(akrk[j], vst[j]) for j in pairs]
        xm = [sm[j][:LANES] + av[j][:LANES] for j in pairs]
        tp = n_ab
        pwf = [_bdot(x, x) for x in n_ab]
        n_steps = int(math.log2(CHUNK)) - 1
        for step in range(n_steps):
            pw = bf(pwf)
            if step + 1 < n_steps:
                both = [_dot(jnp.concatenate([pw[j], tp[j].astype(BF16)], axis=0), pw[j]) for j in pairs]
                tp = [tp[j] + pwf[j] + both[j][LANES:] for j in pairs]
                pwf = [both[j][:LANES] for j in pairs]
            else:
                tp = [tp[j] + pwf[j] + _bdot(tp[j], pw[j]) for j in pairs]
        u = [xm[j] + _bdot(tp[j], xm[j]) for j in pairs]
        yst = [sm[j][LANES:] + av[j][LANES:] + _bdot(a_rb[j], u[j]) for j in pairs]
        for j in pairs:
            y_ref[rs, sls[j]] = yst[j][:CHUNK] + yst[j][CHUNK:]
        uv = [jnp.concatenate([u[j][:CHUNK] + u[j][CHUNK:], v[rs, sls[j]]], axis=0) for j in pairs]
        upd = [jnp.where(same, _bdot(uv[j], qmat[j], _TN), 0.0) for j in pairs]
        mt = [(mt[j] + upd[j]) * w_last[:, sls[j]] for j in pairs]
    for j in pairs:
        st_ref[j] = mt[j]

    y = y_ref[...]
    inv_n = 1.0 / RWKV_HEAD
    yc = y - _sel_r(_sel_r(y, et_mat, 1) * inv_n, e_mat, 2)
    rs = lax.rsqrt(_sel_r(yc * yc, et_mat, 1) * inv_n + RWKV_GN_EPS)
    yn = yc * _sel_r(rs, e_mat, 2) * lnw_ref[...] + lnb_ref[...]
    bonus = _sel_r(_sel_r(r * k2 * rk_ref[...], et_mat, 1), e_mat, 2) * v
    o_ref[...] = ((yn + bonus) * gate).astype(o_ref.dtype)


def _rwkv(feat, mu, w0, w2, a0, a2, g2, k_k, k_a, r_k, ln_w, ln_b, batch):
    t = feat.shape[0]
    nc = t // batch // STEP_ROWS
    d = D_MODEL
    row = lambda w: pl.BlockSpec((STEP_ROWS, w), lambda b, c: (b * nc + c, 0))
    vec = lambda v: v.astype(F32).reshape(1, -1)
    w2p = jnp.concatenate([w2, jnp.zeros_like(w2)], axis=0).astype(BF16)
    a2p = jnp.concatenate([jnp.zeros_like(a2), a2], axis=0).astype(BF16)
    e_mat = _head_expand(RWKV_HEADS, RWKV_HEAD)
    return pl.pallas_call(
        _rwkv_kernel,
        grid=(batch, nc),
        in_specs=[row(RWKV_COLS), _full((1, RWKV_COLS)), _full((1, d)), _full((LANES, d)), _full((1, d)),
                  _full((LANES, d)), _full((LANES, d)), _full((1, d)), _full((1, d)), _full((1, d)),
                  _full((1, d)), _full((1, d)), _full((LANES, d)), _full((d, LANES))],
        out_specs=row(d),
        out_shape=jax.ShapeDtypeStruct((t, d), BF16),
        scratch_shapes=[pltpu.VMEM((STEP_ROWS + 8, RWKV_COLS), F32),
                        pltpu.VMEM((RWKV_HEADS // 2, LANES, LANES), F32),
                        pltpu.VMEM((STEP_ROWS, d), F32)],
        compiler_params=_cparams(("parallel", "arbitrary")),
        name="rwkv7",
    )(feat, vec(mu), vec(w0), w2p, vec(a0), a2p, g2.astype(BF16), vec(k_k), vec(k_a), vec(r_k), vec(ln_w),
      vec(ln_b), e_mat, e_mat.T)


def _merge_kernel(ya_ref, yb_ref, yc_ref, g_ref, h_ref, wa_ref, wb_ref, wc_ref, wo_ref, lg_ref, lb_ref, o_ref):
    d = D_MODEL
    gate = lambda j: _sigmoid(g_ref[:, j * d:(j + 1) * d].astype(F32))
    m = (gate(0) * _dot(ya_ref[...], wa_ref[...]) + gate(1) * _dot(yb_ref[...], wb_ref[...])
         + gate(2) * _dot(yc_ref[...], wc_ref[...]))
    hn = DN_ALPHA * h_ref[...] + _bdot(m, wo_ref[...])
    o_ref[...] = _layernorm_rows(hn, lg_ref[...], lb_ref[...])


def _merge(ya, yb, yc, gates, h, wa, wb, wc, wo, ln_g, ln_b, tm=512):
    t, d = h.shape
    tm = min(tm, t)
    row = lambda w: pl.BlockSpec((tm, w), lambda i: (i, 0))
    return pl.pallas_call(
        _merge_kernel,
        grid=(t // tm,),
        in_specs=[row(d), row(d), row(d), row(3 * d), row(d)] + [_full((d, d))] * 4 + [_full((1, d))] * 2,
        out_specs=row(d),
        out_shape=jax.ShapeDtypeStruct((t, d), F32),
        compiler_params=_cparams(("parallel",)),
        name="merge_ln",
    )(ya, yb, yc, gates, h, wa.astype(BF16), wb.astype(BF16), wc.astype(BF16), wo.astype(BF16),
      ln_g.reshape(1, d), ln_b.reshape(1, d))


def _router_kernel(h_ref, wt_ref, bias_ref, o_ref):
    logits = lax.dot_general(wt_ref[...], h_ref[...], _NT, precision=lax.Precision.HIGHEST,
                             preferred_element_type=F32)
    mx = jnp.max(logits, axis=0, keepdims=True)
    ex = jnp.exp(logits - mx)
    probs = ex / jnp.sum(ex, axis=0, keepdims=True)
    sel = probs + bias_ref[...]
    rows = [sel[e:e + 1, :] for e in range(N_EXPERTS)]
    prow = [probs[e:e + 1, :] for e in range(N_EXPERTS)]
    gscore = []
    for g in range(N_GROUPS):
        m = rows[g * EXPERTS_PER_GROUP:(g + 1) * EXPERTS_PER_GROUP]
        best = None
        for i in range(EXPERTS_PER_GROUP):
            for j in range(i + 1, EXPERTS_PER_GROUP):
                s = m[i] + m[j]
                best = s if best is None else jnp.maximum(best, s)
        gscore.append(best)
    chosen = []
    for g in range(N_GROUPS):
        ok = None
        for g2 in range(N_GROUPS):
            if g2 == g:
                continue
            t = (gscore[g] > gscore[g2]) if g2 < g else (gscore[g] >= gscore[g2])
            ok = t if ok is None else (ok & t)
        chosen.append(ok)
    picked = []
    for e in range(N_EXPERTS):
        g = e // EXPERTS_PER_GROUP
        rank = None
        for e2 in range(g * EXPERTS_PER_GROUP, (g + 1) * EXPERTS_PER_GROUP):
            if e2 == e:
                continue
            ahead = (rows[e2] >= rows[e]) if e2 < e else (rows[e2] > rows[e])
            ahead = ahead.astype(F32)
            rank = ahead if rank is None else rank + ahead
        picked.append(jnp.where(chosen[g] & (rank < 2.0), prow[e], 0.0))
    tot = picked[0]
    for e in range(1, N_EXPERTS):
        tot = tot + picked[e]
    inv = 1.0 / tot
    for e in range(N_EXPERTS):
        o_ref[e:e + 1, :] = picked[e] * inv


def _router(h, router_w, router_bias, tm=512):
    t, d = h.shape
    tm = min(tm, t)
    return pl.pallas_call(
        _router_kernel,
        grid=(t // tm,),
        in_specs=[pl.BlockSpec((tm, d), lambda i: (i, 0)), _full((N_EXPERTS, d)), _full((N_EXPERTS, 1))],
        out_specs=pl.BlockSpec((N_EXPERTS, tm), lambda i: (0, i)),
        out_shape=jax.ShapeDtypeStruct((N_EXPERTS, t), F32),
        compiler_params=_cparams(("parallel",)),
        name="router",
    )(h, router_w.T, router_bias.reshape(N_EXPERTS, 1).astype(F32))


MOE_TILE = 256
MOE_ISSUE_UNROLL = 8
_PAIRS = [(a, b) for a in range(EXPERTS_PER_GROUP) for b in range(a + 1, EXPERTS_PER_GROUP)]
N_CLASSES = N_GROUPS * len(_PAIRS)


def _moe_num_tiles(t):
    return -(-(t + N_CLASSES * (MOE_TILE - 1)) // MOE_TILE) + 1


def _route_meta(gates_t):
    e_n, t = gates_t.shape
    i32 = jnp.int32
    mask = gates_t > 0
    eidx = lax.broadcasted_iota(i32, (e_n, t), 0)
    e0 = jnp.minimum(jnp.min(jnp.where(mask, eidx, e_n), axis=0), e_n - 1)
    e1 = jnp.maximum(jnp.max(jnp.where(mask, eidx, -1), axis=0), e0)
    grp = e0 // EXPERTS_PER_GROUP
    a = e0 % EXPERTS_PER_GROUP
    b = jnp.where(e1 // EXPERTS_PER_GROUP == grp, e1 % EXPERTS_PER_GROUP, a)
    b = jnp.where(b == a, (a + 1) % EXPERTS_PER_GROUP, b)
    lo, hi = jnp.minimum(a, b), jnp.maximum(a, b)
    n_pairs = len(_PAIRS)
    cls = grp * n_pairs + (lo * (2 * EXPERTS_PER_GROUP - 1 - lo)) // 2 + (hi - lo - 1)
    g_lo = jnp.take_along_axis(gates_t, (grp * EXPERTS_PER_GROUP + lo)[None, :], axis=0)[0]
    g_hi = jnp.take_along_axis(gates_t, (grp * EXPERTS_PER_GROUP + hi)[None, :], axis=0)[0]
    onehot = (cls[None, :] == jnp.arange(N_CLASSES, dtype=i32)[:, None]).astype(i32)
    csum = jnp.cumsum(onehot, axis=1)
    cnt = csum[:, -1]
    rank = jnp.sum(onehot * (csum - 1), axis=0)
    ptiles = (cnt + MOE_TILE - 1) // MOE_TILE
    tile_end = jnp.cumsum(ptiles)
    tile_off = tile_end - ptiles
    pos = jnp.take(tile_off, cls) * MOE_TILE + rank
    n_tiles = _moe_num_tiles(t)
    as_i32 = lambda g: lax.bitcast_convert_type(g.astype(F32), i32)
    per_token = jnp.stack([jnp.arange(t, dtype=i32), as_i32(g_lo), as_i32(g_hi)], axis=1)
    per_row = jnp.zeros((n_tiles * MOE_TILE, 3), i32).at[pos].set(per_token, unique_indices=True)
    row_token = per_row[:, 0]
    row_gates = lax.bitcast_convert_type(per_row[:, 1:3], F32)
    tid = jnp.arange(n_tiles, dtype=i32)
    tcls = jnp.minimum(jnp.sum((tid[:, None] >= tile_end[None, :]).astype(i32), axis=1), N_CLASSES - 1)
    n_valid = jnp.clip(jnp.take(cnt, tcls) - (tid - jnp.take(tile_off, tcls)) * MOE_TILE, 0, MOE_TILE)
    n_valid = jnp.where(tid < tile_end[-1], n_valid, 0).astype(i32)
    cls_lo = jnp.array([g * EXPERTS_PER_GROUP + p[0] for g in range(N_GROUPS) for p in _PAIRS], i32)
    cls_hi = jnp.array([g * EXPERTS_PER_GROUP + p[1] for g in range(N_GROUPS) for p in _PAIRS], i32)
    return row_token, jnp.take(cls_lo, tcls), jnp.take(cls_hi, tcls), n_valid, row_gates


def _moe_kernel(rt_ref, ea_ref, eb_ref, nv_ref, h_hbm, rg_ref, wga_ref, wua_ref, wda_ref, wgb_ref, wub_ref,
                wdb_ref, lg_ref, lb_ref, out_hbm, xbuf, obuf, wup_buf, wdn_buf, sem_in, sem_out):
    i = pl.program_id(0)
    n = pl.num_programs(0)
    slot = lax.rem(i, 2)

    def rows_in(row, s, k, rows=1):
        return pltpu.make_async_copy(h_hbm.at[pl.ds(row, rows)], xbuf.at[s, pl.ds(k, rows)], sem_in.at[s])

    def rows_out(row, s, k, rows=1):
        return pltpu.make_async_copy(obuf.at[s, pl.ds(k, rows)], out_hbm.at[pl.ds(row, rows)], sem_out.at[s])

    def issue(copy_at, tile, count):
        base = tile * MOE_TILE

        def group(q, c):
            for r in range(MOE_ISSUE_UNROLL):
                copy_at(base, q * MOE_ISSUE_UNROLL + r).start()
            return c

        def single(k, c):
            copy_at(base, k).start()
            return c
        full = lax.shift_right_logical(count, int(math.log2(MOE_ISSUE_UNROLL)))
        lax.fori_loop(0, full, group, 0)
        lax.fori_loop(full * MOE_ISSUE_UNROLL, count, single, 0)

    def wait_rows(copies, count):
        rows = MOE_TILE
        while rows >= 1:
            @pl.when((count & rows) != 0)
            def _(rows=rows):
                copies(0, 0, rows).wait()
            rows //= 2

    def start_gather(tile, s, part=0, parts=1):
        per = MOE_TILE // parts
        for k in range(part * per, (part + 1) * per):
            rows_in(rt_ref[tile * MOE_TILE + k], s, k).start()

    @pl.when(i == 0)
    def _():
        start_gather(0, 0)

    @pl.when((i == 0) | (nv_ref[jnp.maximum(i - 1, 0)] > 0))
    def _():
        rows_in(0, slot, 0, MOE_TILE).wait()

    @pl.when(i >= 2)
    def _():
        wait_rows(lambda row, k, rows: rows_out(row, slot, k, rows), nv_ref[i - 2])

    @pl.when(nv_ref[i] > 0)
    def _():
        prev = jnp.maximum(i - 1, 0)
        for which, e_ref, (g_ref, u_ref, d_ref) in ((0, ea_ref, (wga_ref, wua_ref, wda_ref)),
                                                    (1, eb_ref, (wgb_ref, wub_ref, wdb_ref))):
            @pl.when((i == 0) | (e_ref[i] != e_ref[prev]))
            def _(which=which, g_ref=g_ref, u_ref=u_ref, d_ref=d_ref):
                wup_buf[2 * which] = g_ref[0, 0].astype(BF16)
                wup_buf[2 * which + 1] = u_ref[0, 0].astype(BF16)
                wdn_buf[which] = d_ref[0, 0].astype(BF16)

        nxt = lambda part: start_gather(i + 1, 1 - slot, part, 8)
        nxt(0)
        x = xbuf[slot]
        xb = x.astype(BF16)
        rg = rg_ref[...]
        nxt(1)
        gate_a = _silu(_dot(xb, wup_buf[0]))
        nxt(2)
        act_a = (gate_a * _dot(xb, wup_buf[1])).astype(BF16)
        nxt(3)
        gate_b = _silu(_dot(xb, wup_buf[2]))
        nxt(4)
        act_b = (gate_b * _dot(xb, wup_buf[3])).astype(BF16)
        nxt(5)
        moe = rg[:, 0:1] * _dot(act_a, wdn_buf[0])
        nxt(6)
        moe = moe + rg[:, 1:2] * _dot(act_b, wdn_buf[1])
        nxt(7)
        obuf[slot] = _layernorm_rows(DN_ALPHA * x + moe, lg_ref[...], lb_ref[...])
        issue(lambda base, k: rows_out(rt_ref[base + k], slot, k), i, nv_ref[i])

    @pl.when(i == n - 1)
    def _():
        @pl.when(i >= 1)
        def _():
            wait_rows(lambda row, k, rows: rows_out(row, 1 - slot, k, rows), nv_ref[i - 1])
        wait_rows(lambda row, k, rows: rows_out(row, slot, k, rows), nv_ref[i])


def _moe(h, gates_t, wg, wu, wd, layer, ln_g, ln_b):
    t, d = h.shape
    n_tiles = _moe_num_tiles(t)
    row_token, exp_a, exp_b, n_valid, row_gates = _route_meta(gates_t)
    w_up = lambda which: pl.BlockSpec((1, 1, d, D_EXPERT),
                                      lambda i, rt, ea, eb, nv: (layer, (ea, eb)[which][i], 0, 0))
    w_dn = lambda which: pl.BlockSpec((1, 1, D_EXPERT, d),
                                      lambda i, rt, ea, eb, nv: (layer, (ea, eb)[which][i], 0, 0))
    const = lambda shape: pl.BlockSpec(shape, lambda i, *_: (0,) * len(shape))
    grid_spec = pltpu.PrefetchScalarGridSpec(
        num_scalar_prefetch=4,
        grid=(n_tiles,),
        in_specs=[pl.BlockSpec(memory_space=pl.ANY),
                  pl.BlockSpec((MOE_TILE, 2), lambda i, *_: (i, 0)),
                  w_up(0), w_up(0), w_dn(0), w_up(1), w_up(1), w_dn(1), const((1, d)), const((1, d))],
        out_specs=pl.BlockSpec(memory_space=pl.ANY),
        scratch_shapes=[pltpu.VMEM((2, MOE_TILE, d), F32), pltpu.VMEM((2, MOE_TILE, d), F32),
                        pltpu.VMEM((4, d, D_EXPERT), BF16), pltpu.VMEM((2, D_EXPERT, d), BF16),
                        pltpu.SemaphoreType.DMA((2,)), pltpu.SemaphoreType.DMA((2,))],
    )
    return pl.pallas_call(
        _moe_kernel,
        grid_spec=grid_spec,
        out_shape=jax.ShapeDtypeStruct((t, d), F32),
        compiler_params=_cparams(("arbitrary",)),
        name="experts_ln",
    )(row_token, exp_a, exp_b, n_valid, h, row_gates, wg, wu, wd, wg, wu, wd, ln_g.reshape(1, d), ln_b.reshape(1, d))


def kernel(x, ln_in_g, ln_in_b, w_in, ssd_conv_w, ssd_conv_b, ssd_dt_bias, ssd_a_log, ssd_d, ssd_norm_w, rwkv_mu, rwkv_w0, rwkv_w2, rwkv_a0, rwkv_a2, rwkv_g2, rwkv_k_k, rwkv_k_a, rwkv_r_k, rwkv_ln_w, rwkv_ln_b, hgrn_lb, hgrn_norm_w, w_br_ssd, w_br_rwkv, w_br_hgrn, w_out, ln1_g, ln1_b, router_w, router_bias, exp_w_gate, exp_w_up, exp_w_down, ln2_g, ln2_b):
    batch, seq, d = x.shape
    t = batch * seq
    h, h_bf = _layernorm(x.reshape(t, d), ln_in_g, ln_in_b)
    lsm = jax.nn.softmax(hgrn_lb.astype(F32), axis=0)
    lower_bounds = jnp.cumsum(lsm, axis=0) - lsm[0]
    w_all = _pack_in_weights(w_in)
    for l in range(DEPTH):
        z, xbc, dt_raw, f_rwkv, f_hgrn, gates = [
            _project(h_bf, w_all, l, seg, F32 if seg == "dt" else BF16)
            for seg in ("z", "xbc", "dt", "rwkv", "hgrn", "gates")]
        y_a = _ssd(z, xbc, dt_raw, ssd_conv_w[l], ssd_conv_b[l], ssd_dt_bias[l], ssd_a_log[l], ssd_d[l],
                   ssd_norm_w[l], batch)
        y_b = _rwkv(f_rwkv, rwkv_mu[l], rwkv_w0[l], rwkv_w2[l], rwkv_a0[l], rwkv_a2[l], rwkv_g2[l],
                    rwkv_k_k[l], rwkv_k_a[l], rwkv_r_k[l].reshape(-1), rwkv_ln_w[l], rwkv_ln_b[l], batch)
        y_c = _hgrn(f_hgrn, lower_bounds[l], hgrn_norm_w[l], batch)
        h = _merge(y_a, y_b, y_c, gates, h, w_br_ssd[l], w_br_rwkv[l], w_br_hgrn[l], w_out[l],
                   ln1_g[l], ln1_b[l])
        gates_t = _router(h, router_w, router_bias)
        h = _moe(h, gates_t, exp_w_gate, exp_w_up, exp_w_down, l, ln2_g[l], ln2_b[l])
        h_bf = h.astype(BF16)
    return h.reshape(batch, seq, d)
```

```python
import functools
import math

import jax
import jax.numpy as jnp
from jax import lax
from jax.experimental import pallas as pl
from jax.experimental.pallas import tpu as pltpu

F32 = jnp.float32
BF16 = jnp.bfloat16

D_MODEL = 1024
DEPTH = 2
CHUNK = 64
STEP_CHUNKS = 4
STEP_ROWS = STEP_CHUNKS * CHUNK
LANES = 128

SSD_HEADS = 16
SSD_HEADDIM = 64
SSD_GROUPS = 2
SSD_STATE = 128
SSD_CONV = 4
SSD_XBC = D_MODEL + 2 * SSD_GROUPS * SSD_STATE

RWKV_HEADS = 16
RWKV_HEAD = 64
RWKV_COLS = 3 * D_MODEL + 64 + 64 + 128
RWKV_GN_EPS = 64e-5

HGRN_HEADS = 8
HGRN_EXPAND = 128

IN_SIZES = (D_MODEL, SSD_XBC, SSD_HEADS, RWKV_COLS, 4 * D_MODEL, 3 * D_MODEL)

N_EXPERTS = 16
N_GROUPS = 4
EXPERTS_PER_GROUP = 4
D_EXPERT = 512

DN_ALPHA = (2 * DEPTH) ** 0.25
LN_EPS = 1e-5
RMS_EPS = 1e-6

VMEM_LIMIT = 56 * 1024 * 1024

_NN = (((1,), (0,)), ((), ()))
_NT = (((1,), (1,)), ((), ()))
_TN = (((0,), (0,)), ((), ()))


def _dot(a, b, dims=_NN):
    return lax.dot_general(a, b, dims, preferred_element_type=F32)


def _bdot(a, b, dims=_NN):
    return _dot(a.astype(BF16), b.astype(BF16), dims)


def _split3(x):
    x1 = x.astype(BF16)
    r1 = x - x1.astype(F32)
    x2 = r1.astype(BF16)
    x3 = (r1 - x2.astype(F32)).astype(BF16)
    return x1, x2, x3


def _sel_r(x, sel, pieces=3):
    n = x.shape[0]
    if pieces == 1:
        return _bdot(x, sel)
    y = _dot(jnp.concatenate(_split3(x)[:pieces], axis=0), sel)
    out = y[:n] + y[n:2 * n]
    return out + y[2 * n:] if pieces == 3 else out


def _chunk_cumsum(x):
    row = lax.broadcasted_iota(jnp.int32, x.shape, 0) & (CHUNK - 1)
    shift = 1
    while shift < CHUNK:
        x = x + jnp.where(row >= shift, pltpu.roll(x, shift, 0), 0.0)
        shift *= 2
    return x


def _sigmoid(x):
    return 1.0 / (1.0 + jnp.exp(-x))


def _silu(x):
    return x * _sigmoid(x)


def _softplus(x):
    return jnp.maximum(x, 0.0) + jnp.log1p(jnp.exp(-jnp.abs(x)))


def _layernorm_rows(x, g, b):
    mu = jnp.mean(x, axis=-1, keepdims=True)
    xc = x - mu
    var = jnp.mean(xc * xc, axis=-1, keepdims=True)
    return xc * lax.rsqrt(var + LN_EPS) * g + b


def _tri(n, strict=False):
    r = lax.broadcasted_iota(jnp.int32, (n, n), 0)
    c = lax.broadcasted_iota(jnp.int32, (n, n), 1)
    return (c < r) if strict else (c <= r)


def _cparams(sem):
    return pltpu.CompilerParams(dimension_semantics=sem, vmem_limit_bytes=VMEM_LIMIT)


def _full(shape):
    return pl.BlockSpec(shape, lambda *_: (0,) * len(shape))


def _ln_kernel(x_ref, g_ref, b_ref, o_ref, ob_ref):
    y = _layernorm_rows(x_ref[...], g_ref[...], b_ref[...])
    o_ref[...] = y
    ob_ref[...] = y.astype(BF16)


def _layernorm(x, g, b, tm=512):
    t, d = x.shape
    row = pl.BlockSpec((tm, d), lambda i: (i, 0))
    return pl.pallas_call(
        _ln_kernel,
        grid=(t // tm,),
        in_specs=[row, _full((1, d)), _full((1, d))],
        out_specs=[row, row],
        out_shape=[jax.ShapeDtypeStruct((t, d), F32), jax.ShapeDtypeStruct((t, d), BF16)],
        compiler_params=_cparams(("parallel",)),
        name="layernorm_in",
    )(x, g.reshape(1, d), b.reshape(1, d))


def _proj_kernel(x_ref, w_ref, o_ref):
    o_ref[...] = _dot(x_ref[...], w_ref[0]).astype(o_ref.dtype)


_IN_LAYOUT = (("hgrn", 4, 4 * D_MODEL, 2048), ("z", 0, D_MODEL, 1024), ("pad0", None, 1024, 1024),
              ("xbc", 1, SSD_XBC, SSD_XBC), ("gates", 5, 3 * D_MODEL, SSD_XBC), ("pad1", None, 896, 128),
              ("rwkv", 3, RWKV_COLS, RWKV_COLS // 2), ("dt", 2, LANES, LANES))


def _in_layout_offsets():
    offs, acc = {}, 0
    for name, _, width, tn in _IN_LAYOUT:
        assert acc % tn == 0 and width % tn == 0, name
        offs[name] = acc
        acc += width
    return offs, acc


def _pack_in_weights(w_in):
    src, acc = [], 0
    for s in IN_SIZES:
        src.append((acc, s))
        acc += s
    parts = []
    for _, idx, width, _ in _IN_LAYOUT:
        if idx is None:
            parts.append(jnp.zeros(w_in.shape[:2] + (width,), w_in.dtype))
            continue
        start, size = src[idx]
        part = w_in[:, :, start:start + size]
        if size < width:
            part = jnp.pad(part, ((0, 0), (0, 0), (0, width - size)))
        parts.append(part)
    return jnp.concatenate(parts, axis=-1).astype(BF16)


def _project(x_bf, w_all, layer, seg, out_dtype=BF16, tm=1024):
    t, k = x_bf.shape
    tm = min(tm, t)
    offs, _ = _in_layout_offsets()
    n, tn = next((width, tn) for name, _, width, tn in _IN_LAYOUT if name == seg)
    first = offs[seg] // tn
    return pl.pallas_call(
        _proj_kernel,
        grid=(n // tn, t // tm),
        in_specs=[pl.BlockSpec((tm, k), lambda j, i: (i, 0)),
                  pl.BlockSpec((1, k, tn), lambda j, i: (layer, 0, first + j))],
        out_specs=pl.BlockSpec((tm, tn), lambda j, i: (i, j)),
        out_shape=jax.ShapeDtypeStruct((t, n), out_dtype),
        compiler_params=_cparams(("parallel", "parallel")),
        name=f"proj_{seg}",
    )(x_bf, w_all)


def _head_expand(n_heads, width):
    h = lax.broadcasted_iota(jnp.int32, (LANES, n_heads * width), 0)
    c = lax.broadcasted_iota(jnp.int32, (LANES, n_heads * width), 1)
    return (c // width == h).astype(BF16)


def _ssd_kernel(z_ref, xbc_ref, dt_ref, cw_ref, cb_ref, dtb_ref, a_ref, dsk_ref, nw_ref, e_ref,
                o_ref, pad_ref, st_ref):
    c = pl.program_id(1)
    n_pairs = SSD_HEADS // 2

    @pl.when(c == 0)
    def _():
        pad_ref[0:8, :] = jnp.zeros((8, SSD_XBC), F32)
        st_ref[...] = jnp.zeros(st_ref.shape, F32)

    pad_ref[8:8 + STEP_ROWS, :] = xbc_ref[...].astype(F32)
    padded = pad_ref[...]
    acc = cb_ref[...] + cw_ref[SSD_CONV - 1:SSD_CONV, :] * padded[8:, :]
    for shift in range(1, SSD_CONV):
        j = SSD_CONV - 1 - shift
        acc = acc + cw_ref[j:j + 1, :] * pltpu.roll(padded, shift, 0)[8:, :]
    pad_ref[0:8, :] = pad_ref[STEP_ROWS:STEP_ROWS + 8, :]
    xbc = _silu(acc)
    xs = xbc[:, :D_MODEL]

    e_mat = e_ref[...]
    dt = _softplus(dt_ref[...] + dtb_ref[...])
    acs = _chunk_cumsum(dt * a_ref[...])
    acs_all = _sel_r(acs, e_mat)
    xs_dt_all = xs * _sel_r(dt, e_mat)

    li = lax.broadcasted_iota(jnp.int32, (CHUNK, D_MODEL), 0)
    si = lax.broadcasted_iota(jnp.int32, (CHUNK, D_MODEL), 1) & (SSD_HEADDIM - 1)
    lane = lax.broadcasted_iota(jnp.int32, (CHUNK, LANES), 1)
    m0 = lane < SSD_HEADDIM

    pairs = range(n_pairs)
    sls = [slice(j * LANES, (j + 1) * LANES) for j in pairs]
    grp = [j // (n_pairs // SSD_GROUPS) for j in pairs]
    st = [st_ref[j] for j in pairs]
    y_rows = []
    for sub in range(STEP_CHUNKS):
        rs = slice(sub * CHUNK, (sub + 1) * CHUNK)
        acs_e, xs_dt = acs_all[rs], xs_dt_all[rs]
        acs_row = jnp.sum(jnp.where(li == si, acs_e, 0.0), axis=0, keepdims=True)
        decay = jnp.exp(jnp.where(si <= li, acs_e - acs_row, -jnp.inf))
        exp_acs = jnp.exp(acs_e)
        acs_last = acs_e[CHUNK - 1:CHUNK, :]
        to_end = jnp.exp(acs_last - acs_e)
        exp_last = jnp.exp(acs_last)
        bm = [xbc[rs, D_MODEL + g * SSD_STATE:D_MODEL + (g + 1) * SSD_STATE].astype(BF16)
              for g in range(SSD_GROUPS)]
        cm = [xbc[rs, D_MODEL + (SSD_GROUPS + g) * SSD_STATE:D_MODEL + (SSD_GROUPS + g + 1) * SSD_STATE
                  ].astype(BF16) for g in range(SSD_GROUPS)]
        cb2 = [_dot(cm[g], jnp.concatenate([bm[g], bm[g]], axis=0), _NT) for g in range(SSD_GROUPS)]
        xp = [xs_dt[:, sl] for sl in sls]
        xbd = [jnp.concatenate([jnp.where(m0, x, 0.0), jnp.where(m0, 0.0, x)], axis=0) for x in xp]
        y_parts = [_bdot(cb2[grp[j]] * decay[:, sls[j]], xbd[j]) + _bdot(cm[grp[j]], st[j]) * exp_acs[:, sls[j]]
                   for j in pairs]
        upd = [_bdot(bm[grp[j]], xp[j] * to_end[:, sls[j]], _TN) for j in pairs]
        st = [st[j] * exp_last[:, sls[j]] + upd[j] for j in pairs]
        y_rows.append(jnp.concatenate(y_parts, axis=1))
    for j in pairs:
        st_ref[j] = st[j]
    y = jnp.concatenate(y_rows, axis=0) + xs * dsk_ref[...]
    y = y * _silu(z_ref[...].astype(F32))
    gw = D_MODEL // SSD_GROUPS
    for g in range(SSD_GROUPS):
        yg = y[:, g * gw:(g + 1) * gw]
        ms = jnp.mean(yg * yg, axis=-1, keepdims=True)
        o_ref[:, g * gw:(g + 1) * gw] = (yg * lax.rsqrt(ms + RMS_EPS) * nw_ref[:, g * gw:(g + 1) * gw]
                                         ).astype(o_ref.dtype)


def _ssd(z, xbc, dt_raw, conv_w, conv_b, dt_bias, a_log, d_skip, norm_w, batch):
    t = z.shape[0]
    nc = t // batch // STEP_ROWS
    pad16 = lambda v: jnp.pad(v.astype(F32), (0, LANES - SSD_HEADS)).reshape(1, LANES)
    row = lambda w: pl.BlockSpec((STEP_ROWS, w), lambda b, c: (b * nc + c, 0))
    return pl.pallas_call(
        _ssd_kernel,
        grid=(batch, nc),
        in_specs=[row(D_MODEL), row(SSD_XBC), row(LANES), _full((SSD_CONV, SSD_XBC)), _full((1, SSD_XBC)),
                  _full((1, LANES)), _full((1, LANES)), _full((1, D_MODEL)), _full((1, D_MODEL)),
                  _full((LANES, D_MODEL))],
        out_specs=row(D_MODEL),
        out_shape=jax.ShapeDtypeStruct((t, D_MODEL), BF16),
        scratch_shapes=[pltpu.VMEM((STEP_ROWS + 8, SSD_XBC), F32),
                        pltpu.VMEM((SSD_HEADS // 2, SSD_STATE, LANES), F32)],
        compiler_params=_cparams(("parallel", "arbitrary")),
        name="ssd",
    )(z, xbc, dt_raw, conv_w, conv_b.reshape(1, -1), pad16(dt_bias), pad16(-jnp.exp(a_log.astype(F32))),
      jnp.repeat(d_skip.astype(F32), SSD_HEADDIM).reshape(1, -1), norm_w.reshape(1, -1),
      _head_expand(SSD_HEADS, SSD_HEADDIM))


def _hgrn_kernel(f_ref, lb_ref, nw_ref, o_ref, st_ref):
    c = pl.program_id(1)

    @pl.when(c == 0)
    def _():
        st_ref[...] = jnp.zeros(st_ref.shape, F32)

    d = D_MODEL
    lb = lb_ref[...]
    q = _silu(f_ref[:, 0:d].astype(F32))
    forget = lb + (1.0 - lb) * _sigmoid(f_ref[:, d:2 * d].astype(F32))
    k = 1.0 - forget
    bc_all = _chunk_cumsum(jnp.log(forget))
    qd_all = q * jnp.exp(bc_all)
    causal = _tri(CHUNK)
    heads = range(HGRN_HEADS)
    sls = [slice(h * HGRN_EXPAND, (h + 1) * HGRN_EXPAND) for h in heads]
    st = [st_ref[h] for h in heads]
    for sub in range(STEP_CHUNKS):
        rs = slice(sub * CHUNK, (sub + 1) * CHUNK)
        bc, qs, ks, qd = bc_all[rs], q[rs], k[rs], qd_all[rs]
        mid = bc[CHUNK // 2:CHUNK // 2 + 1, :]
        last = bc[CHUNK - 1:CHUNK, :]
        qe = qs * jnp.exp(bc - mid)
        ke = ks * jnp.exp(mid - bc)
        kd = ks * jnp.exp(last - bc)
        w_last = jnp.exp(last)
        vb = [f_ref[rs, 2 * d + h * HGRN_EXPAND:2 * d + (h + 1) * HGRN_EXPAND].astype(BF16) for h in heads]
        att = [jnp.where(causal, _bdot(qe[:, sl], ke[:, sl], _NT), 0.0) for sl in sls]
        o = [_bdot(att[h], vb[h]) + _bdot(qd[:, sls[h]], st[h], _NT) for h in heads]
        upd = [_bdot(vb[h], kd[:, sls[h]], _TN) for h in heads]
        st = [st[h] * w_last[:, sls[h]] + upd[h] for h in heads]
        for h in heads:
            oh = o[h] * lax.rsqrt(jnp.mean(o[h] * o[h], axis=-1, keepdims=True) + RMS_EPS) * nw_ref[...]
            gate = _sigmoid(f_ref[rs, 3 * d + h * HGRN_EXPAND:3 * d + (h + 1) * HGRN_EXPAND].astype(F32))
            o_ref[rs, sls[h]] = (oh * gate).astype(o_ref.dtype)
    for h in heads:
        st_ref[h] = st[h]


def _hgrn(feat, lb, norm_w, batch):
    t = feat.shape[0]
    nc = t // batch // STEP_ROWS
    row = lambda w: pl.BlockSpec((STEP_ROWS, w), lambda b, c: (b * nc + c, 0))
    return pl.pallas_call(
        _hgrn_kernel,
        grid=(batch, nc),
        in_specs=[row(4 * D_MODEL), _full((1, D_MODEL)), _full((1, HGRN_EXPAND))],
        out_specs=row(D_MODEL),
        out_shape=jax.ShapeDtypeStruct((t, D_MODEL), BF16),
        scratch_shapes=[pltpu.VMEM((HGRN_HEADS, HGRN_EXPAND, HGRN_EXPAND), F32)],
        compiler_params=_cparams(("parallel", "arbitrary")),
        name="hgrn2",
    )(feat, lb.reshape(1, -1), norm_w.reshape(1, -1))


def _rwkv_kernel(f_ref, mu_ref, w0_ref, w2_ref, a0_ref, a2_ref, g2_ref, kk_ref, ka_ref, rk_ref, lnw_ref,
                 lnb_ref, e_ref, et_ref, o_ref, pad_ref, st_ref, y_ref):
    c = pl.program_id(1)
    d = D_MODEL
    n_pairs = RWKV_HEADS // 2

    @pl.when(c == 0)
    def _():
        pad_ref[0:8, :] = jnp.zeros((8, RWKV_COLS), F32)
        st_ref[...] = jnp.zeros(st_ref.shape, F32)

    cur = f_ref[...].astype(F32)
    pad_ref[8:8 + STEP_ROWS, :] = cur
    prev = pltpu.roll(pad_ref[...], 1, 0)[8:, :]
    pad_ref[0:8, :] = pad_ref[STEP_ROWS:STEP_ROWS + 8, :]
    x = cur + (prev - cur) * mu_ref[...]
    r, k, v = x[:, 0:d], x[:, d:2 * d], x[:, 2 * d:3 * d]
    lora_in = x[:, 3 * d:3 * d + LANES]
    xg = x[:, 3 * d + LANES:3 * d + 2 * LANES]

    e_mat, et_mat = e_ref[...], et_ref[...]
    w = -_softplus(-(w0_ref[...] + _bdot(jnp.tanh(lora_in), w2_ref[...]))) - 0.5
    ld = -jnp.exp(w)
    a = _sigmoid(a0_ref[...] + _bdot(lora_in, a2_ref[...]))
    gate = _bdot(_sigmoid(xg), g2_ref[...])
    kk = k * kk_ref[...]
    nrm = jnp.maximum(jnp.sqrt(_sel_r(kk * kk, et_mat, 1)), 1e-12)
    kk = kk * _sel_r(1.0 / nrm, e_mat, 2)
    k2 = k * (1.0 + (a - 1.0) * ka_ref[...])
    cw = _chunk_cumsum(ld)
    e_neg = jnp.exp(-cw)
    a_t = -kk * jnp.exp(cw - ld)
    r_t = r * jnp.exp(cw)
    b_t = kk * a * e_neg
    k_t = k2 * e_neg

    lane = lax.broadcasted_iota(jnp.int32, (CHUNK, LANES), 1)
    m0 = lane < RWKV_HEAD
    row2 = lax.broadcasted_iota(jnp.int32, (LANES, LANES), 0)
    col2 = lax.broadcasted_iota(jnp.int32, (LANES, LANES), 1)
    same = (row2 < CHUNK) == (col2 < RWKV_HEAD)
    row4 = lax.broadcasted_iota(jnp.int32, (2 * LANES, LANES), 0)
    col4 = lax.broadcasted_iota(jnp.int32, (2 * LANES, LANES), 1)
    top4 = (row4 & (LANES - 1)) < CHUNK
    tt, ss = row4 & (CHUNK - 1), col4 & (CHUNK - 1)
    bd4 = (top4 == (col4 < RWKV_HEAD)) & ((ss < tt) | ((row4 >= LANES) & (ss == tt)))

    def stack(p):
        return jnp.concatenate([jnp.where(m0, p, 0.0), jnp.where(m0, 0.0, p)], axis=0)

    pairs = range(n_pairs)
    sls = [slice(j * LANES, (j + 1) * LANES) for j in pairs]
    bf = lambda xs: [x.astype(BF16) for x in xs]
    mt = [st_ref[j] for j in pairs]
    for sub in range(STEP_CHUNKS):
        rs = slice(sub * CHUNK, (sub + 1) * CHUNK)
        w_last = jnp.exp(cw[(sub + 1) * CHUNK - 1:(sub + 1) * CHUNK, :])
        s12 = bf(jnp.concatenate([stack(a_t[rs, sl]), stack(r_t[rs, sl])], axis=0) for sl in sls)
        vst = bf(stack(v[rs, sl]) for sl in sls)
        qmat = bf(jnp.concatenate([b_t[rs, sl], k_t[rs, sl]], axis=0) for sl in sls)
        mtb = bf(mt)
        gh = [_dot(s12[j], qmat[j], _NT) for j in pairs]
        ghr = [pltpu.roll(x, CHUNK, 1) for x in gh]
        n_ab, a_rb, akrk = [], [], []
        for j in pairs:
            diag = jnp.where(bd4, jnp.where(top4, gh[j], ghr[j]), 0.0)
            n_ab.append(diag[:LANES])
            a_rb.append(diag[LANES:].astype(BF16))
            akrk.append(jnp.where(bd4, jnp.where(top4, ghr[j], gh[j]), 0.0).astype(BF16))
        sm = [_dot(s12[j], mtb[j], _NT) for j in pairs]
        av = [_dot(akrk[j], vst[j]) for j in pairs]
        xm = [sm[j][:LANES] + av[j][:LANES] for j in pairs]
        tp = n_ab
        pwf = [_bdot(x, x) for x in n_ab]
        n_steps = int(math.log2(CHUNK)) - 1
        for step in range(n_steps):
            pw = bf(pwf)
            if step + 1 < n_steps:
                both = [_dot(jnp.concatenate([pw[j], tp[j].astype(BF16)], axis=0), pw[j]) for j in pairs]
                tp = [tp[j] + pwf[j] + both[j][LANES:] for j in pairs]
                pwf = [both[j][:LANES] for j in pairs]
            else:
                tp = [tp[j] + pwf[j] + _bdot(tp[j], pw[j]) for j in pairs]
        u = [xm[j] + _bdot(tp[j], xm[j]) for j in pairs]
        yst = [sm[j][LANES:] + av[j][LANES:] + _bdot(a_rb[j], u[j]) for j in pairs]
        for j in pairs:
            y_ref[rs, sls[j]] = yst[j][:CHUNK] + yst[j][CHUNK:]
        uv = [jnp.concatenate([u[j][:CHUNK] + u[j][CHUNK:], v[rs, sls[j]]], axis=0) for j in pairs]
        upd = [jnp.where(same, _bdot(uv[j], qmat[j], _TN), 0.0) for j in pairs]
        mt = [(mt[j] + upd[j]) * w_last[:, sls[j]] for j in pairs]
    for j in pairs:
        st_ref[j] = mt[j]

    y = y_ref[...]
    inv_n = 1.0 / RWKV_HEAD
    yc = y - _sel_r(_sel_r(y, et_mat, 1) * inv_n, e_mat, 2)
    rs = lax.rsqrt(_sel_r(yc * yc, et_mat, 1) * inv_n + RWKV_GN_EPS)
    yn = yc * _sel_r(rs, e_mat, 2) * lnw_ref[...] + lnb_ref[...]
    bonus = _sel_r(_sel_r(r * k2 * rk_ref[...], et_mat, 1), e_mat, 2) * v
    o_ref[...] = ((yn + bonus) * gate).astype(o_ref.dtype)


def _rwkv(feat, mu, w0, w2, a0, a2, g2, k_k, k_a, r_k, ln_w, ln_b, batch):
    t = feat.shape[0]
    nc = t // batch // STEP_ROWS
    d = D_MODEL
    row = lambda w: pl.BlockSpec((STEP_ROWS, w), lambda b, c: (b * nc + c, 0))
    vec = lambda v: v.astype(F32).reshape(1, -1)
    w2p = jnp.concatenate([w2, jnp.zeros_like(w2)], axis=0).astype(BF16)
    a2p = jnp.concatenate([jnp.zeros_like(a2), a2], axis=0).astype(BF16)
    e_mat = _head_expand(RWKV_HEADS, RWKV_HEAD)
    return pl.pallas_call(
        _rwkv_kernel,
        grid=(batch, nc),
        in_specs=[row(RWKV_COLS), _full((1, RWKV_COLS)), _full((1, d)), _full((LANES, d)), _full((1, d)),
                  _full((LANES, d)), _full((LANES, d)), _full((1, d)), _full((1, d)), _full((1, d)),
                  _full((1, d)), _full((1, d)), _full((LANES, d)), _full((d, LANES))],
        out_specs=row(d),
        out_shape=jax.ShapeDtypeStruct((t, d), BF16),
        scratch_shapes=[pltpu.VMEM((STEP_ROWS + 8, RWKV_COLS), F32),
                        pltpu.VMEM((RWKV_HEADS // 2, LANES, LANES), F32),
                        pltpu.VMEM((STEP_ROWS, d), F32)],
        compiler_params=_cparams(("parallel", "arbitrary")),
        name="rwkv7",
    )(feat, vec(mu), vec(w0), w2p, vec(a0), a2p, g2.astype(BF16), vec(k_k), vec(k_a), vec(r_k), vec(ln_w),
      vec(ln_b), e_mat, e_mat.T)


def _merge_kernel(ya_ref, yb_ref, yc_ref, g_ref, h_ref, wa_ref, wb_ref, wc_ref, wo_ref, lg_ref, lb_ref, o_ref):
    d = D_MODEL
    gate = lambda j: _sigmoid(g_ref[:, j * d:(j + 1) * d].astype(F32))
    m = (gate(0) * _dot(ya_ref[...], wa_ref[...]) + gate(1) * _dot(yb_ref[...], wb_ref[...])
         + gate(2) * _dot(yc_ref[...], wc_ref[...]))
    hn = DN_ALPHA * h_ref[...] + _bdot(m, wo_ref[...])
    o_ref[...] = _layernorm_rows(hn, lg_ref[...], lb_ref[...])


def _merge(ya, yb, yc, gates, h, wa, wb, wc, wo, ln_g, ln_b, tm=512):
    t, d = h.shape
    tm = min(tm, t)
    row = lambda w: pl.BlockSpec((tm, w), lambda i: (i, 0))
    return pl.pallas_call(
        _merge_kernel,
        grid=(t // tm,),
        in_specs=[row(d), row(d), row(d), row(3 * d), row(d)] + [_full((d, d))] * 4 + [_full((1, d))] * 2,
        out_specs=row(d),
        out_shape=jax.ShapeDtypeStruct((t, d), F32),
        compiler_params=_cparams(("parallel",)),
        name="merge_ln",
    )(ya, yb, yc, gates, h, wa.astype(BF16), wb.astype(BF16), wc.astype(BF16), wo.astype(BF16),
      ln_g.reshape(1, d), ln_b.reshape(1, d))


def _router_kernel(h_ref, wt_ref, bias_ref, o_ref):
    logits = lax.dot_general(wt_ref[...], h_ref[...], _NT, precision=lax.Precision.HIGHEST,
                             preferred_element_type=F32)
    mx = jnp.max(logits, axis=0, keepdims=True)
    ex = jnp.exp(logits - mx)
    probs = ex / jnp.sum(ex, axis=0, keepdims=True)
    sel = probs + bias_ref[...]
    rows = [sel[e:e + 1, :] for e in range(N_EXPERTS)]
    prow = [probs[e:e + 1, :] for e in range(N_EXPERTS)]
    gscore = []
    for g in range(N_GROUPS):
        m = rows[g * EXPERTS_PER_GROUP:(g + 1) * EXPERTS_PER_GROUP]
        best = None
        for i in range(EXPERTS_PER_GROUP):
            for j in range(i + 1, EXPERTS_PER_GROUP):
                s = m[i] + m[j]
                best = s if best is None else jnp.maximum(best, s)
        gscore.append(best)
    chosen = []
    for g in range(N_GROUPS):
        ok = None
        for g2 in range(N_GROUPS):
            if g2 == g:
                continue
            t = (gscore[g] > gscore[g2]) if g2 < g else (gscore[g] >= gscore[g2])
            ok = t if ok is None else (ok & t)
        chosen.append(ok)
    picked = []
    for e in range(N_EXPERTS):
        g = e // EXPERTS_PER_GROUP
        rank = None
        for e2 in range(g * EXPERTS_PER_GROUP, (g + 1) * EXPERTS_PER_GROUP):
            if e2 == e:
                continue
            ahead = (rows[e2] >= rows[e]) if e2 < e else (rows[e2] > rows[e])
            ahead = ahead.astype(F32)
            rank = ahead if rank is None else rank + ahead
        picked.append(jnp.where(chosen[g] & (rank < 2.0), prow[e], 0.0))
    tot = picked[0]
    for e in range(1, N_EXPERTS):
        tot = tot + picked[e]
    inv = 1.0 / tot
    for e in range(N_EXPERTS):
        o_ref[e:e + 1, :] = picked[e] * inv


def _router(h, router_w, router_bias, tm=512):
    t, d = h.shape
    tm = min(tm, t)
    return pl.pallas_call(
        _router_kernel,
        grid=(t // tm,),
        in_specs=[pl.BlockSpec((tm, d), lambda i: (i, 0)), _full((N_EXPERTS, d)), _full((N_EXPERTS, 1))],
        out_specs=pl.BlockSpec((N_EXPERTS, tm), lambda i: (0, i)),
        out_shape=jax.ShapeDtypeStruct((N_EXPERTS, t), F32),
        compiler_params=_cparams(("parallel",)),
        name="router",
    )(h, router_w.T, router_bias.reshape(N_EXPERTS, 1).astype(F32))


MOE_TILE = 256
MOE_ISSUE_UNROLLS = (32, 4, 1)
_PAIRS = [(a, b) for a in range(EXPERTS_PER_GROUP) for b in range(a + 1, EXPERTS_PER_GROUP)]
N_CLASSES = N_GROUPS * len(_PAIRS)


def _moe_num_tiles(t):
    return -(-(t + N_CLASSES * (MOE_TILE - 1)) // MOE_TILE)


def _route_meta(gates_t):
    e_n, t = gates_t.shape
    i32 = jnp.int32
    mask = gates_t > 0
    eidx = lax.broadcasted_iota(i32, (e_n, t), 0)
    e0 = jnp.minimum(jnp.min(jnp.where(mask, eidx, e_n), axis=0), e_n - 1)
    e1 = jnp.maximum(jnp.max(jnp.where(mask, eidx, -1), axis=0), e0)
    grp = e0 // EXPERTS_PER_GROUP
    a = e0 % EXPERTS_PER_GROUP
    b = jnp.where(e1 // EXPERTS_PER_GROUP == grp, e1 % EXPERTS_PER_GROUP, a)
    b = jnp.where(b == a, (a + 1) % EXPERTS_PER_GROUP, b)
    lo, hi = jnp.minimum(a, b), jnp.maximum(a, b)
    n_pairs = len(_PAIRS)
    cls = grp * n_pairs + (lo * (2 * EXPERTS_PER_GROUP - 1 - lo)) // 2 + (hi - lo - 1)
    g_lo = jnp.take_along_axis(gates_t, (grp * EXPERTS_PER_GROUP + lo)[None, :], axis=0)[0]
    g_hi = jnp.take_along_axis(gates_t, (grp * EXPERTS_PER_GROUP + hi)[None, :], axis=0)[0]
    onehot = (cls[None, :] == jnp.arange(N_CLASSES, dtype=i32)[:, None]).astype(i32)
    csum = jnp.cumsum(onehot, axis=1)
    cnt = csum[:, -1]
    rank = jnp.sum(onehot * (csum - 1), axis=0)
    ptiles = (cnt + MOE_TILE - 1) // MOE_TILE
    tile_end = jnp.cumsum(ptiles)
    tile_off = tile_end - ptiles
    pos = jnp.take(tile_off, cls) * MOE_TILE + rank
    n_tiles = _moe_num_tiles(t)
    as_i32 = lambda g: lax.bitcast_convert_type(g.astype(F32), i32)
    per_token = jnp.stack([jnp.arange(t, dtype=i32), as_i32(g_lo), as_i32(g_hi)], axis=1)
    per_row = jnp.zeros((n_tiles * MOE_TILE, 3), i32).at[pos].set(per_token, unique_indices=True)
    row_token = per_row[:, 0]
    row_gates = lax.bitcast_convert_type(per_row[:, 1:3], F32)
    tid = jnp.arange(n_tiles, dtype=i32)
    tcls = jnp.minimum(jnp.sum((tid[:, None] >= tile_end[None, :]).astype(i32), axis=1), N_CLASSES - 1)
    n_valid = jnp.clip(jnp.take(cnt, tcls) - (tid - jnp.take(tile_off, tcls)) * MOE_TILE, 0, MOE_TILE)
    n_valid = jnp.where(tid < tile_end[-1], n_valid, 0).astype(i32)
    cls_lo = jnp.array([g * EXPERTS_PER_GROUP + p[0] for g in range(N_GROUPS) for p in _PAIRS], i32)
    cls_hi = jnp.array([g * EXPERTS_PER_GROUP + p[1] for g in range(N_GROUPS) for p in _PAIRS], i32)
    return row_token, jnp.take(cls_lo, tcls), jnp.take(cls_hi, tcls), n_valid, row_gates


def _moe_kernel(rt_ref, ea_ref, eb_ref, nv_ref, h_hbm, rg_ref, wga_ref, wua_ref, wda_ref, wgb_ref, wub_ref,
                wdb_ref, lg_ref, lb_ref, out_hbm, xbuf, obuf, wup_buf, wdn_buf, sem_in, sem_out):
    i = pl.program_id(0)
    n = pl.num_programs(0)
    slot = lax.rem(i, 2)

    def rows_in(row, s, k, rows=1):
        return pltpu.make_async_copy(h_hbm.at[pl.ds(row, rows)], xbuf.at[s, pl.ds(k, rows)], sem_in.at[s])

    def rows_out(row, s, k, rows=1):
        return pltpu.make_async_copy(obuf.at[s, pl.ds(k, rows)], out_hbm.at[pl.ds(row, rows)], sem_out.at[s])

    def issue(copy_at, tile, count):
        base = tile * MOE_TILE
        done = 0
        for unroll in MOE_ISSUE_UNROLLS:
            def group(q, c, unroll=unroll, done=done):
                for r in range(unroll):
                    copy_at(base, done + q * unroll + r).start()
                return c
            groups = lax.shift_right_logical(count - done, int(math.log2(unroll)))
            lax.fori_loop(0, groups, group, 0)
            done = done + groups * unroll

    def wait_rows(copies, count):
        rows = MOE_TILE
        while rows >= 1:
            @pl.when((count & rows) != 0)
            def _(rows=rows):
                copies(0, 0, rows).wait()
            rows //= 2

    def start_gather(tile, s):
        issue(lambda base, k: rows_in(rt_ref[base + k], s, k), tile, nv_ref[tile])

    @pl.when(i == 0)
    def _():
        xbuf[...] = jnp.zeros(xbuf.shape, F32)
        start_gather(0, 0)

    @pl.when(i + 1 < n)
    def _():
        start_gather(i + 1, 1 - slot)

    wait_rows(lambda row, k, rows: rows_in(row, slot, k, rows), nv_ref[i])

    @pl.when(i >= 2)
    def _():
        wait_rows(lambda row, k, rows: rows_out(row, slot, k, rows), nv_ref[i - 2])

    @pl.when(nv_ref[i] > 0)
    def _():
        prev = jnp.maximum(i - 1, 0)
        for which, e_ref, (g_ref, u_ref, d_ref) in ((0, ea_ref, (wga_ref, wua_ref, wda_ref)),
                                                    (1, eb_ref, (wgb_ref, wub_ref, wdb_ref))):
            @pl.when((i == 0) | (e_ref[i] != e_ref[prev]))
            def _(which=which, g_ref=g_ref, u_ref=u_ref, d_ref=d_ref):
                wup_buf[2 * which] = g_ref[0, 0].astype(BF16)
                wup_buf[2 * which + 1] = u_ref[0, 0].astype(BF16)
                wdn_buf[which] = d_ref[0, 0].astype(BF16)

        x = xbuf[slot]
        xb = x.astype(BF16)
        rg = rg_ref[...]
        act_a = (_silu(_dot(xb, wup_buf[0])) * _dot(xb, wup_buf[1])).astype(BF16)
        act_b = (_silu(_dot(xb, wup_buf[2])) * _dot(xb, wup_buf[3])).astype(BF16)
        moe = rg[:, 0:1] * _dot(act_a, wdn_buf[0]) + rg[:, 1:2] * _dot(act_b, wdn_buf[1])
        obuf[slot] = _layernorm_rows(DN_ALPHA * x + moe, lg_ref[...], lb_ref[...])
        issue(lambda base, k: rows_out(rt_ref[base + k], slot, k), i, nv_ref[i])

    @pl.when(i == n - 1)
    def _():
        @pl.when(i >= 1)
        def _():
            wait_rows(lambda row, k, rows: rows_out(row, 1 - slot, k, rows), nv_ref[i - 1])
        wait_rows(lambda row, k, rows: rows_out(row, slot, k, rows), nv_ref[i])


def _moe(h, gates_t, wg, wu, wd, layer, ln_g, ln_b):
    t, d = h.shape
    n_tiles = _moe_num_tiles(t)
    row_token, exp_a, exp_b, n_valid, row_gates = _route_meta(gates_t)
    w_up = lambda which: pl.BlockSpec((1, 1, d, D_EXPERT),
                                      lambda i, rt, ea, eb, nv: (layer, (ea, eb)[which][i], 0, 0))
    w_dn = lambda which: pl.BlockSpec((1, 1, D_EXPERT, d),
                                      lambda i, rt, ea, eb, nv: (layer, (ea, eb)[which][i], 0, 0))
    const = lambda shape: pl.BlockSpec(shape, lambda i, *_: (0,) * len(shape))
    grid_spec = pltpu.PrefetchScalarGridSpec(
        num_scalar_prefetch=4,
        grid=(n_tiles,),
        in_specs=[pl.BlockSpec(memory_space=pl.ANY),
                  pl.BlockSpec((MOE_TILE, 2), lambda i, *_: (i, 0)),
                  w_up(0), w_up(0), w_dn(0), w_up(1), w_up(1), w_dn(1), const((1, d)), const((1, d))],
        out_specs=pl.BlockSpec(memory_space=pl.ANY),
        scratch_shapes=[pltpu.VMEM((2, MOE_TILE, d), F32), pltpu.VMEM((2, MOE_TILE, d), F32),
                        pltpu.VMEM((4, d, D_EXPERT), BF16), pltpu.VMEM((2, D_EXPERT, d), BF16),
                        pltpu.SemaphoreType.DMA((2,)), pltpu.SemaphoreType.DMA((2,))],
    )
    return pl.pallas_call(
        _moe_kernel,
        grid_spec=grid_spec,
        out_shape=jax.ShapeDtypeStruct((t, d), F32),
        compiler_params=_cparams(("arbitrary",)),
        name="experts_ln",
    )(row_token, exp_a, exp_b, n_valid, h, row_gates, wg, wu, wd, wg, wu, wd, ln_g.reshape(1, d), ln_b.reshape(1, d))


def kernel(x, ln_in_g, ln_in_b, w_in, ssd_conv_w, ssd_conv_b, ssd_dt_bias, ssd_a_log, ssd_d, ssd_norm_w, rwkv_mu, rwkv_w0, rwkv_w2, rwkv_a0, rwkv_a2, rwkv_g2, rwkv_k_k, rwkv_k_a, rwkv_r_k, rwkv_ln_w, rwkv_ln_b, hgrn_lb, hgrn_norm_w, w_br_ssd, w_br_rwkv, w_br_hgrn, w_out, ln1_g, ln1_b, router_w, router_bias, exp_w_gate, exp_w_up, exp_w_down, ln2_g, ln2_b):
    batch, seq, d = x.shape
    t = batch * seq
    h, h_bf = _layernorm(x.reshape(t, d), ln_in_g, ln_in_b)
    lsm = jax.nn.softmax(hgrn_lb.astype(F32), axis=0)
    lower_bounds = jnp.cumsum(lsm, axis=0) - lsm[0]
    w_all = _pack_in_weights(w_in)
    for l in range(DEPTH):
        z, xbc, dt_raw, f_rwkv, f_hgrn, gates = [
            _project(h_bf, w_all, l, seg, F32 if seg == "dt" else BF16)
            for seg in ("z", "xbc", "dt", "rwkv", "hgrn", "gates")]
        y_a = _ssd(z, xbc, dt_raw, ssd_conv_w[l], ssd_conv_b[l], ssd_dt_bias[l], ssd_a_log[l], ssd_d[l],
                   ssd_norm_w[l], batch)
        y_b = _rwkv(f_rwkv, rwkv_mu[l], rwkv_w0[l], rwkv_w2[l], rwkv_a0[l], rwkv_a2[l], rwkv_g2[l],
                    rwkv_k_k[l], rwkv_k_a[l], rwkv_r_k[l].reshape(-1), rwkv_ln_w[l], rwkv_ln_b[l], batch)
        y_c = _hgrn(f_hgrn, lower_bounds[l], hgrn_norm_w[l], batch)
        h = _merge(y_a, y_b, y_c, gates, h, w_br_ssd[l], w_br_rwkv[l], w_br_hgrn[l], w_out[l],
                   ln1_g[l], ln1_b[l])
        gates_t = _router(h, router_w, router_bias)
        h = _moe(h, gates_t, exp_w_gate, exp_w_up, exp_w_down, l, ln2_g[l], ln2_b[l])
        h_bf = h.astype(BF16)
    return h.reshape(batch, seq, d)
```

```python
import math

import jax
import jax.numpy as jnp
from jax import lax
from jax.experimental import pallas as pl
from jax.experimental.pallas import tpu as pltpu

F32 = jnp.float32
BF16 = jnp.bfloat16

D_MODEL = 1024
DEPTH = 2
CHUNK = 64
STEP_CHUNKS = 4
STEP_ROWS = STEP_CHUNKS * CHUNK
LANES = 128
HIST = 8

SSD_HEADS = 16
SSD_HEADDIM = 64
SSD_GROUPS = 2
SSD_STATE = 128
SSD_CONV = 4
SSD_XBC = D_MODEL + 2 * SSD_GROUPS * SSD_STATE

RWKV_HEADS = 16
RWKV_HEAD = 64
RWKV_COLS = 3 * D_MODEL + 64 + 64 + 128
RWKV_GN_EPS = 64e-5

HGRN_HEADS = 8
HGRN_EXPAND = 128

IN_SIZES = (D_MODEL, SSD_XBC, SSD_HEADS, RWKV_COLS, 4 * D_MODEL, 3 * D_MODEL)

N_EXPERTS = 16
N_GROUPS = 4
EXPERTS_PER_GROUP = 4
D_EXPERT = 512

DN_ALPHA = (2 * DEPTH) ** 0.25
LN_EPS = 1e-5
RMS_EPS = 1e-6

VMEM_LIMIT = 56 * 1024 * 1024

_NN = (((1,), (0,)), ((), ()))
_NT = (((1,), (1,)), ((), ()))
_TN = (((0,), (0,)), ((), ()))


def _dot(a, b, dims=_NN):
    return lax.dot_general(a, b, dims, preferred_element_type=F32)


def _bdot(a, b, dims=_NN):
    return _dot(a.astype(BF16), b.astype(BF16), dims)


def _split3(x):
    x1 = x.astype(BF16)
    r1 = x - x1.astype(F32)
    x2 = r1.astype(BF16)
    x3 = (r1 - x2.astype(F32)).astype(BF16)
    return x1, x2, x3


def _sel_r(x, sel, pieces=3):
    n = x.shape[0]
    if pieces == 1:
        return _bdot(x, sel)
    y = _dot(jnp.concatenate(_split3(x)[:pieces], axis=0), sel)
    out = y[:n] + y[n:2 * n]
    return out + y[2 * n:] if pieces == 3 else out


def _chunk_cumsum(x):
    row = lax.broadcasted_iota(jnp.int32, x.shape, 0) & (CHUNK - 1)
    shift = 1
    while shift < CHUNK:
        x = x + jnp.where(row >= shift, pltpu.roll(x, shift, 0), 0.0)
        shift *= 2
    return x


def _sigmoid(x):
    return 1.0 / (1.0 + jnp.exp(-x))


def _silu(x):
    return x * _sigmoid(x)


def _softplus(x):
    return jnp.maximum(x, 0.0) + jnp.log1p(jnp.exp(-jnp.abs(x)))


def _layernorm_rows(x, g, b):
    mu = jnp.mean(x, axis=-1, keepdims=True)
    xc = x - mu
    var = jnp.mean(xc * xc, axis=-1, keepdims=True)
    return xc * lax.rsqrt(var + LN_EPS) * g + b


def _tri(n, strict=False):
    r = lax.broadcasted_iota(jnp.int32, (n, n), 0)
    c = lax.broadcasted_iota(jnp.int32, (n, n), 1)
    return (c < r) if strict else (c <= r)


def _cparams(sem):
    return pltpu.CompilerParams(dimension_semantics=sem, vmem_limit_bytes=VMEM_LIMIT)


def _full(shape):
    return pl.BlockSpec(shape, lambda *_: (0,) * len(shape))


def _ln_kernel(x_ref, g_ref, b_ref, o_ref, ob_ref):
    y = _layernorm_rows(x_ref[...], g_ref[...], b_ref[...])
    o_ref[...] = y
    ob_ref[...] = y.astype(BF16)


def _layernorm(x, g, b, tm=512):
    t, d = x.shape
    row = pl.BlockSpec((tm, d), lambda i: (i, 0))
    return pl.pallas_call(
        _ln_kernel,
        grid=(t // tm,),
        in_specs=[row, _full((1, d)), _full((1, d))],
        out_specs=[row, row],
        out_shape=[jax.ShapeDtypeStruct((t, d), F32), jax.ShapeDtypeStruct((t, d), BF16)],
        compiler_params=_cparams(("parallel",)),
        name="layernorm_in",
    )(x, g.reshape(1, d), b.reshape(1, d))


def _proj_kernel(x_ref, w_ref, o_ref):
    o_ref[...] = _dot(x_ref[...], w_ref[0]).astype(o_ref.dtype)


_IN_LAYOUT = (("hgrn", 4, 4 * D_MODEL, 2048), ("z", 0, D_MODEL, 1024), ("pad0", None, 1024, 1024),
              ("xbc", 1, SSD_XBC, SSD_XBC), ("gates", 5, 3 * D_MODEL, SSD_XBC), ("pad1", None, 896, 128),
              ("rwkv", 3, RWKV_COLS, RWKV_COLS // 2), ("dt", 2, LANES, LANES))


def _in_layout_offsets():
    offs, acc = {}, 0
    for name, _, width, tn in _IN_LAYOUT:
        assert acc % tn == 0 and width % tn == 0, name
        offs[name] = acc
        acc += width
    return offs, acc


def _pack_in_weights(w_in):
    src, acc = [], 0
    for s in IN_SIZES:
        src.append((acc, s))
        acc += s
    parts = []
    for _, idx, width, _ in _IN_LAYOUT:
        if idx is None:
            parts.append(jnp.zeros(w_in.shape[:2] + (width,), w_in.dtype))
            continue
        start, size = src[idx]
        part = w_in[:, :, start:start + size]
        if size < width:
            part = jnp.pad(part, ((0, 0), (0, 0), (0, width - size)))
        parts.append(part)
    return jnp.concatenate(parts, axis=-1).astype(BF16)


def _project(x_bf, w_all, layer, seg, out_dtype=BF16, tm=1024):
    t, k = x_bf.shape
    tm = min(tm, t)
    offs, _ = _in_layout_offsets()
    n, tn = next((width, tn) for name, _, width, tn in _IN_LAYOUT if name == seg)
    first = offs[seg] // tn
    return pl.pallas_call(
        _proj_kernel,
        grid=(n // tn, t // tm),
        in_specs=[pl.BlockSpec((tm, k), lambda j, i: (i, 0)),
                  pl.BlockSpec((1, k, tn), lambda j, i: (layer, 0, first + j))],
        out_specs=pl.BlockSpec((tm, tn), lambda j, i: (i, j)),
        out_shape=jax.ShapeDtypeStruct((t, n), out_dtype),
        compiler_params=_cparams(("parallel", "parallel")),
        name=f"proj_{seg}",
    )(x_bf, w_all)


def _head_expand(n_heads, width):
    h = lax.broadcasted_iota(jnp.int32, (LANES, n_heads * width), 0)
    c = lax.broadcasted_iota(jnp.int32, (LANES, n_heads * width), 1)
    return (c // width == h).astype(BF16)


def _ssd_kernel(z_ref, xbc_ref, dt_ref, cw_ref, cb_ref, dtb_ref, a_ref, dsk_ref, nw_ref, e_ref,
                o_ref, pad_ref, st_ref):
    c = pl.program_id(1)
    n_pairs = SSD_HEADS // 2

    @pl.when(c == 0)
    def _():
        pad_ref[0:HIST, :] = jnp.zeros((HIST, SSD_XBC), F32)
        st_ref[...] = jnp.zeros(st_ref.shape, F32)

    pad_ref[HIST:HIST + STEP_ROWS, :] = xbc_ref[...].astype(F32)
    padded = pad_ref[...]
    acc = cb_ref[...] + cw_ref[SSD_CONV - 1:SSD_CONV, :] * padded[HIST:, :]
    for shift in range(1, SSD_CONV):
        j = SSD_CONV - 1 - shift
        acc = acc + cw_ref[j:j + 1, :] * pltpu.roll(padded, shift, 0)[HIST:, :]
    pad_ref[0:HIST, :] = pad_ref[STEP_ROWS:STEP_ROWS + HIST, :]
    xbc = _silu(acc)
    xs = xbc[:, :D_MODEL]

    e_mat = e_ref[...]
    dt = _softplus(dt_ref[...] + dtb_ref[...])
    acs = _chunk_cumsum(dt * a_ref[...])
    acs_all = _sel_r(acs, e_mat)
    xs_dt_all = xs * _sel_r(dt, e_mat)

    li = lax.broadcasted_iota(jnp.int32, (CHUNK, D_MODEL), 0)
    si = lax.broadcasted_iota(jnp.int32, (CHUNK, D_MODEL), 1) & (SSD_HEADDIM - 1)
    lane = lax.broadcasted_iota(jnp.int32, (CHUNK, LANES), 1)
    m0 = lane < SSD_HEADDIM

    pairs = range(n_pairs)
    sls = [slice(j * LANES, (j + 1) * LANES) for j in pairs]
    grp = [j // (n_pairs // SSD_GROUPS) for j in pairs]
    st = [st_ref[j] for j in pairs]
    y_rows = []
    for sub in range(STEP_CHUNKS):
        rs = slice(sub * CHUNK, (sub + 1) * CHUNK)
        acs_e, xs_dt = acs_all[rs], xs_dt_all[rs]
        acs_row = jnp.sum(jnp.where(li == si, acs_e, 0.0), axis=0, keepdims=True)
        decay = jnp.exp(jnp.where(si <= li, acs_e - acs_row, -jnp.inf))
        exp_acs = jnp.exp(acs_e)
        acs_last = acs_e[CHUNK - 1:CHUNK, :]
        to_end = jnp.exp(acs_last - acs_e)
        exp_last = jnp.exp(acs_last)
        bm = [xbc[rs, D_MODEL + g * SSD_STATE:D_MODEL + (g + 1) * SSD_STATE].astype(BF16)
              for g in range(SSD_GROUPS)]
        cm = [xbc[rs, D_MODEL + (SSD_GROUPS + g) * SSD_STATE:D_MODEL + (SSD_GROUPS + g + 1) * SSD_STATE
                  ].astype(BF16) for g in range(SSD_GROUPS)]
        cb2 = [_dot(cm[g], jnp.concatenate([bm[g], bm[g]], axis=0), _NT) for g in range(SSD_GROUPS)]
        xp = [xs_dt[:, sl] for sl in sls]
        xbd = [jnp.concatenate([jnp.where(m0, x, 0.0), jnp.where(m0, 0.0, x)], axis=0) for x in xp]
        y_parts = [_bdot(cb2[grp[j]] * decay[:, sls[j]], xbd[j]) + _bdot(cm[grp[j]], st[j]) * exp_acs[:, sls[j]]
                   for j in pairs]
        upd = [_bdot(bm[grp[j]], xp[j] * to_end[:, sls[j]], _TN) for j in pairs]
        st = [st[j] * exp_last[:, sls[j]] + upd[j] for j in pairs]
        y_rows.append(jnp.concatenate(y_parts, axis=1))
    for j in pairs:
        st_ref[j] = st[j]
    y = jnp.concatenate(y_rows, axis=0) + xs * dsk_ref[...]
    y = y * _silu(z_ref[...].astype(F32))
    gw = D_MODEL // SSD_GROUPS
    for g in range(SSD_GROUPS):
        yg = y[:, g * gw:(g + 1) * gw]
        ms = jnp.mean(yg * yg, axis=-1, keepdims=True)
        o_ref[:, g * gw:(g + 1) * gw] = (yg * lax.rsqrt(ms + RMS_EPS) * nw_ref[:, g * gw:(g + 1) * gw]
                                         ).astype(o_ref.dtype)


def _ssd(z, xbc, dt_raw, conv_w, conv_b, dt_bias, a_log, d_skip, norm_w, batch):
    t = z.shape[0]
    nc = t // batch // STEP_ROWS
    pad16 = lambda v: jnp.pad(v.astype(F32), (0, LANES - SSD_HEADS)).reshape(1, LANES)
    row = lambda w: pl.BlockSpec((STEP_ROWS, w), lambda b, c: (b * nc + c, 0))
    return pl.pallas_call(
        _ssd_kernel,
        grid=(batch, nc),
        in_specs=[row(D_MODEL), row(SSD_XBC), row(LANES), _full((SSD_CONV, SSD_XBC)), _full((1, SSD_XBC)),
                  _full((1, LANES)), _full((1, LANES)), _full((1, D_MODEL)), _full((1, D_MODEL)),
                  _full((LANES, D_MODEL))],
        out_specs=row(D_MODEL),
        out_shape=jax.ShapeDtypeStruct((t, D_MODEL), BF16),
        scratch_shapes=[pltpu.VMEM((STEP_ROWS + HIST, SSD_XBC), F32),
                        pltpu.VMEM((SSD_HEADS // 2, SSD_STATE, LANES), F32)],
        compiler_params=_cparams(("parallel", "arbitrary")),
        name="ssd",
    )(z, xbc, dt_raw, conv_w, conv_b.reshape(1, -1), pad16(dt_bias), pad16(-jnp.exp(a_log.astype(F32))),
      jnp.repeat(d_skip.astype(F32), SSD_HEADDIM).reshape(1, -1), norm_w.reshape(1, -1),
      _head_expand(SSD_HEADS, SSD_HEADDIM))


def _hgrn_kernel(f_ref, lb_ref, nw_ref, o_ref, st_ref):
    c = pl.program_id(1)

    @pl.when(c == 0)
    def _():
        st_ref[...] = jnp.zeros(st_ref.shape, F32)

    d = D_MODEL
    lb = lb_ref[...]
    q = _silu(f_ref[:, 0:d].astype(F32))
    forget = lb + (1.0 - lb) * _sigmoid(f_ref[:, d:2 * d].astype(F32))
    k = 1.0 - forget
    bc_all = _chunk_cumsum(jnp.log(forget))
    qd_all = q * jnp.exp(bc_all)
    causal = _tri(CHUNK)
    heads = range(HGRN_HEADS)
    sls = [slice(h * HGRN_EXPAND, (h + 1) * HGRN_EXPAND) for h in heads]
    st = [st_ref[h] for h in heads]
    for sub in range(STEP_CHUNKS):
        rs = slice(sub * CHUNK, (sub + 1) * CHUNK)
        bc, qs, ks, qd = bc_all[rs], q[rs], k[rs], qd_all[rs]
        mid = bc[CHUNK // 2:CHUNK // 2 + 1, :]
        last = bc[CHUNK - 1:CHUNK, :]
        qe = qs * jnp.exp(bc - mid)
        ke = ks * jnp.exp(mid - bc)
        kd = ks * jnp.exp(last - bc)
        w_last = jnp.exp(last)
        vb = [f_ref[rs, 2 * d + h * HGRN_EXPAND:2 * d + (h + 1) * HGRN_EXPAND].astype(BF16) for h in heads]
        att = [jnp.where(causal, _bdot(qe[:, sl], ke[:, sl], _NT), 0.0) for sl in sls]
        o = [_bdot(att[h], vb[h]) + _bdot(qd[:, sls[h]], st[h], _NT) for h in heads]
        upd = [_bdot(vb[h], kd[:, sls[h]], _TN) for h in heads]
        st = [st[h] * w_last[:, sls[h]] + upd[h] for h in heads]
        for h in heads:
            oh = o[h] * lax.rsqrt(jnp.mean(o[h] * o[h], axis=-1, keepdims=True) + RMS_EPS) * nw_ref[...]
            gate = _sigmoid(f_ref[rs, 3 * d + h * HGRN_EXPAND:3 * d + (h + 1) * HGRN_EXPAND].astype(F32))
            o_ref[rs, sls[h]] = (oh * gate).astype(o_ref.dtype)
    for h in heads:
        st_ref[h] = st[h]


def _hgrn(feat, lb, norm_w, batch):
    t = feat.shape[0]
    nc = t // batch // STEP_ROWS
    row = lambda w: pl.BlockSpec((STEP_ROWS, w), lambda b, c: (b * nc + c, 0))
    return pl.pallas_call(
        _hgrn_kernel,
        grid=(batch, nc),
        in_specs=[row(4 * D_MODEL), _full((1, D_MODEL)), _full((1, HGRN_EXPAND))],
        out_specs=row(D_MODEL),
        out_shape=jax.ShapeDtypeStruct((t, D_MODEL), BF16),
        scratch_shapes=[pltpu.VMEM((HGRN_HEADS, HGRN_EXPAND, HGRN_EXPAND), F32)],
        compiler_params=_cparams(("parallel", "arbitrary")),
        name="hgrn2",
    )(feat, lb.reshape(1, -1), norm_w.reshape(1, -1))


def _rwkv_kernel(f_ref, mu_ref, w0_ref, w2_ref, a0_ref, a2_ref, g2_ref, kk_ref, ka_ref, rk_ref, lnw_ref,
                 lnb_ref, e_ref, et_ref, o_ref, pad_ref, st_ref, y_ref):
    c = pl.program_id(1)
    d = D_MODEL
    n_pairs = RWKV_HEADS // 2

    @pl.when(c == 0)
    def _():
        pad_ref[0:HIST, :] = jnp.zeros((HIST, RWKV_COLS), F32)
        st_ref[...] = jnp.zeros(st_ref.shape, F32)

    cur = f_ref[...].astype(F32)
    pad_ref[HIST:HIST + STEP_ROWS, :] = cur
    prev = pltpu.roll(pad_ref[...], 1, 0)[HIST:, :]
    pad_ref[0:HIST, :] = pad_ref[STEP_ROWS:STEP_ROWS + HIST, :]
    x = cur + (prev - cur) * mu_ref[...]
    r, k, v = x[:, 0:d], x[:, d:2 * d], x[:, 2 * d:3 * d]
    lora_in = x[:, 3 * d:3 * d + LANES]
    xg = x[:, 3 * d + LANES:3 * d + 2 * LANES]

    e_mat, et_mat = e_ref[...], et_ref[...]
    w = -_softplus(-(w0_ref[...] + _bdot(jnp.tanh(lora_in), w2_ref[...]))) - 0.5
    ld = -jnp.exp(w)
    a = _sigmoid(a0_ref[...] + _bdot(lora_in, a2_ref[...]))
    gate = _bdot(_sigmoid(xg), g2_ref[...])
    kk = k * kk_ref[...]
    nrm = jnp.maximum(jnp.sqrt(_sel_r(kk * kk, et_mat, 1)), 1e-12)
    kk = kk * _sel_r(1.0 / nrm, e_mat, 2)
    k2 = k * (1.0 + (a - 1.0) * ka_ref[...])
    cw = _chunk_cumsum(ld)
    e_neg = jnp.exp(-cw)
    a_t = -kk * jnp.exp(cw - ld)
    r_t = r * jnp.exp(cw)
    b_t = kk * a * e_neg
    k_t = k2 * e_neg

    lane = lax.broadcasted_iota(jnp.int32, (CHUNK, LANES), 1)
    m0 = lane < RWKV_HEAD
    row2 = lax.broadcasted_iota(jnp.int32, (LANES, LANES), 0)
    col2 = lax.broadcasted_iota(jnp.int32, (LANES, LANES), 1)
    same = (row2 < CHUNK) == (col2 < RWKV_HEAD)
    row4 = lax.broadcasted_iota(jnp.int32, (2 * LANES, LANES), 0)
    col4 = lax.broadcasted_iota(jnp.int32, (2 * LANES, LANES), 1)
    top4 = (row4 & (LANES - 1)) < CHUNK
    tt, ss = row4 & (CHUNK - 1), col4 & (CHUNK - 1)
    bd4 = (top4 == (col4 < RWKV_HEAD)) & ((ss < tt) | ((row4 >= LANES) & (ss == tt)))

    def stack(p):
        return jnp.concatenate([jnp.where(m0, p, 0.0), jnp.where(m0, 0.0, p)], axis=0)

    pairs = range(n_pairs)
    sls = [slice(j * LANES, (j + 1) * LANES) for j in pairs]
    bf = lambda xs: [x.astype(BF16) for x in xs]
    mt = [st_ref[j] for j in pairs]
    for sub in range(STEP_CHUNKS):
        rs = slice(sub * CHUNK, (sub + 1) * CHUNK)
        w_last = jnp.exp(cw[(sub + 1) * CHUNK - 1:(sub + 1) * CHUNK, :])
        s12 = bf(jnp.concatenate([stack(a_t[rs, sl]), stack(r_t[rs, sl])], axis=0) for sl in sls)
        vst = bf(stack(v[rs, sl]) for sl in sls)
        qmat = bf(jnp.concatenate([b_t[rs, sl], k_t[rs, sl]], axis=0) for sl in sls)
        mtb = bf(mt)
        gh = [_dot(s12[j], qmat[j], _NT) for j in pairs]
        ghr = [pltpu.roll(x, CHUNK, 1) for x in gh]
        n_ab, a_rb, akrk = [], [], []
        for j in pairs:
            diag = jnp.where(bd4, jnp.where(top4, gh[j], ghr[j]), 0.0)
            n_ab.append(diag[:LANES])
            a_rb.append(diag[LANES:].astype(BF16))
            akrk.append(jnp.where(bd4, jnp.where(top4, ghr[j], gh[j]), 0.0).astype(BF16))
        sm = [_dot(s12[j], mtb[j], _NT) for j in pairs]
        av = [_dot(akrk[j], vst[j]) for j in pairs]
        xm = [sm[j][:LANES] + av[j][:LANES] for j in pairs]
        tp = n_ab
        pwf = [_bdot(x, x) for x in n_ab]
        n_steps = int(math.log2(CHUNK)) - 1
        for step in range(n_steps):
            pw = bf(pwf)
            if step + 1 < n_steps:
                both = [_dot(jnp.concatenate([pw[j], tp[j].astype(BF16)], axis=0), pw[j]) for j in pairs]
                tp = [tp[j] + pwf[j] + both[j][LANES:] for j in pairs]
                pwf = [both[j][:LANES] for j in pairs]
            else:
                tp = [tp[j] + pwf[j] + _bdot(tp[j], pw[j]) for j in pairs]
        u = [xm[j] + _bdot(tp[j], xm[j]) for j in pairs]
        yst = [sm[j][LANES:] + av[j][LANES:] + _bdot(a_rb[j], u[j]) for j in pairs]
        for j in pairs:
            y_ref[rs, sls[j]] = yst[j][:CHUNK] + yst[j][CHUNK:]
        uv = [jnp.concatenate([u[j][:CHUNK] + u[j][CHUNK:], v[rs, sls[j]]], axis=0) for j in pairs]
        upd = [jnp.where(same, _bdot(uv[j], qmat[j], _TN), 0.0) for j in pairs]
        mt = [(mt[j] + upd[j]) * w_last[:, sls[j]] for j in pairs]
    for j in pairs:
        st_ref[j] = mt[j]

    y = y_ref[...]
    inv_n = 1.0 / RWKV_HEAD
    yc = y - _sel_r(_sel_r(y, et_mat, 1) * inv_n, e_mat, 2)
    rs = lax.rsqrt(_sel_r(yc * yc, et_mat, 1) * inv_n + RWKV_GN_EPS)
    yn = yc * _sel_r(rs, e_mat, 2) * lnw_ref[...] + lnb_ref[...]
    bonus = _sel_r(_sel_r(r * k2 * rk_ref[...], et_mat, 1), e_mat, 2) * v
    o_ref[...] = ((yn + bonus) * gate).astype(o_ref.dtype)


def _rwkv(feat, mu, w0, w2, a0, a2, g2, k_k, k_a, r_k, ln_w, ln_b, batch):
    t = feat.shape[0]
    nc = t // batch // STEP_ROWS
    d = D_MODEL
    row = lambda w: pl.BlockSpec((STEP_ROWS, w), lambda b, c: (b * nc + c, 0))
    vec = lambda v: v.astype(F32).reshape(1, -1)
    w2p = jnp.concatenate([w2, jnp.zeros_like(w2)], axis=0).astype(BF16)
    a2p = jnp.concatenate([jnp.zeros_like(a2), a2], axis=0).astype(BF16)
    e_mat = _head_expand(RWKV_HEADS, RWKV_HEAD)
    return pl.pallas_call(
        _rwkv_kernel,
        grid=(batch, nc),
        in_specs=[row(RWKV_COLS), _full((1, RWKV_COLS)), _full((1, d)), _full((LANES, d)), _full((1, d)),
                  _full((LANES, d)), _full((LANES, d)), _full((1, d)), _full((1, d)), _full((1, d)),
                  _full((1, d)), _full((1, d)), _full((LANES, d)), _full((d, LANES))],
        out_specs=row(d),
        out_shape=jax.ShapeDtypeStruct((t, d), BF16),
        scratch_shapes=[pltpu.VMEM((STEP_ROWS + HIST, RWKV_COLS), F32),
                        pltpu.VMEM((RWKV_HEADS // 2, LANES, LANES), F32),
                        pltpu.VMEM((STEP_ROWS, d), F32)],
        compiler_params=_cparams(("parallel", "arbitrary")),
        name="rwkv7",
    )(feat, vec(mu), vec(w0), w2p, vec(a0), a2p, g2.astype(BF16), vec(k_k), vec(k_a), vec(r_k), vec(ln_w),
      vec(ln_b), e_mat, e_mat.T)


def _merge_kernel(ya_ref, yb_ref, yc_ref, g_ref, h_ref, wa_ref, wb_ref, wc_ref, wo_ref, lg_ref, lb_ref,
                  rw_ref, rb_ref, o_ref, gt_ref):
    d = D_MODEL
    gate = lambda j: _sigmoid(g_ref[:, j * d:(j + 1) * d].astype(F32))
    m = (gate(0) * _dot(ya_ref[...], wa_ref[...]) + gate(1) * _dot(yb_ref[...], wb_ref[...])
         + gate(2) * _dot(yc_ref[...], wc_ref[...]))
    hn = DN_ALPHA * h_ref[...] + _bdot(m, wo_ref[...])
    y = _layernorm_rows(hn, lg_ref[...], lb_ref[...])
    o_ref[...] = y
    _route(y, rw_ref, rb_ref, gt_ref)


def _merge(ya, yb, yc, gates, h, wa, wb, wc, wo, ln_g, ln_b, router_w, router_bias, tm=512):
    t, d = h.shape
    tm = min(tm, t)
    row = lambda w: pl.BlockSpec((tm, w), lambda i: (i, 0))
    return pl.pallas_call(
        _merge_kernel,
        grid=(t // tm,),
        in_specs=[row(d), row(d), row(d), row(3 * d), row(d)] + [_full((d, d))] * 4 + [_full((1, d))] * 2
        + [_full((N_EXPERTS, d)), _full((N_EXPERTS, 1))],
        out_specs=[row(d), pl.BlockSpec((N_EXPERTS, tm), lambda i: (0, i))],
        out_shape=[jax.ShapeDtypeStruct((t, d), F32), jax.ShapeDtypeStruct((N_EXPERTS, t), F32)],
        compiler_params=_cparams(("parallel",)),
        name="merge_ln_route",
    )(ya, yb, yc, gates, h, wa.astype(BF16), wb.astype(BF16), wc.astype(BF16), wo.astype(BF16),
      ln_g.reshape(1, d), ln_b.reshape(1, d), router_w.T, router_bias.reshape(N_EXPERTS, 1).astype(F32))


def _route(h, wt_ref, bias_ref, o_ref):
    logits = lax.dot_general(wt_ref[...], h, _NT, precision=lax.Precision.HIGHEST,
                             preferred_element_type=F32)
    mx = jnp.max(logits, axis=0, keepdims=True)
    ex = jnp.exp(logits - mx)
    probs = ex / jnp.sum(ex, axis=0, keepdims=True)
    sel = probs + bias_ref[...]
    rows = [sel[e:e + 1, :] for e in range(N_EXPERTS)]
    prow = [probs[e:e + 1, :] for e in range(N_EXPERTS)]
    gscore = []
    for g in range(N_GROUPS):
        m = rows[g * EXPERTS_PER_GROUP:(g + 1) * EXPERTS_PER_GROUP]
        best = None
        for i in range(EXPERTS_PER_GROUP):
            for j in range(i + 1, EXPERTS_PER_GROUP):
                s = m[i] + m[j]
                best = s if best is None else jnp.maximum(best, s)
        gscore.append(best)
    chosen = []
    for g in range(N_GROUPS):
        ok = None
        for g2 in range(N_GROUPS):
            if g2 == g:
                continue
            t = (gscore[g] > gscore[g2]) if g2 < g else (gscore[g] >= gscore[g2])
            ok = t if ok is None else (ok & t)
        chosen.append(ok)
    picked = []
    for e in range(N_EXPERTS):
        g = e // EXPERTS_PER_GROUP
        rank = None
        for e2 in range(g * EXPERTS_PER_GROUP, (g + 1) * EXPERTS_PER_GROUP):
            if e2 == e:
                continue
            ahead = (rows[e2] >= rows[e]) if e2 < e else (rows[e2] > rows[e])
            ahead = ahead.astype(F32)
            rank = ahead if rank is None else rank + ahead
        picked.append(jnp.where(chosen[g] & (rank < 2.0), prow[e], 0.0))
    tot = picked[0]
    for e in range(1, N_EXPERTS):
        tot = tot + picked[e]
    inv = 1.0 / tot
    for e in range(N_EXPERTS):
        o_ref[e:e + 1, :] = picked[e] * inv


MOE_TILE = 256
MOE_ISSUE_UNROLLS = (32, 4, 1)
_PAIRS = [(a, b) for a in range(EXPERTS_PER_GROUP) for b in range(a + 1, EXPERTS_PER_GROUP)]
N_CLASSES = N_GROUPS * len(_PAIRS)


def _moe_num_tiles(t):
    return -(-(t + N_CLASSES * (MOE_TILE - 1)) // MOE_TILE)


def _route_meta(gates_t):
    e_n, t = gates_t.shape
    i32 = jnp.int32
    mask = gates_t > 0
    eidx = lax.broadcasted_iota(i32, (e_n, t), 0)
    e0 = jnp.minimum(jnp.min(jnp.where(mask, eidx, e_n), axis=0), e_n - 1)
    e1 = jnp.maximum(jnp.max(jnp.where(mask, eidx, -1), axis=0), e0)
    grp = e0 // EXPERTS_PER_GROUP
    a = e0 % EXPERTS_PER_GROUP
    b = jnp.where(e1 // EXPERTS_PER_GROUP == grp, e1 % EXPERTS_PER_GROUP, a)
    b = jnp.where(b == a, (a + 1) % EXPERTS_PER_GROUP, b)
    lo, hi = jnp.minimum(a, b), jnp.maximum(a, b)
    n_pairs = len(_PAIRS)
    cls = grp * n_pairs + (lo * (2 * EXPERTS_PER_GROUP - 1 - lo)) // 2 + (hi - lo - 1)
    g_lo = jnp.take_along_axis(gates_t, (grp * EXPERTS_PER_GROUP + lo)[None, :], axis=0)[0]
    g_hi = jnp.take_along_axis(gates_t, (grp * EXPERTS_PER_GROUP + hi)[None, :], axis=0)[0]
    onehot = (cls[None, :] == jnp.arange(N_CLASSES, dtype=i32)[:, None]).astype(i32)
    csum = jnp.cumsum(onehot, axis=1)
    cnt = csum[:, -1]
    rank = jnp.sum(onehot * (csum - 1), axis=0)
    ptiles = (cnt + MOE_TILE - 1) // MOE_TILE
    tile_end = jnp.cumsum(ptiles)
    tile_off = tile_end - ptiles
    pos = jnp.take(tile_off, cls) * MOE_TILE + rank
    n_tiles = _moe_num_tiles(t)
    as_i32 = lambda g: lax.bitcast_convert_type(g.astype(F32), i32)
    per_token = jnp.stack([jnp.arange(t, dtype=i32), as_i32(g_lo), as_i32(g_hi)], axis=1)
    per_row = jnp.zeros((n_tiles * MOE_TILE, 3), i32).at[pos].set(per_token, unique_indices=True)
    row_token = per_row[:, 0]
    row_gates = lax.bitcast_convert_type(per_row[:, 1:3], F32)
    tid = jnp.arange(n_tiles, dtype=i32)
    tcls = jnp.minimum(jnp.sum((tid[:, None] >= tile_end[None, :]).astype(i32), axis=1), N_CLASSES - 1)
    n_valid = jnp.clip(jnp.take(cnt, tcls) - (tid - jnp.take(tile_off, tcls)) * MOE_TILE, 0, MOE_TILE)
    n_valid = jnp.where(tid < tile_end[-1], n_valid, 0).astype(i32)
    cls_lo = jnp.array([g * EXPERTS_PER_GROUP + p[0] for g in range(N_GROUPS) for p in _PAIRS], i32)
    cls_hi = jnp.array([g * EXPERTS_PER_GROUP + p[1] for g in range(N_GROUPS) for p in _PAIRS], i32)
    return row_token, jnp.take(cls_lo, tcls), jnp.take(cls_hi, tcls), n_valid, row_gates


def _moe_kernel(rt_ref, ea_ref, eb_ref, nv_ref, h_hbm, rg_ref, wga_ref, wua_ref, wda_ref, wgb_ref, wub_ref,
                wdb_ref, lg_ref, lb_ref, out_hbm, xbuf, obuf, wup_buf, wdn_buf, sem_in, sem_out):
    i = pl.program_id(0)
    n = pl.num_programs(0)
    slot = lax.rem(i, 2)

    def rows_in(row, s, k, rows=1):
        return pltpu.make_async_copy(h_hbm.at[pl.ds(row, rows)], xbuf.at[s, pl.ds(k, rows)], sem_in.at[s])

    def rows_out(row, s, k, rows=1):
        return pltpu.make_async_copy(obuf.at[s, pl.ds(k, rows)], out_hbm.at[pl.ds(row, rows)], sem_out.at[s])

    def issue(copy_at, tile, count):
        base = tile * MOE_TILE

        @pl.when(count == MOE_TILE)
        def _():
            for k in range(MOE_TILE):
                copy_at(base, k).start()

        @pl.when(count < MOE_TILE)
        def _():
            done = 0
            for unroll in MOE_ISSUE_UNROLLS:
                def group(q, c, unroll=unroll, done=done):
                    for r in range(unroll):
                        copy_at(base, done + q * unroll + r).start()
                    return c
                groups = lax.shift_right_logical(count - done, int(math.log2(unroll)))
                lax.fori_loop(0, groups, group, 0)
                done = done + groups * unroll

    def wait_rows(copies, count):
        rows = MOE_TILE
        while rows >= 1:
            @pl.when((count & rows) != 0)
            def _(rows=rows):
                copies(0, 0, rows).wait()
            rows //= 2

    def start_gather(tile, s):
        issue(lambda base, k: rows_in(rt_ref[base + k], s, k), tile, nv_ref[tile])

    @pl.when(i == 0)
    def _():
        xbuf[...] = jnp.zeros(xbuf.shape, F32)
        start_gather(0, 0)

    @pl.when(i + 1 < n)
    def _():
        start_gather(i + 1, 1 - slot)

    wait_rows(lambda row, k, rows: rows_in(row, slot, k, rows), nv_ref[i])

    @pl.when(i >= 2)
    def _():
        wait_rows(lambda row, k, rows: rows_out(row, slot, k, rows), nv_ref[i - 2])

    @pl.when(nv_ref[i] > 0)
    def _():
        prev = jnp.maximum(i - 1, 0)
        for which, e_ref, (g_ref, u_ref, d_ref) in ((0, ea_ref, (wga_ref, wua_ref, wda_ref)),
                                                    (1, eb_ref, (wgb_ref, wub_ref, wdb_ref))):
            @pl.when((i == 0) | (e_ref[i] != e_ref[prev]))
            def _(which=which, g_ref=g_ref, u_ref=u_ref, d_ref=d_ref):
                wup_buf[2 * which] = g_ref[0, 0].astype(BF16)
                wup_buf[2 * which + 1] = u_ref[0, 0].astype(BF16)
                wdn_buf[which] = d_ref[0, 0].astype(BF16)

        x = xbuf[slot]
        xb = x.astype(BF16)
        rg = rg_ref[...]
        act_a = (_silu(_dot(xb, wup_buf[0])) * _dot(xb, wup_buf[1])).astype(BF16)
        act_b = (_silu(_dot(xb, wup_buf[2])) * _dot(xb, wup_buf[3])).astype(BF16)
        moe = rg[:, 0:1] * _dot(act_a, wdn_buf[0]) + rg[:, 1:2] * _dot(act_b, wdn_buf[1])
        obuf[slot] = _layernorm_rows(DN_ALPHA * x + moe, lg_ref[...], lb_ref[...])
        issue(lambda base, k: rows_out(rt_ref[base + k], slot, k), i, nv_ref[i])

    @pl.when(i == n - 1)
    def _():
        @pl.when(i >= 1)
        def _():
            wait_rows(lambda row, k, rows: rows_out(row, 1 - slot, k, rows), nv_ref[i - 1])
        wait_rows(lambda row, k, rows: rows_out(row, slot, k, rows), nv_ref[i])


def _moe(h, gates_t, wg, wu, wd, layer, ln_g, ln_b):
    t, d = h.shape
    n_tiles = _moe_num_tiles(t)
    row_token, exp_a, exp_b, n_valid, row_gates = _route_meta(gates_t)
    w_up = lambda which: pl.BlockSpec((1, 1, d, D_EXPERT),
                                      lambda i, rt, ea, eb, nv: (layer, (ea, eb)[which][i], 0, 0))
    w_dn = lambda which: pl.BlockSpec((1, 1, D_EXPERT, d),
                                      lambda i, rt, ea, eb, nv: (layer, (ea, eb)[which][i], 0, 0))
    const = lambda shape: pl.BlockSpec(shape, lambda i, *_: (0,) * len(shape))
    grid_spec = pltpu.PrefetchScalarGridSpec(
        num_scalar_prefetch=4,
        grid=(n_tiles,),
        in_specs=[pl.BlockSpec(memory_space=pl.ANY),
                  pl.BlockSpec((MOE_TILE, 2), lambda i, *_: (i, 0)),
                  w_up(0), w_up(0), w_dn(0), w_up(1), w_up(1), w_dn(1), const((1, d)), const((1, d))],
        out_specs=pl.BlockSpec(memory_space=pl.ANY),
        scratch_shapes=[pltpu.VMEM((2, MOE_TILE, d), F32), pltpu.VMEM((2, MOE_TILE, d), F32),
                        pltpu.VMEM((4, d, D_EXPERT), BF16), pltpu.VMEM((2, D_EXPERT, d), BF16),
                        pltpu.SemaphoreType.DMA((2,)), pltpu.SemaphoreType.DMA((2,))],
    )
    return pl.pallas_call(
        _moe_kernel,
        grid_spec=grid_spec,
        out_shape=jax.ShapeDtypeStruct((t, d), F32),
        compiler_params=_cparams(("arbitrary",)),
        name="experts_ln",
    )(row_token, exp_a, exp_b, n_valid, h, row_gates, wg, wu, wd, wg, wu, wd, ln_g.reshape(1, d), ln_b.reshape(1, d))


def kernel(x, ln_in_g, ln_in_b, w_in, ssd_conv_w, ssd_conv_b, ssd_dt_bias, ssd_a_log, ssd_d, ssd_norm_w, rwkv_mu, rwkv_w0, rwkv_w2, rwkv_a0, rwkv_a2, rwkv_g2, rwkv_k_k, rwkv_k_a, rwkv_r_k, rwkv_ln_w, rwkv_ln_b, hgrn_lb, hgrn_norm_w, w_br_ssd, w_br_rwkv, w_br_hgrn, w_out, ln1_g, ln1_b, router_w, router_bias, exp_w_gate, exp_w_up, exp_w_down, ln2_g, ln2_b):
    batch, seq, d = x.shape
    t = batch * seq
    h, h_bf = _layernorm(x.reshape(t, d), ln_in_g, ln_in_b)
    lsm = jax.nn.softmax(hgrn_lb.astype(F32), axis=0)
    lower_bounds = jnp.cumsum(lsm, axis=0) - lsm[0]
    w_all = _pack_in_weights(w_in)
    for l in range(DEPTH):
        z, xbc, dt_raw, f_rwkv, f_hgrn, gates = [
            _project(h_bf, w_all, l, seg, F32 if seg == "dt" else BF16)
            for seg in ("z", "xbc", "dt", "rwkv", "hgrn", "gates")]
        y_a = _ssd(z, xbc, dt_raw, ssd_conv_w[l], ssd_conv_b[l], ssd_dt_bias[l], ssd_a_log[l], ssd_d[l],
                   ssd_norm_w[l], batch)
        y_b = _rwkv(f_rwkv, rwkv_mu[l], rwkv_w0[l], rwkv_w2[l], rwkv_a0[l], rwkv_a2[l], rwkv_g2[l],
                    rwkv_k_k[l], rwkv_k_a[l], rwkv_r_k[l].reshape(-1), rwkv_ln_w[l], rwkv_ln_b[l], batch)
        y_c = _hgrn(f_hgrn, lower_bounds[l], hgrn_norm_w[l], batch)
        h, gates_t = _merge(y_a, y_b, y_c, gates, h, w_br_ssd[l], w_br_rwkv[l], w_br_hgrn[l], w_out[l],
                            ln1_g[l], ln1_b[l], router_w, router_bias)
        h = _moe(h, gates_t, exp_w_gate, exp_w_up, exp_w_down, l, ln2_g[l], ln2_b[l])
        h_bf = h.astype(BF16)
    return h.reshape(batch, seq, d)
```

```python
import math

import jax
import jax.numpy as jnp
from jax import lax
from jax.experimental import pallas as pl
from jax.experimental.pallas import tpu as pltpu

F32 = jnp.float32
BF16 = jnp.bfloat16

D_MODEL = 1024
DEPTH = 2
CHUNK = 64
STEP_CHUNKS = 4
STEP_ROWS = STEP_CHUNKS * CHUNK
LANES = 128
HIST = 8

SSD_HEADS = 16
SSD_HEADDIM = 64
SSD_GROUPS = 2
SSD_STATE = 128
SSD_CONV = 4
SSD_XBC = D_MODEL + 2 * SSD_GROUPS * SSD_STATE

RWKV_HEADS = 16
RWKV_HEAD = 64
RWKV_COLS = 3 * D_MODEL + 64 + 64 + 128
RWKV_GN_EPS = 64e-5

HGRN_HEADS = 8
HGRN_EXPAND = 128

IN_SIZES = (D_MODEL, SSD_XBC, SSD_HEADS, RWKV_COLS, 4 * D_MODEL, 3 * D_MODEL)

N_EXPERTS = 16
N_GROUPS = 4
EXPERTS_PER_GROUP = 4
D_EXPERT = 512

DN_ALPHA = (2 * DEPTH) ** 0.25
LN_EPS = 1e-5
RMS_EPS = 1e-6

VMEM_LIMIT = 56 * 1024 * 1024

_NN = (((1,), (0,)), ((), ()))
_NT = (((1,), (1,)), ((), ()))
_TN = (((0,), (0,)), ((), ()))


def _dot(a, b, dims=_NN):
    return lax.dot_general(a, b, dims, preferred_element_type=F32)


def _bdot(a, b, dims=_NN):
    return _dot(a.astype(BF16), b.astype(BF16), dims)


def _split3(x):
    x1 = x.astype(BF16)
    r1 = x - x1.astype(F32)
    x2 = r1.astype(BF16)
    x3 = (r1 - x2.astype(F32)).astype(BF16)
    return x1, x2, x3


def _sel_r(x, sel, pieces=3):
    n = x.shape[0]
    if pieces == 1:
        return _bdot(x, sel)
    y = _dot(jnp.concatenate(_split3(x)[:pieces], axis=0), sel)
    out = y[:n] + y[n:2 * n]
    return out + y[2 * n:] if pieces == 3 else out


def _chunk_cumsum(x):
    row = lax.broadcasted_iota(jnp.int32, x.shape, 0) & (CHUNK - 1)
    shift = 1
    while shift < CHUNK:
        x = x + jnp.where(row >= shift, pltpu.roll(x, shift, 0), 0.0)
        shift *= 2
    return x


def _sigmoid(x):
    return 1.0 / (1.0 + jnp.exp(-x))


def _silu(x):
    return x * _sigmoid(x)


def _softplus(x):
    return jnp.maximum(x, 0.0) + jnp.log1p(jnp.exp(-jnp.abs(x)))


def _layernorm_rows(x, g, b):
    mu = jnp.mean(x, axis=-1, keepdims=True)
    xc = x - mu
    var = jnp.mean(xc * xc, axis=-1, keepdims=True)
    return xc * lax.rsqrt(var + LN_EPS) * g + b


def _tri(n, strict=False):
    r = lax.broadcasted_iota(jnp.int32, (n, n), 0)
    c = lax.broadcasted_iota(jnp.int32, (n, n), 1)
    return (c < r) if strict else (c <= r)


def _cparams(sem):
    return pltpu.CompilerParams(dimension_semantics=sem, vmem_limit_bytes=VMEM_LIMIT)


def _full(shape):
    return pl.BlockSpec(shape, lambda *_: (0,) * len(shape))


def _ln_kernel(x_ref, g_ref, b_ref, o_ref, ob_ref):
    y = _layernorm_rows(x_ref[...], g_ref[...], b_ref[...])
    o_ref[...] = y
    ob_ref[...] = y.astype(BF16)


def _layernorm(x, g, b, tm=512):
    t, d = x.shape
    row = pl.BlockSpec((tm, d), lambda i: (i, 0))
    return pl.pallas_call(
        _ln_kernel,
        grid=(t // tm,),
        in_specs=[row, _full((1, d)), _full((1, d))],
        out_specs=[row, row],
        out_shape=[jax.ShapeDtypeStruct((t, d), F32), jax.ShapeDtypeStruct((t, d), BF16)],
        compiler_params=_cparams(("parallel",)),
        name="layernorm_in",
    )(x, g.reshape(1, d), b.reshape(1, d))


def _proj_kernel(x_ref, w_ref, o_ref):
    o_ref[...] = _dot(x_ref[...], w_ref[0]).astype(o_ref.dtype)


_IN_LAYOUT = (("hgrn", 4, 4 * D_MODEL, 2048), ("z", 0, D_MODEL, 1024), ("pad0", None, 1024, 1024),
              ("xbc", 1, SSD_XBC, SSD_XBC), ("gates", 5, 3 * D_MODEL, SSD_XBC), ("pad1", None, 896, 128),
              ("rwkv", 3, RWKV_COLS, RWKV_COLS // 2), ("dt", 2, LANES, LANES))


def _in_layout_offsets():
    offs, acc = {}, 0
    for name, _, width, tn in _IN_LAYOUT:
        assert acc % tn == 0 and width % tn == 0, name
        offs[name] = acc
        acc += width
    return offs, acc


def _pack_in_weights(w_in):
    src, acc = [], 0
    for s in IN_SIZES:
        src.append((acc, s))
        acc += s
    parts = []
    for _, idx, width, _ in _IN_LAYOUT:
        if idx is None:
            parts.append(jnp.zeros(w_in.shape[:2] + (width,), w_in.dtype))
            continue
        start, size = src[idx]
        part = w_in[:, :, start:start + size]
        if size < width:
            part = jnp.pad(part, ((0, 0), (0, 0), (0, width - size)))
        parts.append(part)
    return jnp.concatenate(parts, axis=-1).astype(BF16)


def _project(x_bf, w_all, layer, seg, out_dtype=BF16, tm=1024):
    t, k = x_bf.shape
    tm = min(tm, t)
    offs, _ = _in_layout_offsets()
    n, tn = next((width, tn) for name, _, width, tn in _IN_LAYOUT if name == seg)
    first = offs[seg] // tn
    return pl.pallas_call(
        _proj_kernel,
        grid=(n // tn, t // tm),
        in_specs=[pl.BlockSpec((tm, k), lambda j, i: (i, 0)),
                  pl.BlockSpec((1, k, tn), lambda j, i: (layer, 0, first + j))],
        out_specs=pl.BlockSpec((tm, tn), lambda j, i: (i, j)),
        out_shape=jax.ShapeDtypeStruct((t, n), out_dtype),
        compiler_params=_cparams(("parallel", "parallel")),
        name=f"proj_{seg}",
    )(x_bf, w_all)


def _head_expand(n_heads, width):
    h = lax.broadcasted_iota(jnp.int32, (LANES, n_heads * width), 0)
    c = lax.broadcasted_iota(jnp.int32, (LANES, n_heads * width), 1)
    return (c // width == h).astype(BF16)


def _ssd_kernel(z_ref, xbc_ref, dt_ref, cw_ref, cb_ref, dtb_ref, a_ref, dsk_ref, nw_ref, e_ref,
                o_ref, pad_ref, st_ref):
    c = pl.program_id(1)
    n_pairs = SSD_HEADS // 2

    @pl.when(c == 0)
    def _():
        pad_ref[0:HIST, :] = jnp.zeros((HIST, SSD_XBC), F32)
        st_ref[...] = jnp.zeros(st_ref.shape, F32)

    pad_ref[HIST:HIST + STEP_ROWS, :] = xbc_ref[...].astype(F32)
    padded = pad_ref[...]
    acc = cb_ref[...] + cw_ref[SSD_CONV - 1:SSD_CONV, :] * padded[HIST:, :]
    for shift in range(1, SSD_CONV):
        j = SSD_CONV - 1 - shift
        acc = acc + cw_ref[j:j + 1, :] * pltpu.roll(padded, shift, 0)[HIST:, :]
    pad_ref[0:HIST, :] = pad_ref[STEP_ROWS:STEP_ROWS + HIST, :]
    xbc = _silu(acc)
    xs = xbc[:, :D_MODEL]

    e_mat = e_ref[...]
    dt = _softplus(dt_ref[...] + dtb_ref[...])
    acs = _chunk_cumsum(dt * a_ref[...])
    acs_all = _sel_r(acs, e_mat)
    xs_dt_all = xs * _sel_r(dt, e_mat)

    li = lax.broadcasted_iota(jnp.int32, (CHUNK, D_MODEL), 0)
    si = lax.broadcasted_iota(jnp.int32, (CHUNK, D_MODEL), 1) & (SSD_HEADDIM - 1)
    lane = lax.broadcasted_iota(jnp.int32, (CHUNK, LANES), 1)
    m0 = lane < SSD_HEADDIM

    pairs = range(n_pairs)
    sls = [slice(j * LANES, (j + 1) * LANES) for j in pairs]
    grp = [j // (n_pairs // SSD_GROUPS) for j in pairs]
    st = [st_ref[j] for j in pairs]
    y_rows = []
    for sub in range(STEP_CHUNKS):
        rs = slice(sub * CHUNK, (sub + 1) * CHUNK)
        acs_e, xs_dt = acs_all[rs], xs_dt_all[rs]
        acs_row = jnp.sum(jnp.where(li == si, acs_e, 0.0), axis=0, keepdims=True)
        decay = jnp.exp(jnp.where(si <= li, acs_e - acs_row, -jnp.inf))
        exp_acs = jnp.exp(acs_e)
        acs_last = acs_e[CHUNK - 1:CHUNK, :]
        to_end = jnp.exp(acs_last - acs_e)
        exp_last = jnp.exp(acs_last)
        bm = [xbc[rs, D_MODEL + g * SSD_STATE:D_MODEL + (g + 1) * SSD_STATE].astype(BF16)
              for g in range(SSD_GROUPS)]
        cm = [xbc[rs, D_MODEL + (SSD_GROUPS + g) * SSD_STATE:D_MODEL + (SSD_GROUPS + g + 1) * SSD_STATE
                  ].astype(BF16) for g in range(SSD_GROUPS)]
        cb2 = [_dot(cm[g], jnp.concatenate([bm[g], bm[g]], axis=0), _NT) for g in range(SSD_GROUPS)]
        xp = [xs_dt[:, sl] for sl in sls]
        xbd = [jnp.concatenate([jnp.where(m0, x, 0.0), jnp.where(m0, 0.0, x)], axis=0) for x in xp]
        y_parts = [_bdot(cb2[grp[j]] * decay[:, sls[j]], xbd[j]) + _bdot(cm[grp[j]], st[j]) * exp_acs[:, sls[j]]
                   for j in pairs]
        upd = [_bdot(bm[grp[j]], xp[j] * to_end[:, sls[j]], _TN) for j in pairs]
        st = [st[j] * exp_last[:, sls[j]] + upd[j] for j in pairs]
        y_rows.append(jnp.concatenate(y_parts, axis=1))
    for j in pairs:
        st_ref[j] = st[j]
    y = jnp.concatenate(y_rows, axis=0) + xs * dsk_ref[...]
    y = y * _silu(z_ref[...].astype(F32))
    gw = D_MODEL // SSD_GROUPS
    for g in range(SSD_GROUPS):
        yg = y[:, g * gw:(g + 1) * gw]
        ms = jnp.mean(yg * yg, axis=-1, keepdims=True)
        o_ref[:, g * gw:(g + 1) * gw] = (yg * lax.rsqrt(ms + RMS_EPS) * nw_ref[:, g * gw:(g + 1) * gw]
                                         ).astype(o_ref.dtype)


def _ssd(z, xbc, dt_raw, conv_w, conv_b, dt_bias, a_log, d_skip, norm_w, batch):
    t = z.shape[0]
    nc = t // batch // STEP_ROWS
    pad16 = lambda v: jnp.pad(v.astype(F32), (0, LANES - SSD_HEADS)).reshape(1, LANES)
    row = lambda w: pl.BlockSpec((STEP_ROWS, w), lambda b, c: (b * nc + c, 0))
    return pl.pallas_call(
        _ssd_kernel,
        grid=(batch, nc),
        in_specs=[row(D_MODEL), row(SSD_XBC), row(LANES), _full((SSD_CONV, SSD_XBC)), _full((1, SSD_XBC)),
                  _full((1, LANES)), _full((1, LANES)), _full((1, D_MODEL)), _full((1, D_MODEL)),
                  _full((LANES, D_MODEL))],
        out_specs=row(D_MODEL),
        out_shape=jax.ShapeDtypeStruct((t, D_MODEL), BF16),
        scratch_shapes=[pltpu.VMEM((STEP_ROWS + HIST, SSD_XBC), F32),
                        pltpu.VMEM((SSD_HEADS // 2, SSD_STATE, LANES), F32)],
        compiler_params=_cparams(("parallel", "arbitrary")),
        name="ssd",
    )(z, xbc, dt_raw, conv_w, conv_b.reshape(1, -1), pad16(dt_bias), pad16(-jnp.exp(a_log.astype(F32))),
      jnp.repeat(d_skip.astype(F32), SSD_HEADDIM).reshape(1, -1), norm_w.reshape(1, -1),
      _head_expand(SSD_HEADS, SSD_HEADDIM))


def _hgrn_kernel(f_ref, lb_ref, nw_ref, o_ref, st_ref):
    c = pl.program_id(1)

    @pl.when(c == 0)
    def _():
        st_ref[...] = jnp.zeros(st_ref.shape, F32)

    d = D_MODEL
    lb = lb_ref[...]
    q = _silu(f_ref[:, 0:d].astype(F32))
    forget = lb + (1.0 - lb) * _sigmoid(f_ref[:, d:2 * d].astype(F32))
    k = 1.0 - forget
    bc_all = _chunk_cumsum(jnp.log(forget))
    qd_all = q * jnp.exp(bc_all)
    causal = _tri(CHUNK)
    heads = range(HGRN_HEADS)
    sls = [slice(h * HGRN_EXPAND, (h + 1) * HGRN_EXPAND) for h in heads]
    st = [st_ref[h] for h in heads]
    for sub in range(STEP_CHUNKS):
        rs = slice(sub * CHUNK, (sub + 1) * CHUNK)
        bc, qs, ks, qd = bc_all[rs], q[rs], k[rs], qd_all[rs]
        mid = bc[CHUNK // 2:CHUNK // 2 + 1, :]
        last = bc[CHUNK - 1:CHUNK, :]
        qe = qs * jnp.exp(bc - mid)
        ke = ks * jnp.exp(mid - bc)
        kd = ks * jnp.exp(last - bc)
        w_last = jnp.exp(last)
        vb = [f_ref[rs, 2 * d + h * HGRN_EXPAND:2 * d + (h + 1) * HGRN_EXPAND].astype(BF16) for h in heads]
        att = [jnp.where(causal, _bdot(qe[:, sl], ke[:, sl], _NT), 0.0) for sl in sls]
        o = [_bdot(att[h], vb[h]) + _bdot(qd[:, sls[h]], st[h], _NT) for h in heads]
        upd = [_bdot(vb[h], kd[:, sls[h]], _TN) for h in heads]
        st = [st[h] * w_last[:, sls[h]] + upd[h] for h in heads]
        for h in heads:
            oh = o[h] * lax.rsqrt(jnp.mean(o[h] * o[h], axis=-1, keepdims=True) + RMS_EPS) * nw_ref[...]
            gate = _sigmoid(f_ref[rs, 3 * d + h * HGRN_EXPAND:3 * d + (h + 1) * HGRN_EXPAND].astype(F32))
            o_ref[rs, sls[h]] = (oh * gate).astype(o_ref.dtype)
    for h in heads:
        st_ref[h] = st[h]


def _hgrn(feat, lb, norm_w, batch):
    t = feat.shape[0]
    nc = t // batch // STEP_ROWS
    row = lambda w: pl.BlockSpec((STEP_ROWS, w), lambda b, c: (b * nc + c, 0))
    return pl.pallas_call(
        _hgrn_kernel,
        grid=(batch, nc),
        in_specs=[row(4 * D_MODEL), _full((1, D_MODEL)), _full((1, HGRN_EXPAND))],
        out_specs=row(D_MODEL),
        out_shape=jax.ShapeDtypeStruct((t, D_MODEL), BF16),
        scratch_shapes=[pltpu.VMEM((HGRN_HEADS, HGRN_EXPAND, HGRN_EXPAND), F32)],
        compiler_params=_cparams(("parallel", "arbitrary")),
        name="hgrn2",
    )(feat, lb.reshape(1, -1), norm_w.reshape(1, -1))


def _rwkv_kernel(f_ref, mu_ref, w0_ref, w2_ref, a0_ref, a2_ref, g2_ref, kk_ref, ka_ref, rk_ref, lnw_ref,
                 lnb_ref, e_ref, et_ref, o_ref, pad_ref, st_ref, y_ref):
    c = pl.program_id(1)
    d = D_MODEL
    n_pairs = RWKV_HEADS // 2

    @pl.when(c == 0)
    def _():
        pad_ref[0:HIST, :] = jnp.zeros((HIST, RWKV_COLS), F32)
        st_ref[...] = jnp.zeros(st_ref.shape, F32)

    cur = f_ref[...].astype(F32)
    pad_ref[HIST:HIST + STEP_ROWS, :] = cur
    prev = pltpu.roll(pad_ref[...], 1, 0)[HIST:, :]
    pad_ref[0:HIST, :] = pad_ref[STEP_ROWS:STEP_ROWS + HIST, :]
    x = cur + (prev - cur) * mu_ref[...]
    r, k, v = x[:, 0:d], x[:, d:2 * d], x[:, 2 * d:3 * d]
    lora_in = x[:, 3 * d:3 * d + LANES]
    xg = x[:, 3 * d + LANES:3 * d + 2 * LANES]

    e_mat, et_mat = e_ref[...], et_ref[...]
    w = -_softplus(-(w0_ref[...] + _bdot(jnp.tanh(lora_in), w2_ref[...]))) - 0.5
    ld = -jnp.exp(w)
    a = _sigmoid(a0_ref[...] + _bdot(lora_in, a2_ref[...]))
    gate = _bdot(_sigmoid(xg), g2_ref[...])
    kk = k * kk_ref[...]
    nrm = jnp.maximum(jnp.sqrt(_sel_r(kk * kk, et_mat, 1)), 1e-12)
    kk = kk * _sel_r(1.0 / nrm, e_mat, 2)
    k2 = k * (1.0 + (a - 1.0) * ka_ref[...])
    cw = _chunk_cumsum(ld)
    e_neg = jnp.exp(-cw)
    a_t = -kk * jnp.exp(cw - ld)
    r_t = r * jnp.exp(cw)
    b_t = kk * a * e_neg
    k_t = k2 * e_neg

    lane = lax.broadcasted_iota(jnp.int32, (CHUNK, LANES), 1)
    m0 = lane < RWKV_HEAD
    row2 = lax.broadcasted_iota(jnp.int32, (LANES, LANES), 0)
    col2 = lax.broadcasted_iota(jnp.int32, (LANES, LANES), 1)
    same = (row2 < CHUNK) == (col2 < RWKV_HEAD)
    row4 = lax.broadcasted_iota(jnp.int32, (2 * CHUNK, 2 * LANES), 0)
    col4 = lax.broadcasted_iota(jnp.int32, (2 * CHUNK, 2 * LANES), 1)
    tt, ss = row4 & (CHUNK - 1), col4 & (CHUNK - 1)
    causal4 = (ss < tt) | ((row4 >= CHUNK) & (ss == tt))

    def stack(p):
        return jnp.concatenate([jnp.where(m0, p, 0.0), jnp.where(m0, 0.0, p)], axis=0)

    pairs = range(n_pairs)
    sls = [slice(j * LANES, (j + 1) * LANES) for j in pairs]
    bf = lambda xs: [x.astype(BF16) for x in xs]
    mt = [st_ref[j] for j in pairs]
    for sub in range(STEP_CHUNKS):
        rs = slice(sub * CHUNK, (sub + 1) * CHUNK)
        w_last = jnp.exp(cw[(sub + 1) * CHUNK - 1:(sub + 1) * CHUNK, :])
        ar = bf(jnp.concatenate([a_t[rs, sl], r_t[rs, sl]], axis=0) for sl in sls)
        qmat = bf(jnp.concatenate([b_t[rs, sl], k_t[rs, sl]], axis=0) for sl in sls)
        qbd = bf(jnp.concatenate([stack(b_t[rs, sl]), stack(k_t[rs, sl])], axis=0) for sl in sls)
        vbd = bf(stack(v[rs, sl]) for sl in sls)
        mtb = bf(mt)
        gh = [jnp.where(causal4, _dot(ar[j], qbd[j], _NT), 0.0) for j in pairs]
        n_ab = [g[:CHUNK, :LANES] for g in gh]
        a_rb = [g[CHUNK:, :LANES] for g in gh]
        akrk = bf(g[:, LANES:] for g in gh)
        sm = [_dot(ar[j], mtb[j], _NT) for j in pairs]
        av = [_dot(akrk[j], vbd[j]) for j in pairs]
        xm = [sm[j][:CHUNK] + av[j][:CHUNK] for j in pairs]
        tp = n_ab
        pwf = [_bdot(x, stack(x)) for x in n_ab]
        n_steps = int(math.log2(CHUNK)) - 1
        for step in range(n_steps):
            pbd = bf(stack(x) for x in pwf)
            if step + 1 < n_steps:
                both = [_bdot(jnp.concatenate([pwf[j], tp[j]], axis=0), pbd[j]) for j in pairs]
                tp = [tp[j] + pwf[j] + both[j][CHUNK:] for j in pairs]
                pwf = [both[j][:CHUNK] for j in pairs]
            else:
                tp = [tp[j] + pwf[j] + _bdot(tp[j], pbd[j]) for j in pairs]
        u = [xm[j] + _bdot(tp[j], stack(xm[j])) for j in pairs]
        for j in pairs:
            y_ref[rs, sls[j]] = sm[j][CHUNK:] + av[j][CHUNK:] + _bdot(a_rb[j], stack(u[j]))
        uv = [jnp.concatenate([u[j], v[rs, sls[j]]], axis=0) for j in pairs]
        upd = [jnp.where(same, _bdot(uv[j], qmat[j], _TN), 0.0) for j in pairs]
        mt = [(mt[j] + upd[j]) * w_last[:, sls[j]] for j in pairs]
    for j in pairs:
        st_ref[j] = mt[j]

    y = y_ref[...]
    inv_n = 1.0 / RWKV_HEAD
    yc = y - _sel_r(_sel_r(y, et_mat, 1) * inv_n, e_mat, 2)
    rs = lax.rsqrt(_sel_r(yc * yc, et_mat, 1) * inv_n + RWKV_GN_EPS)
    yn = yc * _sel_r(rs, e_mat, 2) * lnw_ref[...] + lnb_ref[...]
    bonus = _sel_r(_sel_r(r * k2 * rk_ref[...], et_mat, 1), e_mat, 2) * v
    o_ref[...] = ((yn + bonus) * gate).astype(o_ref.dtype)


def _rwkv(feat, mu, w0, w2, a0, a2, g2, k_k, k_a, r_k, ln_w, ln_b, batch):
    t = feat.shape[0]
    nc = t // batch // STEP_ROWS
    d = D_MODEL
    row = lambda w: pl.BlockSpec((STEP_ROWS, w), lambda b, c: (b * nc + c, 0))
    vec = lambda v: v.astype(F32).reshape(1, -1)
    w2p = jnp.concatenate([w2, jnp.zeros_like(w2)], axis=0).astype(BF16)
    a2p = jnp.concatenate([jnp.zeros_like(a2), a2], axis=0).astype(BF16)
    e_mat = _head_expand(RWKV_HEADS, RWKV_HEAD)
    return pl.pallas_call(
        _rwkv_kernel,
        grid=(batch, nc),
        in_specs=[row(RWKV_COLS), _full((1, RWKV_COLS)), _full((1, d)), _full((LANES, d)), _full((1, d)),
                  _full((LANES, d)), _full((LANES, d)), _full((1, d)), _full((1, d)), _full((1, d)),
                  _full((1, d)), _full((1, d)), _full((LANES, d)), _full((d, LANES))],
        out_specs=row(d),
        out_shape=jax.ShapeDtypeStruct((t, d), BF16),
        scratch_shapes=[pltpu.VMEM((STEP_ROWS + HIST, RWKV_COLS), F32),
                        pltpu.VMEM((RWKV_HEADS // 2, LANES, LANES), F32),
                        pltpu.VMEM((STEP_ROWS, d), F32)],
        compiler_params=_cparams(("parallel", "arbitrary")),
        name="rwkv7",
    )(feat, vec(mu), vec(w0), w2p, vec(a0), a2p, g2.astype(BF16), vec(k_k), vec(k_a), vec(r_k), vec(ln_w),
      vec(ln_b), e_mat, e_mat.T)


def _merge_kernel(ya_ref, yb_ref, yc_ref, g_ref, h_ref, wa_ref, wb_ref, wc_ref, wo_ref, lg_ref, lb_ref,
                  rw_ref, rb_ref, o_ref, gt_ref):
    d = D_MODEL
    gate = lambda j: _sigmoid(g_ref[:, j * d:(j + 1) * d].astype(F32))
    m = (gate(0) * _dot(ya_ref[...], wa_ref[...]) + gate(1) * _dot(yb_ref[...], wb_ref[...])
         + gate(2) * _dot(yc_ref[...], wc_ref[...]))
    hn = DN_ALPHA * h_ref[...] + _bdot(m, wo_ref[...])
    y = _layernorm_rows(hn, lg_ref[...], lb_ref[...])
    o_ref[...] = y
    _route(y, rw_ref, rb_ref, gt_ref)


def _merge(ya, yb, yc, gates, h, wa, wb, wc, wo, ln_g, ln_b, router_w, router_bias, tm=512):
    t, d = h.shape
    tm = min(tm, t)
    row = lambda w: pl.BlockSpec((tm, w), lambda i: (i, 0))
    return pl.pallas_call(
        _merge_kernel,
        grid=(t // tm,),
        in_specs=[row(d), row(d), row(d), row(3 * d), row(d)] + [_full((d, d))] * 4 + [_full((1, d))] * 2
        + [_full((N_EXPERTS, d)), _full((N_EXPERTS, 1))],
        out_specs=[row(d), pl.BlockSpec((N_EXPERTS, tm), lambda i: (0, i))],
        out_shape=[jax.ShapeDtypeStruct((t, d), F32), jax.ShapeDtypeStruct((N_EXPERTS, t), F32)],
        compiler_params=_cparams(("parallel",)),
        name="merge_ln_route",
    )(ya, yb, yc, gates, h, wa.astype(BF16), wb.astype(BF16), wc.astype(BF16), wo.astype(BF16),
      ln_g.reshape(1, d), ln_b.reshape(1, d), router_w.T, router_bias.reshape(N_EXPERTS, 1).astype(F32))


def _route(h, wt_ref, bias_ref, o_ref):
    logits = lax.dot_general(wt_ref[...], h, _NT, precision=lax.Precision.HIGHEST,
                             preferred_element_type=F32)
    mx = jnp.max(logits, axis=0, keepdims=True)
    ex = jnp.exp(logits - mx)
    probs = ex / jnp.sum(ex, axis=0, keepdims=True)
    sel = probs + bias_ref[...]
    rows = [sel[e:e + 1, :] for e in range(N_EXPERTS)]
    prow = [probs[e:e + 1, :] for e in range(N_EXPERTS)]
    gscore = []
    for g in range(N_GROUPS):
        m = rows[g * EXPERTS_PER_GROUP:(g + 1) * EXPERTS_PER_GROUP]
        best = None
        for i in range(EXPERTS_PER_GROUP):
            for j in range(i + 1, EXPERTS_PER_GROUP):
                s = m[i] + m[j]
                best = s if best is None else jnp.maximum(best, s)
        gscore.append(best)
    chosen = []
    for g in range(N_GROUPS):
        ok = None
        for g2 in range(N_GROUPS):
            if g2 == g:
                continue
            t = (gscore[g] > gscore[g2]) if g2 < g else (gscore[g] >= gscore[g2])
            ok = t if ok is None else (ok & t)
        chosen.append(ok)
    picked = []
    for e in range(N_EXPERTS):
        g = e // EXPERTS_PER_GROUP
        rank = None
        for e2 in range(g * EXPERTS_PER_GROUP, (g + 1) * EXPERTS_PER_GROUP):
            if e2 == e:
                continue
            ahead = (rows[e2] >= rows[e]) if e2 < e else (rows[e2] > rows[e])
            ahead = ahead.astype(F32)
            rank = ahead if rank is None else rank + ahead
        picked.append(jnp.where(chosen[g] & (rank < 2.0), prow[e], 0.0))
    tot = picked[0]
    for e in range(1, N_EXPERTS):
        tot = tot + picked[e]
    inv = 1.0 / tot
    for e in range(N_EXPERTS):
        o_ref[e:e + 1, :] = picked[e] * inv


MOE_TILE = 256
MOE_ISSUE_UNROLLS = (32, 4, 1)
_PAIRS = [(a, b) for a in range(EXPERTS_PER_GROUP) for b in range(a + 1, EXPERTS_PER_GROUP)]
N_CLASSES = N_GROUPS * len(_PAIRS)


def _moe_num_tiles(t):
    return -(-(t + N_CLASSES * (MOE_TILE - 1)) // MOE_TILE)


def _route_meta(gates_t):
    e_n, t = gates_t.shape
    i32 = jnp.int32
    mask = gates_t > 0
    eidx = lax.broadcasted_iota(i32, (e_n, t), 0)
    e0 = jnp.minimum(jnp.min(jnp.where(mask, eidx, e_n), axis=0), e_n - 1)
    e1 = jnp.maximum(jnp.max(jnp.where(mask, eidx, -1), axis=0), e0)
    grp = e0 // EXPERTS_PER_GROUP
    a = e0 % EXPERTS_PER_GROUP
    b = jnp.where(e1 // EXPERTS_PER_GROUP == grp, e1 % EXPERTS_PER_GROUP, a)
    b = jnp.where(b == a, (a + 1) % EXPERTS_PER_GROUP, b)
    lo, hi = jnp.minimum(a, b), jnp.maximum(a, b)
    n_pairs = len(_PAIRS)
    cls = grp * n_pairs + (lo * (2 * EXPERTS_PER_GROUP - 1 - lo)) // 2 + (hi - lo - 1)
    g_lo = jnp.take_along_axis(gates_t, (grp * EXPERTS_PER_GROUP + lo)[None, :], axis=0)[0]
    g_hi = jnp.take_along_axis(gates_t, (grp * EXPERTS_PER_GROUP + hi)[None, :], axis=0)[0]
    onehot = (cls[None, :] == jnp.arange(N_CLASSES, dtype=i32)[:, None]).astype(i32)
    csum = jnp.cumsum(onehot, axis=1)
    cnt = csum[:, -1]
    rank = jnp.sum(onehot * (csum - 1), axis=0)
    ptiles = (cnt + MOE_TILE - 1) // MOE_TILE
    tile_end = jnp.cumsum(ptiles)
    tile_off = tile_end - ptiles
    pos = jnp.take(tile_off, cls) * MOE_TILE + rank
    n_tiles = _moe_num_tiles(t)
    as_i32 = lambda g: lax.bitcast_convert_type(g.astype(F32), i32)
    per_token = jnp.stack([jnp.arange(t, dtype=i32), as_i32(g_lo), as_i32(g_hi)], axis=1)
    per_row = jnp.zeros((n_tiles * MOE_TILE, 3), i32).at[pos].set(per_token, unique_indices=True)
    row_token = per_row[:, 0]
    row_gates = lax.bitcast_convert_type(per_row[:, 1:3], F32)
    tid = jnp.arange(n_tiles, dtype=i32)
    tcls = jnp.minimum(jnp.sum((tid[:, None] >= tile_end[None, :]).astype(i32), axis=1), N_CLASSES - 1)
    n_valid = jnp.clip(jnp.take(cnt, tcls) - (tid - jnp.take(tile_off, tcls)) * MOE_TILE, 0, MOE_TILE)
    n_valid = jnp.where(tid < tile_end[-1], n_valid, 0).astype(i32)
    cls_lo = jnp.array([g * EXPERTS_PER_GROUP + p[0] for g in range(N_GROUPS) for p in _PAIRS], i32)
    cls_hi = jnp.array([g * EXPERTS_PER_GROUP + p[1] for g in range(N_GROUPS) for p in _PAIRS], i32)
    return row_token, jnp.take(cls_lo, tcls), jnp.take(cls_hi, tcls), n_valid, row_gates


def _moe_kernel(rt_ref, ea_ref, eb_ref, nv_ref, h_hbm, rg_ref, wga_ref, wua_ref, wda_ref, wgb_ref, wub_ref,
                wdb_ref, lg_ref, lb_ref, out_hbm, xbuf, obuf, wup_buf, wdn_buf, sem_in, sem_out):
    i = pl.program_id(0)
    n = pl.num_programs(0)
    slot = lax.rem(i, 2)

    def rows_in(row, s, k, rows=1):
        return pltpu.make_async_copy(h_hbm.at[pl.ds(row, rows)], xbuf.at[s, pl.ds(k, rows)], sem_in.at[s])

    def rows_out(row, s, k, rows=1):
        return pltpu.make_async_copy(obuf.at[s, pl.ds(k, rows)], out_hbm.at[pl.ds(row, rows)], sem_out.at[s])

    def issue(copy_at, tile, count):
        base = tile * MOE_TILE

        @pl.when(count == MOE_TILE)
        def _():
            for k in range(MOE_TILE):
                copy_at(base, k).start()

        @pl.when(count < MOE_TILE)
        def _():
            done = 0
            for unroll in MOE_ISSUE_UNROLLS:
                def group(q, c, unroll=unroll, done=done):
                    for r in range(unroll):
                        copy_at(base, done + q * unroll + r).start()
                    return c
                groups = lax.shift_right_logical(count - done, int(math.log2(unroll)))
                lax.fori_loop(0, groups, group, 0)
                done = done + groups * unroll

    def wait_rows(copies, count):
        rows = MOE_TILE
        while rows >= 1:
            @pl.when((count & rows) != 0)
            def _(rows=rows):
                copies(0, 0, rows).wait()
            rows //= 2

    def start_gather(tile, s):
        issue(lambda base, k: rows_in(rt_ref[base + k], s, k), tile, nv_ref[tile])

    @pl.when(i == 0)
    def _():
        xbuf[...] = jnp.zeros(xbuf.shape, F32)
        start_gather(0, 0)

    @pl.when(i + 1 < n)
    def _():
        start_gather(i + 1, 1 - slot)

    wait_rows(lambda row, k, rows: rows_in(row, slot, k, rows), nv_ref[i])

    @pl.when(i >= 2)
    def _():
        wait_rows(lambda row, k, rows: rows_out(row, slot, k, rows), nv_ref[i - 2])

    @pl.when(nv_ref[i] > 0)
    def _():
        prev = jnp.maximum(i - 1, 0)
        for which, e_ref, (g_ref, u_ref, d_ref) in ((0, ea_ref, (wga_ref, wua_ref, wda_ref)),
                                                    (1, eb_ref, (wgb_ref, wub_ref, wdb_ref))):
            @pl.when((i == 0) | (e_ref[i] != e_ref[prev]))
            def _(which=which, g_ref=g_ref, u_ref=u_ref, d_ref=d_ref):
                wup_buf[2 * which] = g_ref[0, 0].astype(BF16)
                wup_buf[2 * which + 1] = u_ref[0, 0].astype(BF16)
                wdn_buf[which] = d_ref[0, 0].astype(BF16)

        x = xbuf[slot]
        xb = x.astype(BF16)
        rg = rg_ref[...]
        act_a = (_silu(_dot(xb, wup_buf[0])) * _dot(xb, wup_buf[1])).astype(BF16)
        act_b = (_silu(_dot(xb, wup_buf[2])) * _dot(xb, wup_buf[3])).astype(BF16)
        moe = rg[:, 0:1] * _dot(act_a, wdn_buf[0]) + rg[:, 1:2] * _dot(act_b, wdn_buf[1])
        obuf[slot] = _layernorm_rows(DN_ALPHA * x + moe, lg_ref[...], lb_ref[...])
        issue(lambda base, k: rows_out(rt_ref[base + k], slot, k), i, nv_ref[i])

    @pl.when(i == n - 1)
    def _():
        @pl.when(i >= 1)
        def _():
            wait_rows(lambda row, k, rows: rows_out(row, 1 - slot, k, rows), nv_ref[i - 1])
        wait_rows(lambda row, k, rows: rows_out(row, slot, k, rows), nv_ref[i])


def _moe(h, gates_t, wg, wu, wd, layer, ln_g, ln_b):
    t, d = h.shape
    n_tiles = _moe_num_tiles(t)
    row_token, exp_a, exp_b, n_valid, row_gates = _route_meta(gates_t)
    w_up = lambda which: pl.BlockSpec((1, 1, d, D_EXPERT),
                                      lambda i, rt, ea, eb, nv: (layer, (ea, eb)[which][i], 0, 0))
    w_dn = lambda which: pl.BlockSpec((1, 1, D_EXPERT, d),
                                      lambda i, rt, ea, eb, nv: (layer, (ea, eb)[which][i], 0, 0))
    const = lambda shape: pl.BlockSpec(shape, lambda i, *_: (0,) * len(shape))
    grid_spec = pltpu.PrefetchScalarGridSpec(
        num_scalar_prefetch=4,
        grid=(n_tiles,),
        in_specs=[pl.BlockSpec(memory_space=pl.ANY),
                  pl.BlockSpec((MOE_TILE, 2), lambda i, *_: (i, 0)),
                  w_up(0), w_up(0), w_dn(0), w_up(1), w_up(1), w_dn(1), const((1, d)), const((1, d))],
        out_specs=pl.BlockSpec(memory_space=pl.ANY),
        scratch_shapes=[pltpu.VMEM((2, MOE_TILE, d), F32), pltpu.VMEM((2, MOE_TILE, d), F32),
                        pltpu.VMEM((4, d, D_EXPERT), BF16), pltpu.VMEM((2, D_EXPERT, d), BF16),
                        pltpu.SemaphoreType.DMA((2,)), pltpu.SemaphoreType.DMA((2,))],
    )
    return pl.pallas_call(
        _moe_kernel,
        grid_spec=grid_spec,
        out_shape=jax.ShapeDtypeStruct((t, d), F32),
        compiler_params=_cparams(("arbitrary",)),
        name="experts_ln",
    )(row_token, exp_a, exp_b, n_valid, h, row_gates, wg, wu, wd, wg, wu, wd, ln_g.reshape(1, d), ln_b.reshape(1, d))


def kernel(x, ln_in_g, ln_in_b, w_in, ssd_conv_w, ssd_conv_b, ssd_dt_bias, ssd_a_log, ssd_d, ssd_norm_w, rwkv_mu, rwkv_w0, rwkv_w2, rwkv_a0, rwkv_a2, rwkv_g2, rwkv_k_k, rwkv_k_a, rwkv_r_k, rwkv_ln_w, rwkv_ln_b, hgrn_lb, hgrn_norm_w, w_br_ssd, w_br_rwkv, w_br_hgrn, w_out, ln1_g, ln1_b, router_w, router_bias, exp_w_gate, exp_w_up, exp_w_down, ln2_g, ln2_b):
    batch, seq, d = x.shape
    t = batch * seq
    h, h_bf = _layernorm(x.reshape(t, d), ln_in_g, ln_in_b)
    lsm = jax.nn.softmax(hgrn_lb.astype(F32), axis=0)
    lower_bounds = jnp.cumsum(lsm, axis=0) - lsm[0]
    w_all = _pack_in_weights(w_in)
    for l in range(DEPTH):
        z, xbc, dt_raw, f_rwkv, f_hgrn, gates = [
            _project(h_bf, w_all, l, seg, F32 if seg == "dt" else BF16)
            for seg in ("z", "xbc", "dt", "rwkv", "hgrn", "gates")]
        y_a = _ssd(z, xbc, dt_raw, ssd_conv_w[l], ssd_conv_b[l], ssd_dt_bias[l], ssd_a_log[l], ssd_d[l],
                   ssd_norm_w[l], batch)
        y_b = _rwkv(f_rwkv, rwkv_mu[l], rwkv_w0[l], rwkv_w2[l], rwkv_a0[l], rwkv_a2[l], rwkv_g2[l],
                    rwkv_k_k[l], rwkv_k_a[l], rwkv_r_k[l].reshape(-1), rwkv_ln_w[l], rwkv_ln_b[l], batch)
        y_c = _hgrn(f_hgrn, lower_bounds[l], hgrn_norm_w[l], batch)
        h, gates_t = _merge(y_a, y_b, y_c, gates, h, w_br_ssd[l], w_br_rwkv[l], w_br_hgrn[l], w_out[l],
                            ln1_g[l], ln1_b[l], router_w, router_bias)
        h = _moe(h, gates_t, exp_w_gate, exp_w_up, exp_w_down, l, ln2_g[l], ln2_b[l])
        h_bf = h.astype(BF16)
    return h.reshape(batch, seq, d)
```

```python
import math

import jax
import jax.numpy as jnp
from jax import lax
from jax.experimental import pallas as pl
from jax.experimental.pallas import tpu as pltpu

F32 = jnp.float32
BF16 = jnp.bfloat16

D_MODEL = 1024
DEPTH = 2
CHUNK = 64
STEP_CHUNKS = 4
STEP_ROWS = STEP_CHUNKS * CHUNK
LANES = 128
HIST = 8

SSD_HEADS = 16
SSD_HEADDIM = 64
SSD_GROUPS = 2
SSD_STATE = 128
SSD_CONV = 4
SSD_XBC = D_MODEL + 2 * SSD_GROUPS * SSD_STATE

RWKV_HEADS = 16
RWKV_HEAD = 64
RWKV_COLS = 3 * D_MODEL + 64 + 64 + 128
RWKV_GN_EPS = 64e-5

HGRN_HEADS = 8
HGRN_EXPAND = 128

IN_SIZES = (D_MODEL, SSD_XBC, SSD_HEADS, RWKV_COLS, 4 * D_MODEL, 3 * D_MODEL)

N_EXPERTS = 16
N_GROUPS = 4
EXPERTS_PER_GROUP = 4
D_EXPERT = 512

DN_ALPHA = (2 * DEPTH) ** 0.25
LN_EPS = 1e-5
RMS_EPS = 1e-6

VMEM_LIMIT = 56 * 1024 * 1024

_NN = (((1,), (0,)), ((), ()))
_NT = (((1,), (1,)), ((), ()))
_TN = (((0,), (0,)), ((), ()))


def _dot(a, b, dims=_NN):
    return lax.dot_general(a, b, dims, preferred_element_type=F32)


def _bdot(a, b, dims=_NN):
    return _dot(a.astype(BF16), b.astype(BF16), dims)


def _split3(x):
    x1 = x.astype(BF16)
    r1 = x - x1.astype(F32)
    x2 = r1.astype(BF16)
    x3 = (r1 - x2.astype(F32)).astype(BF16)
    return x1, x2, x3


def _sel_r(x, sel, pieces=3):
    n = x.shape[0]
    if pieces == 1:
        return _bdot(x, sel)
    y = _dot(jnp.concatenate(_split3(x)[:pieces], axis=0), sel)
    out = y[:n] + y[n:2 * n]
    return out + y[2 * n:] if pieces == 3 else out


def _chunk_cumsum(x):
    row = lax.broadcasted_iota(jnp.int32, x.shape, 0) & (CHUNK - 1)
    shift = 1
    while shift < CHUNK:
        x = x + jnp.where(row >= shift, pltpu.roll(x, shift, 0), 0.0)
        shift *= 2
    return x


def _sigmoid(x):
    return 1.0 / (1.0 + jnp.exp(-x))


def _silu(x):
    return x * _sigmoid(x)


def _softplus(x):
    return jnp.maximum(x, 0.0) + jnp.log1p(jnp.exp(-jnp.abs(x)))


def _layernorm_rows(x, g, b):
    mu = jnp.mean(x, axis=-1, keepdims=True)
    xc = x - mu
    var = jnp.mean(xc * xc, axis=-1, keepdims=True)
    return xc * lax.rsqrt(var + LN_EPS) * g + b


def _tri(n, strict=False):
    r = lax.broadcasted_iota(jnp.int32, (n, n), 0)
    c = lax.broadcasted_iota(jnp.int32, (n, n), 1)
    return (c < r) if strict else (c <= r)


def _cparams(sem):
    return pltpu.CompilerParams(dimension_semantics=sem, vmem_limit_bytes=VMEM_LIMIT)


def _full(shape):
    return pl.BlockSpec(shape, lambda *_: (0,) * len(shape))


def _ln_kernel(x_ref, g_ref, b_ref, o_ref, ob_ref):
    y = _layernorm_rows(x_ref[...], g_ref[...], b_ref[...])
    o_ref[...] = y
    ob_ref[...] = y.astype(BF16)


def _layernorm(x, g, b, tm=512):
    t, d = x.shape
    row = pl.BlockSpec((tm, d), lambda i: (i, 0))
    return pl.pallas_call(
        _ln_kernel,
        grid=(t // tm,),
        in_specs=[row, _full((1, d)), _full((1, d))],
        out_specs=[row, row],
        out_shape=[jax.ShapeDtypeStruct((t, d), F32), jax.ShapeDtypeStruct((t, d), BF16)],
        compiler_params=_cparams(("parallel",)),
        name="layernorm_in",
    )(x, g.reshape(1, d), b.reshape(1, d))


def _proj_kernel(x_ref, w_ref, o_ref):
    o_ref[...] = _dot(x_ref[...], w_ref[0]).astype(o_ref.dtype)


_IN_LAYOUT = (("hgrn", 4, 4 * D_MODEL, 2048), ("z", 0, D_MODEL, 1024), ("pad0", None, 1024, 1024),
              ("xbc", 1, SSD_XBC, SSD_XBC), ("gates", 5, 3 * D_MODEL, SSD_XBC), ("pad1", None, 896, 128),
              ("rwkv", 3, RWKV_COLS, RWKV_COLS // 2), ("dt", 2, LANES, LANES))


def _in_layout_offsets():
    offs, acc = {}, 0
    for name, _, width, tn in _IN_LAYOUT:
        assert acc % tn == 0 and width % tn == 0, name
        offs[name] = acc
        acc += width
    return offs, acc


def _pack_in_weights(w_in):
    src, acc = [], 0
    for s in IN_SIZES:
        src.append((acc, s))
        acc += s
    parts = []
    for _, idx, width, _ in _IN_LAYOUT:
        if idx is None:
            parts.append(jnp.zeros(w_in.shape[:2] + (width,), w_in.dtype))
            continue
        start, size = src[idx]
        part = w_in[:, :, start:start + size]
        if size < width:
            part = jnp.pad(part, ((0, 0), (0, 0), (0, width - size)))
        parts.append(part)
    return jnp.concatenate(parts, axis=-1).astype(BF16)


def _project(x_bf, w_all, layer, seg, out_dtype=BF16, tm=1024):
    t, k = x_bf.shape
    tm = min(tm, t)
    offs, _ = _in_layout_offsets()
    n, tn = next((width, tn) for name, _, width, tn in _IN_LAYOUT if name == seg)
    first = offs[seg] // tn
    return pl.pallas_call(
        _proj_kernel,
        grid=(n // tn, t // tm),
        in_specs=[pl.BlockSpec((tm, k), lambda j, i: (i, 0)),
                  pl.BlockSpec((1, k, tn), lambda j, i: (layer, 0, first + j))],
        out_specs=pl.BlockSpec((tm, tn), lambda j, i: (i, j)),
        out_shape=jax.ShapeDtypeStruct((t, n), out_dtype),
        compiler_params=_cparams(("parallel", "parallel")),
        name=f"proj_{seg}",
    )(x_bf, w_all)


def _proj_z_dt_kernel(x_ref, wz_ref, wdt_ref, z_ref, dt_ref):
    x = x_ref[...]
    z_ref[...] = _dot(x, wz_ref[0]).astype(z_ref.dtype)
    dt_ref[...] = _dot(x, wdt_ref[0])


def _project_z_dt(x_bf, w_all, layer, tm=1024):
    t, k = x_bf.shape
    tm = min(tm, t)
    offs, _ = _in_layout_offsets()
    wspec = lambda seg, n: pl.BlockSpec((1, k, n), lambda i: (layer, 0, offs[seg] // n))
    return pl.pallas_call(
        _proj_z_dt_kernel,
        grid=(t // tm,),
        in_specs=[pl.BlockSpec((tm, k), lambda i: (i, 0)), wspec("z", D_MODEL), wspec("dt", LANES)],
        out_specs=[pl.BlockSpec((tm, D_MODEL), lambda i: (i, 0)), pl.BlockSpec((tm, LANES), lambda i: (i, 0))],
        out_shape=[jax.ShapeDtypeStruct((t, D_MODEL), BF16), jax.ShapeDtypeStruct((t, LANES), F32)],
        compiler_params=_cparams(("parallel",)),
        name="proj_z_dt",
    )(x_bf, w_all, w_all)


def _head_expand(n_heads, width):
    h = lax.broadcasted_iota(jnp.int32, (LANES, n_heads * width), 0)
    c = lax.broadcasted_iota(jnp.int32, (LANES, n_heads * width), 1)
    return (c // width == h).astype(BF16)


def _ssd_kernel(z_ref, xbc_ref, dt_ref, cw_ref, cb_ref, dtb_ref, a_ref, dsk_ref, nw_ref, e_ref,
                o_ref, pad_ref, st_ref):
    c = pl.program_id(1)
    n_pairs = SSD_HEADS // 2

    @pl.when(c == 0)
    def _():
        pad_ref[0:HIST, :] = jnp.zeros((HIST, SSD_XBC), F32)
        st_ref[...] = jnp.zeros(st_ref.shape, F32)

    pad_ref[HIST:HIST + STEP_ROWS, :] = xbc_ref[...].astype(F32)
    padded = pad_ref[...]
    acc = cb_ref[...] + cw_ref[SSD_CONV - 1:SSD_CONV, :] * padded[HIST:, :]
    for shift in range(1, SSD_CONV):
        j = SSD_CONV - 1 - shift
        acc = acc + cw_ref[j:j + 1, :] * pltpu.roll(padded, shift, 0)[HIST:, :]
    pad_ref[0:HIST, :] = pad_ref[STEP_ROWS:STEP_ROWS + HIST, :]
    xbc = _silu(acc)
    xs = xbc[:, :D_MODEL]

    e_mat = e_ref[...]
    dt = _softplus(dt_ref[...] + dtb_ref[...])
    acs = _chunk_cumsum(dt * a_ref[...])
    acs_all = _sel_r(acs, e_mat)
    xs_dt_all = xs * _sel_r(dt, e_mat)

    li = lax.broadcasted_iota(jnp.int32, (CHUNK, D_MODEL), 0)
    si = lax.broadcasted_iota(jnp.int32, (CHUNK, D_MODEL), 1) & (SSD_HEADDIM - 1)
    lane = lax.broadcasted_iota(jnp.int32, (CHUNK, LANES), 1)
    m0 = lane < SSD_HEADDIM

    pairs = range(n_pairs)
    sls = [slice(j * LANES, (j + 1) * LANES) for j in pairs]
    grp = [j // (n_pairs // SSD_GROUPS) for j in pairs]
    st = [st_ref[j] for j in pairs]
    y_rows = []
    for sub in range(STEP_CHUNKS):
        rs = slice(sub * CHUNK, (sub + 1) * CHUNK)
        acs_e, xs_dt = acs_all[rs], xs_dt_all[rs]
        acs_row = jnp.sum(jnp.where(li == si, acs_e, 0.0), axis=0, keepdims=True)
        decay = jnp.exp(jnp.where(si <= li, acs_e - acs_row, -jnp.inf))
        exp_acs = jnp.exp(acs_e)
        acs_last = acs_e[CHUNK - 1:CHUNK, :]
        to_end = jnp.exp(acs_last - acs_e)
        exp_last = jnp.exp(acs_last)
        bm = [xbc[rs, D_MODEL + g * SSD_STATE:D_MODEL + (g + 1) * SSD_STATE].astype(BF16)
              for g in range(SSD_GROUPS)]
        cm = [xbc[rs, D_MODEL + (SSD_GROUPS + g) * SSD_STATE:D_MODEL + (SSD_GROUPS + g + 1) * SSD_STATE
                  ].astype(BF16) for g in range(SSD_GROUPS)]
        cb2 = [_dot(cm[g], jnp.concatenate([bm[g], bm[g]], axis=0), _NT) for g in range(SSD_GROUPS)]
        xp = [xs_dt[:, sl] for sl in sls]
        xbd = [jnp.concatenate([jnp.where(m0, x, 0.0), jnp.where(m0, 0.0, x)], axis=0) for x in xp]
        y_parts = [_bdot(cb2[grp[j]] * decay[:, sls[j]], xbd[j]) + _bdot(cm[grp[j]], st[j]) * exp_acs[:, sls[j]]
                   for j in pairs]
        upd = [_bdot(bm[grp[j]], xp[j] * to_end[:, sls[j]], _TN) for j in pairs]
        st = [st[j] * exp_last[:, sls[j]] + upd[j] for j in pairs]
        y_rows.append(jnp.concatenate(y_parts, axis=1))
    for j in pairs:
        st_ref[j] = st[j]
    y = jnp.concatenate(y_rows, axis=0) + xs * dsk_ref[...]
    y = y * _silu(z_ref[...].astype(F32))
    gw = D_MODEL // SSD_GROUPS
    for g in range(SSD_GROUPS):
        yg = y[:, g * gw:(g + 1) * gw]
        ms = jnp.mean(yg * yg, axis=-1, keepdims=True)
        o_ref[:, g * gw:(g + 1) * gw] = (yg * lax.rsqrt(ms + RMS_EPS) * nw_ref[:, g * gw:(g + 1) * gw]
                                         ).astype(o_ref.dtype)


def _ssd(z, xbc, dt_raw, conv_w, conv_b, dt_bias, a_log, d_skip, norm_w, batch):
    t = z.shape[0]
    nc = t // batch // STEP_ROWS
    pad16 = lambda v: jnp.pad(v.astype(F32), (0, LANES - SSD_HEADS)).reshape(1, LANES)
    row = lambda w: pl.BlockSpec((STEP_ROWS, w), lambda b, c: (b * nc + c, 0))
    return pl.pallas_call(
        _ssd_kernel,
        grid=(batch, nc),
        in_specs=[row(D_MODEL), row(SSD_XBC), row(LANES), _full((SSD_CONV, SSD_XBC)), _full((1, SSD_XBC)),
                  _full((1, LANES)), _full((1, LANES)), _full((1, D_MODEL)), _full((1, D_MODEL)),
                  _full((LANES, D_MODEL))],
        out_specs=row(D_MODEL),
        out_shape=jax.ShapeDtypeStruct((t, D_MODEL), BF16),
        scratch_shapes=[pltpu.VMEM((STEP_ROWS + HIST, SSD_XBC), F32),
                        pltpu.VMEM((SSD_HEADS // 2, SSD_STATE, LANES), F32)],
        compiler_params=_cparams(("parallel", "arbitrary")),
        name="ssd",
    )(z, xbc, dt_raw, conv_w, conv_b.reshape(1, -1), pad16(dt_bias), pad16(-jnp.exp(a_log.astype(F32))),
      jnp.repeat(d_skip.astype(F32), SSD_HEADDIM).reshape(1, -1), norm_w.reshape(1, -1),
      _head_expand(SSD_HEADS, SSD_HEADDIM))


def _hgrn_kernel(f_ref, lb_ref, nw_ref, o_ref, st_ref):
    c = pl.program_id(1)

    @pl.when(c == 0)
    def _():
        st_ref[...] = jnp.zeros(st_ref.shape, F32)

    d = D_MODEL
    lb = lb_ref[...]
    q = _silu(f_ref[:, 0:d].astype(F32))
    forget = lb + (1.0 - lb) * _sigmoid(f_ref[:, d:2 * d].astype(F32))
    k = 1.0 - forget
    bc_all = _chunk_cumsum(jnp.log(forget))
    qd_all = q * jnp.exp(bc_all)
    causal = _tri(CHUNK)
    heads = range(HGRN_HEADS)
    sls = [slice(h * HGRN_EXPAND, (h + 1) * HGRN_EXPAND) for h in heads]
    st = [st_ref[h] for h in heads]
    for sub in range(STEP_CHUNKS):
        rs = slice(sub * CHUNK, (sub + 1) * CHUNK)
        bc, qs, ks, qd = bc_all[rs], q[rs], k[rs], qd_all[rs]
        mid = bc[CHUNK // 2:CHUNK // 2 + 1, :]
        last = bc[CHUNK - 1:CHUNK, :]
        qe = qs * jnp.exp(bc - mid)
        ke = ks * jnp.exp(mid - bc)
        kd = ks * jnp.exp(last - bc)
        w_last = jnp.exp(last)
        vb = [f_ref[rs, 2 * d + h * HGRN_EXPAND:2 * d + (h + 1) * HGRN_EXPAND].astype(BF16) for h in heads]
        att = [jnp.where(causal, _bdot(qe[:, sl], ke[:, sl], _NT), 0.0) for sl in sls]
        o = [_bdot(att[h], vb[h]) + _bdot(qd[:, sls[h]], st[h], _NT) for h in heads]
        upd = [_bdot(vb[h], kd[:, sls[h]], _TN) for h in heads]
        st = [st[h] * w_last[:, sls[h]] + upd[h] for h in heads]
        for h in heads:
            oh = o[h] * lax.rsqrt(jnp.mean(o[h] * o[h], axis=-1, keepdims=True) + RMS_EPS) * nw_ref[...]
            gate = _sigmoid(f_ref[rs, 3 * d + h * HGRN_EXPAND:3 * d + (h + 1) * HGRN_EXPAND].astype(F32))
            o_ref[rs, sls[h]] = (oh * gate).astype(o_ref.dtype)
    for h in heads:
        st_ref[h] = st[h]


def _hgrn(feat, lb, norm_w, batch):
    t = feat.shape[0]
    nc = t // batch // STEP_ROWS
    row = lambda w: pl.BlockSpec((STEP_ROWS, w), lambda b, c: (b * nc + c, 0))
    return pl.pallas_call(
        _hgrn_kernel,
        grid=(batch, nc),
        in_specs=[row(4 * D_MODEL), _full((1, D_MODEL)), _full((1, HGRN_EXPAND))],
        out_specs=row(D_MODEL),
        out_shape=jax.ShapeDtypeStruct((t, D_MODEL), BF16),
        scratch_shapes=[pltpu.VMEM((HGRN_HEADS, HGRN_EXPAND, HGRN_EXPAND), F32)],
        compiler_params=_cparams(("parallel", "arbitrary")),
        name="hgrn2",
    )(feat, lb.reshape(1, -1), norm_w.reshape(1, -1))


def _rwkv_kernel(f_ref, mu_ref, w0_ref, w2_ref, a0_ref, a2_ref, g2_ref, kk_ref, ka_ref, rk_ref, lnw_ref,
                 lnb_ref, e_ref, et_ref, o_ref, pad_ref, st_ref, y_ref):
    c = pl.program_id(1)
    d = D_MODEL
    n_pairs = RWKV_HEADS // 2

    @pl.when(c == 0)
    def _():
        pad_ref[0:HIST, :] = jnp.zeros((HIST, RWKV_COLS), F32)
        st_ref[...] = jnp.zeros(st_ref.shape, F32)

    cur = f_ref[...].astype(F32)
    pad_ref[HIST:HIST + STEP_ROWS, :] = cur
    prev = pltpu.roll(pad_ref[...], 1, 0)[HIST:, :]
    pad_ref[0:HIST, :] = pad_ref[STEP_ROWS:STEP_ROWS + HIST, :]
    x = cur + (prev - cur) * mu_ref[...]
    r, k, v = x[:, 0:d], x[:, d:2 * d], x[:, 2 * d:3 * d]
    lora_in = x[:, 3 * d:3 * d + LANES]
    xg = x[:, 3 * d + LANES:3 * d + 2 * LANES]

    e_mat, et_mat = e_ref[...], et_ref[...]
    w = -_softplus(-(w0_ref[...] + _bdot(jnp.tanh(lora_in), w2_ref[...]))) - 0.5
    ld = -jnp.exp(w)
    a = _sigmoid(a0_ref[...] + _bdot(lora_in, a2_ref[...]))
    gate = _bdot(_sigmoid(xg), g2_ref[...])
    kk = k * kk_ref[...]
    nrm = jnp.maximum(jnp.sqrt(_sel_r(kk * kk, et_mat, 1)), 1e-12)
    kk = kk * _sel_r(1.0 / nrm, e_mat, 2)
    k2 = k * (1.0 + (a - 1.0) * ka_ref[...])
    cw = _chunk_cumsum(ld)
    e_neg = jnp.exp(-cw)
    a_t = -kk * jnp.exp(cw - ld)
    r_t = r * jnp.exp(cw)
    b_t = kk * a * e_neg
    k_t = k2 * e_neg

    lane = lax.broadcasted_iota(jnp.int32, (CHUNK, LANES), 1)
    m0 = lane < RWKV_HEAD
    row2 = lax.broadcasted_iota(jnp.int32, (LANES, LANES), 0)
    col2 = lax.broadcasted_iota(jnp.int32, (LANES, LANES), 1)
    same = (row2 < CHUNK) == (col2 < RWKV_HEAD)
    row4 = lax.broadcasted_iota(jnp.int32, (2 * CHUNK, 2 * LANES), 0)
    col4 = lax.broadcasted_iota(jnp.int32, (2 * CHUNK, 2 * LANES), 1)
    tt, ss = row4 & (CHUNK - 1), col4 & (CHUNK - 1)
    causal4 = (ss < tt) | ((row4 >= CHUNK) & (ss == tt))

    def stack(p):
        return jnp.concatenate([jnp.where(m0, p, 0.0), jnp.where(m0, 0.0, p)], axis=0)

    pairs = range(n_pairs)
    sls = [slice(j * LANES, (j + 1) * LANES) for j in pairs]
    bf = lambda xs: [x.astype(BF16) for x in xs]
    mt = [st_ref[j] for j in pairs]
    for sub in range(STEP_CHUNKS):
        rs = slice(sub * CHUNK, (sub + 1) * CHUNK)
        w_last = jnp.exp(cw[(sub + 1) * CHUNK - 1:(sub + 1) * CHUNK, :])
        ar = bf(jnp.concatenate([a_t[rs, sl], r_t[rs, sl]], axis=0) for sl in sls)
        qmat = bf(jnp.concatenate([b_t[rs, sl], k_t[rs, sl]], axis=0) for sl in sls)
        qbd = bf(jnp.concatenate([stack(b_t[rs, sl]), stack(k_t[rs, sl])], axis=0) for sl in sls)
        vbd = bf(stack(v[rs, sl]) for sl in sls)
        mtb = bf(mt)
        gh = [jnp.where(causal4, _dot(ar[j], qbd[j], _NT), 0.0) for j in pairs]
        n_ab = [g[:CHUNK, :LANES] for g in gh]
        a_rb = [g[CHUNK:, :LANES] for g in gh]
        akrk = bf(g[:, LANES:] for g in gh)
        sm = [_dot(ar[j], mtb[j], _NT) for j in pairs]
        av = [_dot(akrk[j], vbd[j]) for j in pairs]
        xm = [sm[j][:CHUNK] + av[j][:CHUNK] for j in pairs]
        tp = n_ab
        pwf = [_bdot(x, stack(x)) for x in n_ab]
        n_steps = int(math.log2(CHUNK)) - 1
        for step in range(n_steps):
            pbd = bf(stack(x) for x in pwf)
            if step + 1 < n_steps:
                both = [_bdot(jnp.concatenate([pwf[j], tp[j]], axis=0), pbd[j]) for j in pairs]
                tp = [tp[j] + pwf[j] + both[j][CHUNK:] for j in pairs]
                pwf = [both[j][:CHUNK] for j in pairs]
            else:
                tp = [tp[j] + pwf[j] + _bdot(tp[j], pbd[j]) for j in pairs]
        u = [xm[j] + _bdot(tp[j], stack(xm[j])) for j in pairs]
        for j in pairs:
            y_ref[rs, sls[j]] = sm[j][CHUNK:] + av[j][CHUNK:] + _bdot(a_rb[j], stack(u[j]))
        uv = [jnp.concatenate([u[j], v[rs, sls[j]]], axis=0) for j in pairs]
        upd = [jnp.where(same, _bdot(uv[j], qmat[j], _TN), 0.0) for j in pairs]
        mt = [(mt[j] + upd[j]) * w_last[:, sls[j]] for j in pairs]
    for j in pairs:
        st_ref[j] = mt[j]

    y = y_ref[...]
    inv_n = 1.0 / RWKV_HEAD
    yc = y - _sel_r(_sel_r(y, et_mat, 1) * inv_n, e_mat, 2)
    rs = lax.rsqrt(_sel_r(yc * yc, et_mat, 1) * inv_n + RWKV_GN_EPS)
    yn = yc * _sel_r(rs, e_mat, 2) * lnw_ref[...] + lnb_ref[...]
    bonus = _sel_r(_sel_r(r * k2 * rk_ref[...], et_mat, 1), e_mat, 2) * v
    o_ref[...] = ((yn + bonus) * gate).astype(o_ref.dtype)


def _rwkv(feat, mu, w0, w2, a0, a2, g2, k_k, k_a, r_k, ln_w, ln_b, batch):
    t = feat.shape[0]
    nc = t // batch // STEP_ROWS
    d = D_MODEL
    row = lambda w: pl.BlockSpec((STEP_ROWS, w), lambda b, c: (b * nc + c, 0))
    vec = lambda v: v.astype(F32).reshape(1, -1)
    w2p = jnp.concatenate([w2, jnp.zeros_like(w2)], axis=0).astype(BF16)
    a2p = jnp.concatenate([jnp.zeros_like(a2), a2], axis=0).astype(BF16)
    e_mat = _head_expand(RWKV_HEADS, RWKV_HEAD)
    return pl.pallas_call(
        _rwkv_kernel,
        grid=(batch, nc),
        in_specs=[row(RWKV_COLS), _full((1, RWKV_COLS)), _full((1, d)), _full((LANES, d)), _full((1, d)),
                  _full((LANES, d)), _full((LANES, d)), _full((1, d)), _full((1, d)), _full((1, d)),
                  _full((1, d)), _full((1, d)), _full((LANES, d)), _full((d, LANES))],
        out_specs=row(d),
        out_shape=jax.ShapeDtypeStruct((t, d), BF16),
        scratch_shapes=[pltpu.VMEM((STEP_ROWS + HIST, RWKV_COLS), F32),
                        pltpu.VMEM((RWKV_HEADS // 2, LANES, LANES), F32),
                        pltpu.VMEM((STEP_ROWS, d), F32)],
        compiler_params=_cparams(("parallel", "arbitrary")),
        name="rwkv7",
    )(feat, vec(mu), vec(w0), w2p, vec(a0), a2p, g2.astype(BF16), vec(k_k), vec(k_a), vec(r_k), vec(ln_w),
      vec(ln_b), e_mat, e_mat.T)


def _merge_kernel(ya_ref, yb_ref, yc_ref, g_ref, h_ref, wa_ref, wb_ref, wc_ref, wo_ref, lg_ref, lb_ref,
                  rw_ref, rb_ref, o_ref, gt_ref):
    d = D_MODEL
    gate = lambda j: _sigmoid(g_ref[:, j * d:(j + 1) * d].astype(F32))
    m = (gate(0) * _dot(ya_ref[...], wa_ref[...]) + gate(1) * _dot(yb_ref[...], wb_ref[...])
         + gate(2) * _dot(yc_ref[...], wc_ref[...]))
    hn = DN_ALPHA * h_ref[...] + _bdot(m, wo_ref[...])
    y = _layernorm_rows(hn, lg_ref[...], lb_ref[...])
    o_ref[...] = y
    _route(y, rw_ref, rb_ref, gt_ref)


def _merge(ya, yb, yc, gates, h, wa, wb, wc, wo, ln_g, ln_b, router_w, router_bias, tm=512):
    t, d = h.shape
    tm = min(tm, t)
    row = lambda w: pl.BlockSpec((tm, w), lambda i: (i, 0))
    return pl.pallas_call(
        _merge_kernel,
        grid=(t // tm,),
        in_specs=[row(d), row(d), row(d), row(3 * d), row(d)] + [_full((d, d))] * 4 + [_full((1, d))] * 2
        + [_full((N_EXPERTS, d)), _full((N_EXPERTS, 1))],
        out_specs=[row(d), pl.BlockSpec((N_EXPERTS, tm), lambda i: (0, i))],
        out_shape=[jax.ShapeDtypeStruct((t, d), F32), jax.ShapeDtypeStruct((N_EXPERTS, t), F32)],
        compiler_params=_cparams(("parallel",)),
        name="merge_ln_route",
    )(ya, yb, yc, gates, h, wa.astype(BF16), wb.astype(BF16), wc.astype(BF16), wo.astype(BF16),
      ln_g.reshape(1, d), ln_b.reshape(1, d), router_w.T, router_bias.reshape(N_EXPERTS, 1).astype(F32))


def _route(h, wt_ref, bias_ref, o_ref):
    logits = lax.dot_general(wt_ref[...], h, _NT, precision=lax.Precision.HIGHEST,
                             preferred_element_type=F32)
    mx = jnp.max(logits, axis=0, keepdims=True)
    ex = jnp.exp(logits - mx)
    probs = ex / jnp.sum(ex, axis=0, keepdims=True)
    sel = probs + bias_ref[...]
    rows = [sel[e:e + 1, :] for e in range(N_EXPERTS)]
    prow = [probs[e:e + 1, :] for e in range(N_EXPERTS)]
    gscore = []
    for g in range(N_GROUPS):
        m = rows[g * EXPERTS_PER_GROUP:(g + 1) * EXPERTS_PER_GROUP]
        best = None
        for i in range(EXPERTS_PER_GROUP):
            for j in range(i + 1, EXPERTS_PER_GROUP):
                s = m[i] + m[j]
                best = s if best is None else jnp.maximum(best, s)
        gscore.append(best)
    chosen = []
    for g in range(N_GROUPS):
        ok = None
        for g2 in range(N_GROUPS):
            if g2 == g:
                continue
            t = (gscore[g] > gscore[g2]) if g2 < g else (gscore[g] >= gscore[g2])
            ok = t if ok is None else (ok & t)
        chosen.append(ok)
    picked = []
    for e in range(N_EXPERTS):
        g = e // EXPERTS_PER_GROUP
        rank = None
        for e2 in range(g * EXPERTS_PER_GROUP, (g + 1) * EXPERTS_PER_GROUP):
            if e2 == e:
                continue
            ahead = (rows[e2] >= rows[e]) if e2 < e else (rows[e2] > rows[e])
            ahead = ahead.astype(F32)
            rank = ahead if rank is None else rank + ahead
        picked.append(jnp.where(chosen[g] & (rank < 2.0), prow[e], 0.0))
    tot = picked[0]
    for e in range(1, N_EXPERTS):
        tot = tot + picked[e]
    inv = 1.0 / tot
    for e in range(N_EXPERTS):
        o_ref[e:e + 1, :] = picked[e] * inv


MOE_TILE = 256
MOE_ISSUE_UNROLLS = (32, 4, 1)
_PAIRS = [(a, b) for a in range(EXPERTS_PER_GROUP) for b in range(a + 1, EXPERTS_PER_GROUP)]
N_CLASSES = N_GROUPS * len(_PAIRS)


def _moe_num_tiles(t):
    return -(-(t + N_CLASSES * (MOE_TILE - 1)) // MOE_TILE)


def _route_meta(gates_t):
    e_n, t = gates_t.shape
    i32 = jnp.int32
    mask = gates_t > 0
    eidx = lax.broadcasted_iota(i32, (e_n, t), 0)
    e0 = jnp.minimum(jnp.min(jnp.where(mask, eidx, e_n), axis=0), e_n - 1)
    e1 = jnp.maximum(jnp.max(jnp.where(mask, eidx, -1), axis=0), e0)
    grp = e0 // EXPERTS_PER_GROUP
    a = e0 % EXPERTS_PER_GROUP
    b = jnp.where(e1 // EXPERTS_PER_GROUP == grp, e1 % EXPERTS_PER_GROUP, a)
    b = jnp.where(b == a, (a + 1) % EXPERTS_PER_GROUP, b)
    lo, hi = jnp.minimum(a, b), jnp.maximum(a, b)
    n_pairs = len(_PAIRS)
    cls = grp * n_pairs + (lo * (2 * EXPERTS_PER_GROUP - 1 - lo)) // 2 + (hi - lo - 1)
    g_lo = jnp.take_along_axis(gates_t, (grp * EXPERTS_PER_GROUP + lo)[None, :], axis=0)[0]
    g_hi = jnp.take_along_axis(gates_t, (grp * EXPERTS_PER_GROUP + hi)[None, :], axis=0)[0]
    onehot = (cls[None, :] == jnp.arange(N_CLASSES, dtype=i32)[:, None]).astype(i32)
    csum = jnp.cumsum(onehot, axis=1)
    cnt = csum[:, -1]
    rank = jnp.sum(onehot * (csum - 1), axis=0)
    ptiles = (cnt + MOE_TILE - 1) // MOE_TILE
    tile_end = jnp.cumsum(ptiles)
    tile_off = tile_end - ptiles
    pos = jnp.take(tile_off, cls) * MOE_TILE + rank
    n_tiles = _moe_num_tiles(t)
    as_i32 = lambda g: lax.bitcast_convert_type(g.astype(F32), i32)
    per_token = jnp.stack([jnp.arange(t, dtype=i32), as_i32(g_lo), as_i32(g_hi)], axis=1)
    per_row = jnp.zeros((n_tiles * MOE_TILE, 3), i32).at[pos].set(per_token, unique_indices=True)
    row_token = per_row[:, 0]
    row_gates = lax.bitcast_convert_type(per_row[:, 1:3], F32)
    tid = jnp.arange(n_tiles, dtype=i32)
    tcls = jnp.minimum(jnp.sum((tid[:, None] >= tile_end[None, :]).astype(i32), axis=1), N_CLASSES - 1)
    n_valid = jnp.clip(jnp.take(cnt, tcls) - (tid - jnp.take(tile_off, tcls)) * MOE_TILE, 0, MOE_TILE)
    n_valid = jnp.where(tid < tile_end[-1], n_valid, 0).astype(i32)
    cls_lo = jnp.array([g * EXPERTS_PER_GROUP + p[0] for g in range(N_GROUPS) for p in _PAIRS], i32)
    cls_hi = jnp.array([g * EXPERTS_PER_GROUP + p[1] for g in range(N_GROUPS) for p in _PAIRS], i32)
    return row_token, jnp.take(cls_lo, tcls), jnp.take(cls_hi, tcls), n_valid, row_gates


def _moe_kernel(rt_ref, ea_ref, eb_ref, nv_ref, h_hbm, rg_ref, wga_ref, wua_ref, wda_ref, wgb_ref, wub_ref,
                wdb_ref, lg_ref, lb_ref, out_hbm, xbuf, obuf, wup_buf, wdn_buf, sem_in, sem_out):
    i = pl.program_id(0)
    n = pl.num_programs(0)
    slot = lax.rem(i, 2)

    def rows_in(row, s, k, rows=1):
        return pltpu.make_async_copy(h_hbm.at[pl.ds(row, rows)], xbuf.at[s, pl.ds(k, rows)], sem_in.at[s])

    def rows_out(row, s, k, rows=1):
        return pltpu.make_async_copy(obuf.at[s, pl.ds(k, rows)], out_hbm.at[pl.ds(row, rows)], sem_out.at[s])

    def issue(copy_at, tile, count):
        base = tile * MOE_TILE

        @pl.when(count == MOE_TILE)
        def _():
            for k in range(MOE_TILE):
                copy_at(base, k).start()

        @pl.when(count < MOE_TILE)
        def _():
            done = 0
            for unroll in MOE_ISSUE_UNROLLS:
                def group(q, c, unroll=unroll, done=done):
                    for r in range(unroll):
                        copy_at(base, done + q * unroll + r).start()
                    return c
                groups = lax.shift_right_logical(count - done, int(math.log2(unroll)))
                lax.fori_loop(0, groups, group, 0)
                done = done + groups * unroll

    def wait_rows(copies, count):
        rows = MOE_TILE
        while rows >= 1:
            @pl.when((count & rows) != 0)
            def _(rows=rows):
                copies(0, 0, rows).wait()
            rows //= 2

    def start_gather(tile, s):
        issue(lambda base, k: rows_in(rt_ref[base + k], s, k), tile, nv_ref[tile])

    @pl.when(i == 0)
    def _():
        xbuf[...] = jnp.zeros(xbuf.shape, F32)
        start_gather(0, 0)

    @pl.when(i + 1 < n)
    def _():
        start_gather(i + 1, 1 - slot)

    wait_rows(lambda row, k, rows: rows_in(row, slot, k, rows), nv_ref[i])

    @pl.when(i >= 2)
    def _():
        wait_rows(lambda row, k, rows: rows_out(row, slot, k, rows), nv_ref[i - 2])

    @pl.when(nv_ref[i] > 0)
    def _():
        prev = jnp.maximum(i - 1, 0)
        for which, e_ref, (g_ref, u_ref, d_ref) in ((0, ea_ref, (wga_ref, wua_ref, wda_ref)),
                                                    (1, eb_ref, (wgb_ref, wub_ref, wdb_ref))):
            @pl.when((i == 0) | (e_ref[i] != e_ref[prev]))
            def _(which=which, g_ref=g_ref, u_ref=u_ref, d_ref=d_ref):
                wup_buf[2 * which] = g_ref[0, 0].astype(BF16)
                wup_buf[2 * which + 1] = u_ref[0, 0].astype(BF16)
                wdn_buf[which] = d_ref[0, 0].astype(BF16)

        x = xbuf[slot]
        xb = x.astype(BF16)
        rg = rg_ref[...]
        act_a = (_silu(_dot(xb, wup_buf[0])) * _dot(xb, wup_buf[1])).astype(BF16)
        act_b = (_silu(_dot(xb, wup_buf[2])) * _dot(xb, wup_buf[3])).astype(BF16)
        moe = rg[:, 0:1] * _dot(act_a, wdn_buf[0]) + rg[:, 1:2] * _dot(act_b, wdn_buf[1])
        obuf[slot] = _layernorm_rows(DN_ALPHA * x + moe, lg_ref[...], lb_ref[...])
        issue(lambda base, k: rows_out(rt_ref[base + k], slot, k), i, nv_ref[i])

    @pl.when(i == n - 1)
    def _():
        @pl.when(i >= 1)
        def _():
            wait_rows(lambda row, k, rows: rows_out(row, 1 - slot, k, rows), nv_ref[i - 1])
        wait_rows(lambda row, k, rows: rows_out(row, slot, k, rows), nv_ref[i])


def _moe(h, gates_t, wg, wu, wd, layer, ln_g, ln_b):
    t, d = h.shape
    n_tiles = _moe_num_tiles(t)
    row_token, exp_a, exp_b, n_valid, row_gates = _route_meta(gates_t)
    w_up = lambda which: pl.BlockSpec((1, 1, d, D_EXPERT),
                                      lambda i, rt, ea, eb, nv: (layer, (ea, eb)[which][i], 0, 0))
    w_dn = lambda which: pl.BlockSpec((1, 1, D_EXPERT, d),
                                      lambda i, rt, ea, eb, nv: (layer, (ea, eb)[which][i], 0, 0))
    const = lambda shape: pl.BlockSpec(shape, lambda i, *_: (0,) * len(shape))
    grid_spec = pltpu.PrefetchScalarGridSpec(
        num_scalar_prefetch=4,
        grid=(n_tiles,),
        in_specs=[pl.BlockSpec(memory_space=pl.ANY),
                  pl.BlockSpec((MOE_TILE, 2), lambda i, *_: (i, 0)),
                  w_up(0), w_up(0), w_dn(0), w_up(1), w_up(1), w_dn(1), const((1, d)), const((1, d))],
        out_specs=pl.BlockSpec(memory_space=pl.ANY),
        scratch_shapes=[pltpu.VMEM((2, MOE_TILE, d), F32), pltpu.VMEM((2, MOE_TILE, d), F32),
                        pltpu.VMEM((4, d, D_EXPERT), BF16), pltpu.VMEM((2, D_EXPERT, d), BF16),
                        pltpu.SemaphoreType.DMA((2,)), pltpu.SemaphoreType.DMA((2,))],
    )
    return pl.pallas_call(
        _moe_kernel,
        grid_spec=grid_spec,
        out_shape=jax.ShapeDtypeStruct((t, d), F32),
        compiler_params=_cparams(("arbitrary",)),
        name="experts_ln",
    )(row_token, exp_a, exp_b, n_valid, h, row_gates, wg, wu, wd, wg, wu, wd, ln_g.reshape(1, d), ln_b.reshape(1, d))


def kernel(x, ln_in_g, ln_in_b, w_in, ssd_conv_w, ssd_conv_b, ssd_dt_bias, ssd_a_log, ssd_d, ssd_norm_w, rwkv_mu, rwkv_w0, rwkv_w2, rwkv_a0, rwkv_a2, rwkv_g2, rwkv_k_k, rwkv_k_a, rwkv_r_k, rwkv_ln_w, rwkv_ln_b, hgrn_lb, hgrn_norm_w, w_br_ssd, w_br_rwkv, w_br_hgrn, w_out, ln1_g, ln1_b, router_w, router_bias, exp_w_gate, exp_w_up, exp_w_down, ln2_g, ln2_b):
    batch, seq, d = x.shape
    t = batch * seq
    h, h_bf = _layernorm(x.reshape(t, d), ln_in_g, ln_in_b)
    lsm = jax.nn.softmax(hgrn_lb.astype(F32), axis=0)
    lower_bounds = jnp.cumsum(lsm, axis=0) - lsm[0]
    w_all = _pack_in_weights(w_in)
    for l in range(DEPTH):
        z, dt_raw = _project_z_dt(h_bf, w_all, l)
        xbc, f_rwkv, f_hgrn, gates = [_project(h_bf, w_all, l, seg) for seg in ("xbc", "rwkv", "hgrn", "gates")]
        y_a = _ssd(z, xbc, dt_raw, ssd_conv_w[l], ssd_conv_b[l], ssd_dt_bias[l], ssd_a_log[l], ssd_d[l],
                   ssd_norm_w[l], batch)
        y_b = _rwkv(f_rwkv, rwkv_mu[l], rwkv_w0[l], rwkv_w2[l], rwkv_a0[l], rwkv_a2[l], rwkv_g2[l],
                    rwkv_k_k[l], rwkv_k_a[l], rwkv_r_k[l].reshape(-1), rwkv_ln_w[l], rwkv_ln_b[l], batch)
        y_c = _hgrn(f_hgrn, lower_bounds[l], hgrn_norm_w[l], batch)
        h, gates_t = _merge(y_a, y_b, y_c, gates, h, w_br_ssd[l], w_br_rwkv[l], w_br_hgrn[l], w_out[l],
                            ln1_g[l], ln1_b[l], router_w, router_bias)
        h = _moe(h, gates_t, exp_w_gate, exp_w_up, exp_w_down, l, ln2_g[l], ln2_b[l])
        h_bf = h.astype(BF16)
    return h.reshape(batch, seq, d)
```

```python
import math

import jax
import jax.numpy as jnp
from jax import lax
from jax.experimental import pallas as pl
from jax.experimental.pallas import tpu as pltpu

F32 = jnp.float32
BF16 = jnp.bfloat16

D_MODEL = 1024
DEPTH = 2
CHUNK = 64
STEP_CHUNKS = 4
STEP_ROWS = STEP_CHUNKS * CHUNK
LANES = 128
HIST = 8

SSD_HEADS = 16
SSD_HEADDIM = 64
SSD_GROUPS = 2
SSD_STATE = 128
SSD_CONV = 4
SSD_XBC = D_MODEL + 2 * SSD_GROUPS * SSD_STATE

RWKV_HEADS = 16
RWKV_HEAD = 64
RWKV_COLS = 3 * D_MODEL + 64 + 64 + 128
RWKV_GN_EPS = 64e-5

HGRN_HEADS = 8
HGRN_EXPAND = 128

IN_SIZES = (D_MODEL, SSD_XBC, SSD_HEADS, RWKV_COLS, 4 * D_MODEL, 3 * D_MODEL)

N_EXPERTS = 16
N_GROUPS = 4
EXPERTS_PER_GROUP = 4
D_EXPERT = 512

DN_ALPHA = (2 * DEPTH) ** 0.25
LN_EPS = 1e-5
RMS_EPS = 1e-6

VMEM_LIMIT = 56 * 1024 * 1024

_NN = (((1,), (0,)), ((), ()))
_NT = (((1,), (1,)), ((), ()))
_TN = (((0,), (0,)), ((), ()))


def _dot(a, b, dims=_NN):
    return lax.dot_general(a, b, dims, preferred_element_type=F32)


def _bdot(a, b, dims=_NN):
    return _dot(a.astype(BF16), b.astype(BF16), dims)


def _split3(x):
    x1 = x.astype(BF16)
    r1 = x - x1.astype(F32)
    x2 = r1.astype(BF16)
    x3 = (r1 - x2.astype(F32)).astype(BF16)
    return x1, x2, x3


def _sel_r(x, sel, pieces=3):
    n = x.shape[0]
    if pieces == 1:
        return _bdot(x, sel)
    y = _dot(jnp.concatenate(_split3(x)[:pieces], axis=0), sel)
    out = y[:n] + y[n:2 * n]
    return out + y[2 * n:] if pieces == 3 else out


def _chunk_cumsum(x):
    row = lax.broadcasted_iota(jnp.int32, x.shape, 0) & (CHUNK - 1)
    shift = 1
    while shift < CHUNK:
        x = x + jnp.where(row >= shift, pltpu.roll(x, shift, 0), 0.0)
        shift *= 2
    return x


def _sigmoid(x):
    return 1.0 / (1.0 + jnp.exp(-x))


def _silu(x):
    return x * _sigmoid(x)


def _softplus(x):
    return jnp.maximum(x, 0.0) + jnp.log1p(jnp.exp(-jnp.abs(x)))


def _layernorm_rows(x, g, b):
    mu = jnp.mean(x, axis=-1, keepdims=True)
    xc = x - mu
    var = jnp.mean(xc * xc, axis=-1, keepdims=True)
    return xc * lax.rsqrt(var + LN_EPS) * g + b


def _tri(n, strict=False):
    r = lax.broadcasted_iota(jnp.int32, (n, n), 0)
    c = lax.broadcasted_iota(jnp.int32, (n, n), 1)
    return (c < r) if strict else (c <= r)


def _cparams(sem):
    return pltpu.CompilerParams(dimension_semantics=sem, vmem_limit_bytes=VMEM_LIMIT)


def _full(shape):
    return pl.BlockSpec(shape, lambda *_: (0,) * len(shape))


def _ln_kernel(x_ref, g_ref, b_ref, o_ref, ob_ref):
    y = _layernorm_rows(x_ref[...], g_ref[...], b_ref[...])
    o_ref[...] = y
    ob_ref[...] = y.astype(BF16)


def _layernorm(x, g, b, tm=512):
    t, d = x.shape
    row = pl.BlockSpec((tm, d), lambda i: (i, 0))
    return pl.pallas_call(
        _ln_kernel,
        grid=(t // tm,),
        in_specs=[row, _full((1, d)), _full((1, d))],
        out_specs=[row, row],
        out_shape=[jax.ShapeDtypeStruct((t, d), F32), jax.ShapeDtypeStruct((t, d), BF16)],
        compiler_params=_cparams(("parallel",)),
        name="layernorm_in",
    )(x, g.reshape(1, d), b.reshape(1, d))


def _proj_kernel(x_ref, w_ref, o_ref):
    o_ref[...] = _dot(x_ref[...], w_ref[0]).astype(o_ref.dtype)


_IN_LAYOUT = (("hgrn", 4, 4 * D_MODEL, 2048), ("z", 0, D_MODEL, 1024), ("pad0", None, 1024, 1024),
              ("xbc", 1, SSD_XBC, SSD_XBC), ("gates", 5, 3 * D_MODEL, SSD_XBC), ("pad1", None, 896, 128),
              ("rwkv", 3, RWKV_COLS, RWKV_COLS // 2), ("dt", 2, LANES, LANES))


def _in_layout_offsets():
    offs, acc = {}, 0
    for name, _, width, tn in _IN_LAYOUT:
        assert acc % tn == 0 and width % tn == 0, name
        offs[name] = acc
        acc += width
    return offs, acc


def _pack_in_weights(w_in):
    src, acc = [], 0
    for s in IN_SIZES:
        src.append((acc, s))
        acc += s
    parts = []
    for _, idx, width, _ in _IN_LAYOUT:
        if idx is None:
            parts.append(jnp.zeros(w_in.shape[:2] + (width,), w_in.dtype))
            continue
        start, size = src[idx]
        part = w_in[:, :, start:start + size]
        if size < width:
            part = jnp.pad(part, ((0, 0), (0, 0), (0, width - size)))
        parts.append(part)
    return jnp.concatenate(parts, axis=-1).astype(BF16)


def _project(x_bf, w_all, layer, seg, out_dtype=BF16, tm=1024):
    t, k = x_bf.shape
    tm = min(tm, t)
    offs, _ = _in_layout_offsets()
    n, tn = next((width, tn) for name, _, width, tn in _IN_LAYOUT if name == seg)
    first = offs[seg] // tn
    return pl.pallas_call(
        _proj_kernel,
        grid=(n // tn, t // tm),
        in_specs=[pl.BlockSpec((tm, k), lambda j, i: (i, 0)),
                  pl.BlockSpec((1, k, tn), lambda j, i: (layer, 0, first + j))],
        out_specs=pl.BlockSpec((tm, tn), lambda j, i: (i, j)),
        out_shape=jax.ShapeDtypeStruct((t, n), out_dtype),
        compiler_params=_cparams(("parallel", "parallel")),
        name=f"proj_{seg}",
    )(x_bf, w_all)


def _proj_z_dt_kernel(x_ref, wz_ref, wdt_ref, z_ref, dt_ref):
    x = x_ref[...]
    z_ref[...] = _dot(x, wz_ref[0]).astype(z_ref.dtype)
    dt_ref[...] = _dot(x, wdt_ref[0])


def _project_z_dt(x_bf, w_all, layer, tm=1024):
    t, k = x_bf.shape
    tm = min(tm, t)
    offs, _ = _in_layout_offsets()
    wspec = lambda seg, n: pl.BlockSpec((1, k, n), lambda i: (layer, 0, offs[seg] // n))
    return pl.pallas_call(
        _proj_z_dt_kernel,
        grid=(t // tm,),
        in_specs=[pl.BlockSpec((tm, k), lambda i: (i, 0)), wspec("z", D_MODEL), wspec("dt", LANES)],
        out_specs=[pl.BlockSpec((tm, D_MODEL), lambda i: (i, 0)), pl.BlockSpec((tm, LANES), lambda i: (i, 0))],
        out_shape=[jax.ShapeDtypeStruct((t, D_MODEL), BF16), jax.ShapeDtypeStruct((t, LANES), F32)],
        compiler_params=_cparams(("parallel",)),
        name="proj_z_dt",
    )(x_bf, w_all, w_all)


def _head_expand(n_heads, width):
    h = lax.broadcasted_iota(jnp.int32, (LANES, n_heads * width), 0)
    c = lax.broadcasted_iota(jnp.int32, (LANES, n_heads * width), 1)
    return (c // width == h).astype(BF16)


def _ssd_kernel(z_ref, xbc_ref, dt_ref, cw_ref, cb_ref, dtb_ref, a_ref, dsk_ref, nw_ref, e_ref,
                o_ref, pad_ref, st_ref):
    c = pl.program_id(1)
    n_pairs = SSD_HEADS // 2

    @pl.when(c == 0)
    def _():
        pad_ref[0:HIST, :] = jnp.zeros((HIST, SSD_XBC), F32)
        st_ref[...] = jnp.zeros(st_ref.shape, F32)

    pad_ref[HIST:HIST + STEP_ROWS, :] = xbc_ref[...].astype(F32)
    padded = pad_ref[...]
    acc = cb_ref[...] + cw_ref[SSD_CONV - 1:SSD_CONV, :] * padded[HIST:, :]
    for shift in range(1, SSD_CONV):
        j = SSD_CONV - 1 - shift
        acc = acc + cw_ref[j:j + 1, :] * pltpu.roll(padded, shift, 0)[HIST:, :]
    pad_ref[0:HIST, :] = pad_ref[STEP_ROWS:STEP_ROWS + HIST, :]
    xbc = _silu(acc)
    xs = xbc[:, :D_MODEL]

    e_mat = e_ref[...]
    dt = _softplus(dt_ref[...] + dtb_ref[...])
    acs = _chunk_cumsum(dt * a_ref[...])
    acs_all = _sel_r(acs, e_mat)
    xs_dt_all = xs * _sel_r(dt, e_mat)

    li = lax.broadcasted_iota(jnp.int32, (CHUNK, D_MODEL), 0)
    si = lax.broadcasted_iota(jnp.int32, (CHUNK, D_MODEL), 1) & (SSD_HEADDIM - 1)
    lane = lax.broadcasted_iota(jnp.int32, (CHUNK, LANES), 1)
    m0 = lane < SSD_HEADDIM

    pairs = range(n_pairs)
    sls = [slice(j * LANES, (j + 1) * LANES) for j in pairs]
    grp = [j // (n_pairs // SSD_GROUPS) for j in pairs]
    st = [st_ref[j] for j in pairs]
    y_rows = []
    for sub in range(STEP_CHUNKS):
        rs = slice(sub * CHUNK, (sub + 1) * CHUNK)
        acs_e, xs_dt = acs_all[rs], xs_dt_all[rs]
        acs_row = jnp.sum(jnp.where(li == si, acs_e, 0.0), axis=0, keepdims=True)
        decay = jnp.exp(jnp.where(si <= li, acs_e - acs_row, -jnp.inf))
        exp_acs = jnp.exp(acs_e)
        acs_last = acs_e[CHUNK - 1:CHUNK, :]
        to_end = jnp.exp(acs_last - acs_e)
        exp_last = jnp.exp(acs_last)
        bm = [xbc[rs, D_MODEL + g * SSD_STATE:D_MODEL + (g + 1) * SSD_STATE].astype(BF16)
              for g in range(SSD_GROUPS)]
        cm = [xbc[rs, D_MODEL + (SSD_GROUPS + g) * SSD_STATE:D_MODEL + (SSD_GROUPS + g + 1) * SSD_STATE
                  ].astype(BF16) for g in range(SSD_GROUPS)]
        cb2 = [_dot(cm[g], jnp.concatenate([bm[g], bm[g]], axis=0), _NT) for g in range(SSD_GROUPS)]
        xp = [xs_dt[:, sl] for sl in sls]
        xbd = [jnp.concatenate([jnp.where(m0, x, 0.0), jnp.where(m0, 0.0, x)], axis=0) for x in xp]
        y_parts = [_bdot(cb2[grp[j]] * decay[:, sls[j]], xbd[j]) + _bdot(cm[grp[j]], st[j]) * exp_acs[:, sls[j]]
                   for j in pairs]
        upd = [_bdot(bm[grp[j]], xp[j] * to_end[:, sls[j]], _TN) for j in pairs]
        st = [st[j] * exp_last[:, sls[j]] + upd[j] for j in pairs]
        y_rows.append(jnp.concatenate(y_parts, axis=1))
    for j in pairs:
        st_ref[j] = st[j]
    y = jnp.concatenate(y_rows, axis=0) + xs * dsk_ref[...]
    y = y * _silu(z_ref[...].astype(F32))
    gw = D_MODEL // SSD_GROUPS
    for g in range(SSD_GROUPS):
        yg = y[:, g * gw:(g + 1) * gw]
        ms = jnp.mean(yg * yg, axis=-1, keepdims=True)
        o_ref[:, g * gw:(g + 1) * gw] = (yg * lax.rsqrt(ms + RMS_EPS) * nw_ref[:, g * gw:(g + 1) * gw]
                                         ).astype(o_ref.dtype)


def _ssd(z, xbc, dt_raw, conv_w, conv_b, dt_bias, a_log, d_skip, norm_w, batch):
    t = z.shape[0]
    nc = t // batch // STEP_ROWS
    pad16 = lambda v: jnp.pad(v.astype(F32), (0, LANES - SSD_HEADS)).reshape(1, LANES)
    row = lambda w: pl.BlockSpec((STEP_ROWS, w), lambda b, c: (b * nc + c, 0))
    return pl.pallas_call(
        _ssd_kernel,
        grid=(batch, nc),
        in_specs=[row(D_MODEL), row(SSD_XBC), row(LANES), _full((SSD_CONV, SSD_XBC)), _full((1, SSD_XBC)),
                  _full((1, LANES)), _full((1, LANES)), _full((1, D_MODEL)), _full((1, D_MODEL)),
                  _full((LANES, D_MODEL))],
        out_specs=row(D_MODEL),
        out_shape=jax.ShapeDtypeStruct((t, D_MODEL), BF16),
        scratch_shapes=[pltpu.VMEM((STEP_ROWS + HIST, SSD_XBC), F32),
                        pltpu.VMEM((SSD_HEADS // 2, SSD_STATE, LANES), F32)],
        compiler_params=_cparams(("parallel", "arbitrary")),
        name="ssd",
    )(z, xbc, dt_raw, conv_w, conv_b.reshape(1, -1), pad16(dt_bias), pad16(-jnp.exp(a_log.astype(F32))),
      jnp.repeat(d_skip.astype(F32), SSD_HEADDIM).reshape(1, -1), norm_w.reshape(1, -1),
      _head_expand(SSD_HEADS, SSD_HEADDIM))


def _hgrn_kernel(f_ref, lb_ref, nw_ref, o_ref, st_ref):
    c = pl.program_id(1)

    @pl.when(c == 0)
    def _():
        st_ref[...] = jnp.zeros(st_ref.shape, F32)

    d = D_MODEL
    lb = lb_ref[...]
    q = _silu(f_ref[:, 0:d].astype(F32))
    forget = lb + (1.0 - lb) * _sigmoid(f_ref[:, d:2 * d].astype(F32))
    k = 1.0 - forget
    bc_all = _chunk_cumsum(jnp.log(forget))
    qd_all = q * jnp.exp(bc_all)
    causal = _tri(CHUNK)
    heads = range(HGRN_HEADS)
    sls = [slice(h * HGRN_EXPAND, (h + 1) * HGRN_EXPAND) for h in heads]
    st = [st_ref[h] for h in heads]
    for sub in range(STEP_CHUNKS):
        rs = slice(sub * CHUNK, (sub + 1) * CHUNK)
        bc, qs, ks, qd = bc_all[rs], q[rs], k[rs], qd_all[rs]
        mid = bc[CHUNK // 2:CHUNK // 2 + 1, :]
        last = bc[CHUNK - 1:CHUNK, :]
        qe = qs * jnp.exp(bc - mid)
        ke = ks * jnp.exp(mid - bc)
        kd = ks * jnp.exp(last - bc)
        w_last = jnp.exp(last)
        vb = [f_ref[rs, 2 * d + h * HGRN_EXPAND:2 * d + (h + 1) * HGRN_EXPAND].astype(BF16) for h in heads]
        att = [jnp.where(causal, _bdot(qe[:, sl], ke[:, sl], _NT), 0.0) for sl in sls]
        o = [_bdot(att[h], vb[h]) + _bdot(qd[:, sls[h]], st[h], _NT) for h in heads]
        upd = [_bdot(vb[h], kd[:, sls[h]], _TN) for h in heads]
        st = [st[h] * w_last[:, sls[h]] + upd[h] for h in heads]
        for h in heads:
            oh = o[h] * lax.rsqrt(jnp.mean(o[h] * o[h], axis=-1, keepdims=True) + RMS_EPS) * nw_ref[...]
            gate = _sigmoid(f_ref[rs, 3 * d + h * HGRN_EXPAND:3 * d + (h + 1) * HGRN_EXPAND].astype(F32))
            o_ref[rs, sls[h]] = (oh * gate).astype(o_ref.dtype)
    for h in heads:
        st_ref[h] = st[h]


def _hgrn(feat, lb, norm_w, batch):
    t = feat.shape[0]
    nc = t // batch // STEP_ROWS
    row = lambda w: pl.BlockSpec((STEP_ROWS, w), lambda b, c: (b * nc + c, 0))
    return pl.pallas_call(
        _hgrn_kernel,
        grid=(batch, nc),
        in_specs=[row(4 * D_MODEL), _full((1, D_MODEL)), _full((1, HGRN_EXPAND))],
        out_specs=row(D_MODEL),
        out_shape=jax.ShapeDtypeStruct((t, D_MODEL), BF16),
        scratch_shapes=[pltpu.VMEM((HGRN_HEADS, HGRN_EXPAND, HGRN_EXPAND), F32)],
        compiler_params=_cparams(("parallel", "arbitrary")),
        name="hgrn2",
    )(feat, lb.reshape(1, -1), norm_w.reshape(1, -1))


def _rwkv_kernel(f_ref, mu_ref, w0_ref, w2_ref, a0_ref, a2_ref, g2_ref, kk_ref, ka_ref, rk_ref, lnw_ref,
                 lnb_ref, e_ref, et_ref, o_ref, pad_ref, st_ref, y_ref):
    c = pl.program_id(1)
    d = D_MODEL
    n_pairs = RWKV_HEADS // 2

    @pl.when(c == 0)
    def _():
        pad_ref[0:HIST, :] = jnp.zeros((HIST, RWKV_COLS), F32)
        st_ref[...] = jnp.zeros(st_ref.shape, F32)

    cur = f_ref[...].astype(F32)
    pad_ref[HIST:HIST + STEP_ROWS, :] = cur
    prev = pltpu.roll(pad_ref[...], 1, 0)[HIST:, :]
    pad_ref[0:HIST, :] = pad_ref[STEP_ROWS:STEP_ROWS + HIST, :]
    x = cur + (prev - cur) * mu_ref[...]
    r, k, v = x[:, 0:d], x[:, d:2 * d], x[:, 2 * d:3 * d]
    lora_in = x[:, 3 * d:3 * d + LANES]
    xg = x[:, 3 * d + LANES:3 * d + 2 * LANES]

    e_mat, et_mat = e_ref[...], et_ref[...]
    w = -_softplus(-(w0_ref[...] + _bdot(jnp.tanh(lora_in), w2_ref[...]))) - 0.5
    ld = -jnp.exp(w)
    a = _sigmoid(a0_ref[...] + _bdot(lora_in, a2_ref[...]))
    gate = _bdot(_sigmoid(xg), g2_ref[...])
    kk = k * kk_ref[...]
    nrm = jnp.maximum(jnp.sqrt(_sel_r(kk * kk, et_mat, 1)), 1e-12)
    kk = kk * _sel_r(1.0 / nrm, e_mat, 2)
    k2 = k * (1.0 + (a - 1.0) * ka_ref[...])
    cw = _chunk_cumsum(ld)
    e_neg = jnp.exp(-cw)
    a_t = -kk * jnp.exp(cw - ld)
    r_t = r * jnp.exp(cw)
    b_t = kk * a * e_neg
    k_t = k2 * e_neg

    lane = lax.broadcasted_iota(jnp.int32, (CHUNK, LANES), 1)
    m0 = lane < RWKV_HEAD
    row2 = lax.broadcasted_iota(jnp.int32, (LANES, LANES), 0)
    col2 = lax.broadcasted_iota(jnp.int32, (LANES, LANES), 1)
    same = (row2 < CHUNK) == (col2 < RWKV_HEAD)
    row4 = lax.broadcasted_iota(jnp.int32, (2 * CHUNK, 2 * LANES), 0)
    col4 = lax.broadcasted_iota(jnp.int32, (2 * CHUNK, 2 * LANES), 1)
    tt, ss = row4 & (CHUNK - 1), col4 & (CHUNK - 1)
    causal4 = (ss < tt) | ((row4 >= CHUNK) & (ss == tt))

    def stack(p):
        return jnp.concatenate([jnp.where(m0, p, 0.0), jnp.where(m0, 0.0, p)], axis=0)

    pairs = range(n_pairs)
    sls = [slice(j * LANES, (j + 1) * LANES) for j in pairs]
    bf = lambda xs: [x.astype(BF16) for x in xs]
    mt = [st_ref[j] for j in pairs]
    for sub in range(STEP_CHUNKS):
        rs = slice(sub * CHUNK, (sub + 1) * CHUNK)
        w_last = jnp.exp(cw[(sub + 1) * CHUNK - 1:(sub + 1) * CHUNK, :])
        ar = bf(jnp.concatenate([a_t[rs, sl], r_t[rs, sl]], axis=0) for sl in sls)
        qmat = bf(jnp.concatenate([b_t[rs, sl], k_t[rs, sl]], axis=0) for sl in sls)
        qbd = bf(jnp.concatenate([stack(b_t[rs, sl]), stack(k_t[rs, sl])], axis=0) for sl in sls)
        vbd = bf(stack(v[rs, sl]) for sl in sls)
        mtb = bf(mt)
        gh = [jnp.where(causal4, _dot(ar[j], qbd[j], _NT), 0.0) for j in pairs]
        n_ab = [g[:CHUNK, :LANES] for g in gh]
        a_rb = [g[CHUNK:, :LANES] for g in gh]
        akrk = bf(g[:, LANES:] for g in gh)
        sm = [_dot(ar[j], mtb[j], _NT) for j in pairs]
        av = [_dot(akrk[j], vbd[j]) for j in pairs]
        xm = [sm[j][:CHUNK] + av[j][:CHUNK] for j in pairs]
        tp = n_ab
        pwf = [_bdot(x, stack(x)) for x in n_ab]
        n_steps = int(math.log2(CHUNK)) - 1
        for step in range(n_steps):
            pbd = bf(stack(x) for x in pwf)
            if step + 1 < n_steps:
                both = [_bdot(jnp.concatenate([pwf[j], tp[j]], axis=0), pbd[j]) for j in pairs]
                tp = [tp[j] + pwf[j] + both[j][CHUNK:] for j in pairs]
                pwf = [both[j][:CHUNK] for j in pairs]
            else:
                tp = [tp[j] + pwf[j] + _bdot(tp[j], pbd[j]) for j in pairs]
        u = [xm[j] + _bdot(tp[j], stack(xm[j])) for j in pairs]
        for j in pairs:
            y_ref[rs, sls[j]] = sm[j][CHUNK:] + av[j][CHUNK:] + _bdot(a_rb[j], stack(u[j]))
        uv = [jnp.concatenate([u[j], v[rs, sls[j]]], axis=0) for j in pairs]
        upd = [jnp.where(same, _bdot(uv[j], qmat[j], _TN), 0.0) for j in pairs]
        mt = [(mt[j] + upd[j]) * w_last[:, sls[j]] for j in pairs]
    for j in pairs:
        st_ref[j] = mt[j]

    y = y_ref[...]
    inv_n = 1.0 / RWKV_HEAD
    yc = y - _sel_r(_sel_r(y, et_mat, 1) * inv_n, e_mat, 2)
    rs = lax.rsqrt(_sel_r(yc * yc, et_mat, 1) * inv_n + RWKV_GN_EPS)
    yn = yc * _sel_r(rs, e_mat, 2) * lnw_ref[...] + lnb_ref[...]
    bonus = _sel_r(_sel_r(r * k2 * rk_ref[...], et_mat, 1), e_mat, 2) * v
    o_ref[...] = ((yn + bonus) * gate).astype(o_ref.dtype)


def _rwkv(feat, mu, w0, w2, a0, a2, g2, k_k, k_a, r_k, ln_w, ln_b, batch):
    t = feat.shape[0]
    nc = t // batch // STEP_ROWS
    d = D_MODEL
    row = lambda w: pl.BlockSpec((STEP_ROWS, w), lambda b, c: (b * nc + c, 0))
    vec = lambda v: v.astype(F32).reshape(1, -1)
    w2p = jnp.concatenate([w2, jnp.zeros_like(w2)], axis=0).astype(BF16)
    a2p = jnp.concatenate([jnp.zeros_like(a2), a2], axis=0).astype(BF16)
    e_mat = _head_expand(RWKV_HEADS, RWKV_HEAD)
    return pl.pallas_call(
        _rwkv_kernel,
        grid=(batch, nc),
        in_specs=[row(RWKV_COLS), _full((1, RWKV_COLS)), _full((1, d)), _full((LANES, d)), _full((1, d)),
                  _full((LANES, d)), _full((LANES, d)), _full((1, d)), _full((1, d)), _full((1, d)),
                  _full((1, d)), _full((1, d)), _full((LANES, d)), _full((d, LANES))],
        out_specs=row(d),
        out_shape=jax.ShapeDtypeStruct((t, d), BF16),
        scratch_shapes=[pltpu.VMEM((STEP_ROWS + HIST, RWKV_COLS), F32),
                        pltpu.VMEM((RWKV_HEADS // 2, LANES, LANES), F32),
                        pltpu.VMEM((STEP_ROWS, d), F32)],
        compiler_params=_cparams(("parallel", "arbitrary")),
        name="rwkv7",
    )(feat, vec(mu), vec(w0), w2p, vec(a0), a2p, g2.astype(BF16), vec(k_k), vec(k_a), vec(r_k), vec(ln_w),
      vec(ln_b), e_mat, e_mat.T)


def _merge_kernel(ya_ref, yb_ref, yc_ref, g_ref, h_ref, wa_ref, wb_ref, wc_ref, wo_ref, lg_ref, lb_ref,
                  rw_ref, rb_ref, o_ref, gt_ref):
    d = D_MODEL
    gate = lambda j: _sigmoid(g_ref[:, j * d:(j + 1) * d].astype(F32))
    m = (gate(0) * _dot(ya_ref[...], wa_ref[...]) + gate(1) * _dot(yb_ref[...], wb_ref[...])
         + gate(2) * _dot(yc_ref[...], wc_ref[...]))
    hn = DN_ALPHA * h_ref[...] + _bdot(m, wo_ref[...])
    y = _layernorm_rows(hn, lg_ref[...], lb_ref[...])
    o_ref[...] = y
    _route(y, rw_ref, rb_ref, gt_ref)


def _merge(ya, yb, yc, gates, h, wa, wb, wc, wo, ln_g, ln_b, router_w, router_bias, tm=512):
    t, d = h.shape
    tm = min(tm, t)
    row = lambda w: pl.BlockSpec((tm, w), lambda i: (i, 0))
    return pl.pallas_call(
        _merge_kernel,
        grid=(t // tm,),
        in_specs=[row(d), row(d), row(d), row(3 * d), row(d)] + [_full((d, d))] * 4 + [_full((1, d))] * 2
        + [_full((N_EXPERTS, d)), _full((N_EXPERTS, 1))],
        out_specs=[row(d), pl.BlockSpec((N_EXPERTS, tm), lambda i: (0, i))],
        out_shape=[jax.ShapeDtypeStruct((t, d), F32), jax.ShapeDtypeStruct((N_EXPERTS, t), F32)],
        compiler_params=_cparams(("parallel",)),
        name="merge_ln_route",
    )(ya, yb, yc, gates, h, wa.astype(BF16), wb.astype(BF16), wc.astype(BF16), wo.astype(BF16),
      ln_g.reshape(1, d), ln_b.reshape(1, d), router_w.T, router_bias.reshape(N_EXPERTS, 1).astype(F32))


def _route(h, wt_ref, bias_ref, o_ref):
    logits = lax.dot_general(wt_ref[...], h, _NT, precision=lax.Precision.HIGHEST,
                             preferred_element_type=F32)
    mx = jnp.max(logits, axis=0, keepdims=True)
    ex = jnp.exp(logits - mx)
    probs = ex / jnp.sum(ex, axis=0, keepdims=True)
    sel = probs + bias_ref[...]
    rows = [sel[e:e + 1, :] for e in range(N_EXPERTS)]
    prow = [probs[e:e + 1, :] for e in range(N_EXPERTS)]
    gscore = []
    for g in range(N_GROUPS):
        m = rows[g * EXPERTS_PER_GROUP:(g + 1) * EXPERTS_PER_GROUP]
        best = None
        for i in range(EXPERTS_PER_GROUP):
            for j in range(i + 1, EXPERTS_PER_GROUP):
                s = m[i] + m[j]
                best = s if best is None else jnp.maximum(best, s)
        gscore.append(best)
    chosen = []
    for g in range(N_GROUPS):
        ok = None
        for g2 in range(N_GROUPS):
            if g2 == g:
                continue
            t = (gscore[g] > gscore[g2]) if g2 < g else (gscore[g] >= gscore[g2])
            ok = t if ok is None else (ok & t)
        chosen.append(ok)
    picked = []
    for e in range(N_EXPERTS):
        g = e // EXPERTS_PER_GROUP
        rank = None
        for e2 in range(g * EXPERTS_PER_GROUP, (g + 1) * EXPERTS_PER_GROUP):
            if e2 == e:
                continue
            ahead = (rows[e2] >= rows[e]) if e2 < e else (rows[e2] > rows[e])
            ahead = ahead.astype(F32)
            rank = ahead if rank is None else rank + ahead
        picked.append(jnp.where(chosen[g] & (rank < 2.0), prow[e], 0.0))
    tot = picked[0]
    for e in range(1, N_EXPERTS):
        tot = tot + picked[e]
    inv = 1.0 / tot
    for e in range(N_EXPERTS):
        o_ref[e:e + 1, :] = picked[e] * inv


MOE_TILE = 256
MOE_ISSUE_UNROLLS = (32, 4, 1)
_PAIRS = [(a, b) for a in range(EXPERTS_PER_GROUP) for b in range(a + 1, EXPERTS_PER_GROUP)]
N_CLASSES = N_GROUPS * len(_PAIRS)


def _moe_num_tiles(t):
    return -(-(t + N_CLASSES * (MOE_TILE - 1)) // MOE_TILE)


def _route_meta(gates_t):
    e_n, t = gates_t.shape
    i32 = jnp.int32
    mask = gates_t > 0
    eidx = lax.broadcasted_iota(i32, (e_n, t), 0)
    e0 = jnp.minimum(jnp.min(jnp.where(mask, eidx, e_n), axis=0), e_n - 1)
    e1 = jnp.maximum(jnp.max(jnp.where(mask, eidx, -1), axis=0), e0)
    grp = e0 // EXPERTS_PER_GROUP
    a = e0 % EXPERTS_PER_GROUP
    b = jnp.where(e1 // EXPERTS_PER_GROUP == grp, e1 % EXPERTS_PER_GROUP, a)
    b = jnp.where(b == a, (a + 1) % EXPERTS_PER_GROUP, b)
    lo, hi = jnp.minimum(a, b), jnp.maximum(a, b)
    n_pairs = len(_PAIRS)
    cls = grp * n_pairs + (lo * (2 * EXPERTS_PER_GROUP - 1 - lo)) // 2 + (hi - lo - 1)
    g_lo = jnp.take_along_axis(gates_t, (grp * EXPERTS_PER_GROUP + lo)[None, :], axis=0)[0]
    g_hi = jnp.take_along_axis(gates_t, (grp * EXPERTS_PER_GROUP + hi)[None, :], axis=0)[0]
    onehot = (cls[None, :] == jnp.arange(N_CLASSES, dtype=i32)[:, None]).astype(i32)
    csum = jnp.cumsum(onehot, axis=1)
    cnt = csum[:, -1]
    rank = jnp.sum(onehot * (csum - 1), axis=0)
    ptiles = (cnt + MOE_TILE - 1) // MOE_TILE
    tile_end = jnp.cumsum(ptiles)
    tile_off = tile_end - ptiles
    pos = jnp.take(tile_off, cls) * MOE_TILE + rank
    n_tiles = _moe_num_tiles(t)
    as_i32 = lambda g: lax.bitcast_convert_type(g.astype(F32), i32)
    per_token = jnp.stack([jnp.arange(t, dtype=i32), as_i32(g_lo), as_i32(g_hi)], axis=1)
    per_row = jnp.zeros((n_tiles * MOE_TILE, 3), i32).at[pos].set(per_token, unique_indices=True)
    row_token = per_row[:, 0]
    row_gates = lax.bitcast_convert_type(per_row[:, 1:3], F32)
    tid = jnp.arange(n_tiles, dtype=i32)
    tcls = jnp.minimum(jnp.sum((tid[:, None] >= tile_end[None, :]).astype(i32), axis=1), N_CLASSES - 1)
    n_valid = jnp.clip(jnp.take(cnt, tcls) - (tid - jnp.take(tile_off, tcls)) * MOE_TILE, 0, MOE_TILE)
    n_valid = jnp.where(tid < tile_end[-1], n_valid, 0).astype(i32)
    cls_lo = jnp.array([g * EXPERTS_PER_GROUP + p[0] for g in range(N_GROUPS) for p in _PAIRS], i32)
    cls_hi = jnp.array([g * EXPERTS_PER_GROUP + p[1] for g in range(N_GROUPS) for p in _PAIRS], i32)
    return row_token, jnp.take(cls_lo, tcls), jnp.take(cls_hi, tcls), n_valid, row_gates


def _moe_kernel(rt_ref, ea_ref, eb_ref, nv_ref, h_hbm, rg_ref, wga_ref, wua_ref, wda_ref, wgb_ref, wub_ref,
                wdb_ref, lg_ref, lb_ref, out_hbm, xbuf, obuf, wup_buf, wdn_buf, sem_in, sem_out):
    i = pl.program_id(0)
    n = pl.num_programs(0)
    slot = lax.rem(i, 2)

    def rows_in(row, s, k, rows=1):
        return pltpu.make_async_copy(h_hbm.at[pl.ds(row, rows)], xbuf.at[s, pl.ds(k, rows)], sem_in.at[s])

    def rows_out(row, s, k, rows=1):
        return pltpu.make_async_copy(obuf.at[s, pl.ds(k, rows)], out_hbm.at[pl.ds(row, rows)], sem_out.at[s])

    def issue(copy_at, tile, count):
        base = tile * MOE_TILE

        @pl.when(count == MOE_TILE)
        def _():
            for k in range(MOE_TILE):
                copy_at(base, k).start(priority=k % 2)

        @pl.when(count < MOE_TILE)
        def _():
            done = 0
            for unroll in MOE_ISSUE_UNROLLS:
                def group(q, c, unroll=unroll, done=done):
                    for r in range(unroll):
                        copy_at(base, done + q * unroll + r).start(priority=r % 2)
                    return c
                groups = lax.shift_right_logical(count - done, int(math.log2(unroll)))
                lax.fori_loop(0, groups, group, 0)
                done = done + groups * unroll

    def wait_rows(copies, count):
        rows = MOE_TILE
        while rows >= 1:
            @pl.when((count & rows) != 0)
            def _(rows=rows):
                copies(0, 0, rows).wait()
            rows //= 2

    def start_gather(tile, s):
        issue(lambda base, k: rows_in(rt_ref[base + k], s, k), tile, nv_ref[tile])

    @pl.when(i == 0)
    def _():
        xbuf[...] = jnp.zeros(xbuf.shape, F32)
        start_gather(0, 0)

    @pl.when(i + 1 < n)
    def _():
        start_gather(i + 1, 1 - slot)

    wait_rows(lambda row, k, rows: rows_in(row, slot, k, rows), nv_ref[i])

    @pl.when(i >= 2)
    def _():
        wait_rows(lambda row, k, rows: rows_out(row, slot, k, rows), nv_ref[i - 2])

    @pl.when(nv_ref[i] > 0)
    def _():
        prev = jnp.maximum(i - 1, 0)
        for which, e_ref, (g_ref, u_ref, d_ref) in ((0, ea_ref, (wga_ref, wua_ref, wda_ref)),
                                                    (1, eb_ref, (wgb_ref, wub_ref, wdb_ref))):
            @pl.when((i == 0) | (e_ref[i] != e_ref[prev]))
            def _(which=which, g_ref=g_ref, u_ref=u_ref, d_ref=d_ref):
                wup_buf[2 * which] = g_ref[0, 0].astype(BF16)
                wup_buf[2 * which + 1] = u_ref[0, 0].astype(BF16)
                wdn_buf[which] = d_ref[0, 0].astype(BF16)

        x = xbuf[slot]
        xb = x.astype(BF16)
        rg = rg_ref[...]
        act_a = (_silu(_dot(xb, wup_buf[0])) * _dot(xb, wup_buf[1])).astype(BF16)
        act_b = (_silu(_dot(xb, wup_buf[2])) * _dot(xb, wup_buf[3])).astype(BF16)
        moe = rg[:, 0:1] * _dot(act_a, wdn_buf[0]) + rg[:, 1:2] * _dot(act_b, wdn_buf[1])
        obuf[slot] = _layernorm_rows(DN_ALPHA * x + moe, lg_ref[...], lb_ref[...])
        issue(lambda base, k: rows_out(rt_ref[base + k], slot, k), i, nv_ref[i])

    @pl.when(i == n - 1)
    def _():
        @pl.when(i >= 1)
        def _():
            wait_rows(lambda row, k, rows: rows_out(row, 1 - slot, k, rows), nv_ref[i - 1])
        wait_rows(lambda row, k, rows: rows_out(row, slot, k, rows), nv_ref[i])


def _moe(h, gates_t, wg, wu, wd, layer, ln_g, ln_b):
    t, d = h.shape
    n_tiles = _moe_num_tiles(t)
    row_token, exp_a, exp_b, n_valid, row_gates = _route_meta(gates_t)
    w_up = lambda which: pl.BlockSpec((1, 1, d, D_EXPERT),
                                      lambda i, rt, ea, eb, nv: (layer, (ea, eb)[which][i], 0, 0))
    w_dn = lambda which: pl.BlockSpec((1, 1, D_EXPERT, d),
                                      lambda i, rt, ea, eb, nv: (layer, (ea, eb)[which][i], 0, 0))
    const = lambda shape: pl.BlockSpec(shape, lambda i, *_: (0,) * len(shape))
    grid_spec = pltpu.PrefetchScalarGridSpec(
        num_scalar_prefetch=4,
        grid=(n_tiles,),
        in_specs=[pl.BlockSpec(memory_space=pl.ANY),
                  pl.BlockSpec((MOE_TILE, 2), lambda i, *_: (i, 0)),
                  w_up(0), w_up(0), w_dn(0), w_up(1), w_up(1), w_dn(1), const((1, d)), const((1, d))],
        out_specs=pl.BlockSpec(memory_space=pl.ANY),
        scratch_shapes=[pltpu.VMEM((2, MOE_TILE, d), F32), pltpu.VMEM((2, MOE_TILE, d), F32),
                        pltpu.VMEM((4, d, D_EXPERT), BF16), pltpu.VMEM((2, D_EXPERT, d), BF16),
                        pltpu.SemaphoreType.DMA((2,)), pltpu.SemaphoreType.DMA((2,))],
    )
    return pl.pallas_call(
        _moe_kernel,
        grid_spec=grid_spec,
        out_shape=jax.ShapeDtypeStruct((t, d), F32),
        compiler_params=_cparams(("arbitrary",)),
        name="experts_ln",
    )(row_token, exp_a, exp_b, n_valid, h, row_gates, wg, wu, wd, wg, wu, wd, ln_g.reshape(1, d), ln_b.reshape(1, d))


def kernel(x, ln_in_g, ln_in_b, w_in, ssd_conv_w, ssd_conv_b, ssd_dt_bias, ssd_a_log, ssd_d, ssd_norm_w, rwkv_mu, rwkv_w0, rwkv_w2, rwkv_a0, rwkv_a2, rwkv_g2, rwkv_k_k, rwkv_k_a, rwkv_r_k, rwkv_ln_w, rwkv_ln_b, hgrn_lb, hgrn_norm_w, w_br_ssd, w_br_rwkv, w_br_hgrn, w_out, ln1_g, ln1_b, router_w, router_bias, exp_w_gate, exp_w_up, exp_w_down, ln2_g, ln2_b):
    batch, seq, d = x.shape
    t = batch * seq
    h, h_bf = _layernorm(x.reshape(t, d), ln_in_g, ln_in_b)
    lsm = jax.nn.softmax(hgrn_lb.astype(F32), axis=0)
    lower_bounds = jnp.cumsum(lsm, axis=0) - lsm[0]
    w_all = _pack_in_weights(w_in)
    for l in range(DEPTH):
        z, dt_raw = _project_z_dt(h_bf, w_all, l)
        xbc, f_rwkv, f_hgrn, gates = [_project(h_bf, w_all, l, seg) for seg in ("xbc", "rwkv", "hgrn", "gates")]
        y_a = _ssd(z, xbc, dt_raw, ssd_conv_w[l], ssd_conv_b[l], ssd_dt_bias[l], ssd_a_log[l], ssd_d[l],
                   ssd_norm_w[l], batch)
        y_b = _rwkv(f_rwkv, rwkv_mu[l], rwkv_w0[l], rwkv_w2[l], rwkv_a0[l], rwkv_a2[l], rwkv_g2[l],
                    rwkv_k_k[l], rwkv_k_a[l], rwkv_r_k[l].reshape(-1), rwkv_ln_w[l], rwkv_ln_b[l], batch)
        y_c = _hgrn(f_hgrn, lower_bounds[l], hgrn_norm_w[l], batch)
        h, gates_t = _merge(y_a, y_b, y_c, gates, h, w_br_ssd[l], w_br_rwkv[l], w_br_hgrn[l], w_out[l],
                            ln1_g[l], ln1_b[l], router_w, router_bias)
        h = _moe(h, gates_t, exp_w_gate, exp_w_up, exp_w_down, l, ln2_g[l], ln2_b[l])
        h_bf = h.astype(BF16)
    return h.reshape(batch, seq, d)
```
